```python
import math
import jax, jax.numpy as jnp
from jax import lax
import numpy as np


D_MODEL = 1024
BATCH = 8
SEQ = 2048
DEPTH = 2
DEC_BATCH = 128
DEC_SEQ = 8
PAST_LEN = 16384
PAGE_SIZE = 128

N_META = 16
D_MIX = D_MODEL
D_LRU = D_MIX // 2
LRU_BLOCKS = 8
LRU_BW = D_LRU // LRU_BLOCKS
CONV_W = 4
LRU_C = 8.0
GLA_HEADS = 4
GLA_DV = (D_MIX - D_LRU) // GLA_HEADS
GLA_DK = GLA_DV // 2
GLA_RANK = 16
GLA_GATE_NORM = 16.0
GLA_CHUNK = 64
D_FF = 128 * ((8 * D_MODEL // 3 + 127) // 128)
EPS = 1e-6
D_IN = 2 * D_LRU + 2 * GLA_HEADS * GLA_DK + 2 * GLA_HEADS * GLA_DV + GLA_RANK

kernel_name = "hymba_rglru_gla_macaron_step"


def rmsnorm(x, g):
    xf = x.astype(jnp.float32)
    y = xf * lax.rsqrt(jnp.mean(xf * xf, axis=-1, keepdims=True) + EPS)
    return (y * g.astype(jnp.float32)).astype(x.dtype)


def swiglu(x, w_gu, w_down):
    gate, up = jnp.split(x @ w_gu, 2, axis=-1)
    return (jax.nn.silu(gate) * up) @ w_down


def causal_conv(x, buf, w, b):
    L = x.shape[1]
    xp = jnp.concatenate([buf.astype(x.dtype), x], axis=1)
    y = b
    for k in range(CONV_W):
        y = y + w[k] * xp[:, k:k + L]
    return y, xp[:, -(CONV_W - 1):]


def rg_lru(x, h0, pos, wa, ba, wx, bx, lam):
    B, L, _ = x.shape
    xb = x.reshape(B, L, LRU_BLOCKS, LRU_BW)
    r = jax.nn.sigmoid(jnp.einsum('blnc,ncd->blnd', xb, wa).reshape(B, L, D_LRU) + ba)
    i = jax.nn.sigmoid(jnp.einsum('blnc,ncd->blnd', xb, wx).reshape(B, L, D_LRU) + bx)
    log_a = -LRU_C * r.astype(jnp.float32) * jax.nn.softplus(-lam.astype(jnp.float32))
    a = jnp.exp(log_a)
    mult = jnp.sqrt(-jnp.expm1(2.0 * log_a))
    reset = (pos == 0)[None, :, None]
    mult = jnp.where(reset, 1.0, mult)
    a = jnp.where(reset, 0.0, a)
    bt = mult * (i * x).astype(jnp.float32)
    bt = bt.at[:, 0].add(a[:, 0] * h0.astype(jnp.float32))

    def combine(c1, c2):
        a1, b1 = c1
        a2, b2 = c2
        return a1 * a2, a2 * b1 + b2

    _, h = lax.associative_scan(combine, (a, bt), axis=1)
    return h.astype(x.dtype), h[:, -1].astype(x.dtype)


def gla_chunked(q, k, v, g, S0):
    B, L, H, _ = q.shape
    C = math.gcd(L, GLA_CHUNK)
    N = L // C

    def to_chunks(t):
        return t.astype(jnp.float32).reshape(B, N, C, H, t.shape[-1]).transpose(1, 0, 3, 2, 4)

    causal = jnp.tril(jnp.ones((C, C), dtype=bool))

    def step(S, inp):
        qc, kc, vc, gc = inp
        b = jnp.cumsum(gc, axis=2)
        b_last = b[:, :, -1:, :]
        q_s = qc * jnp.exp(b)
        k_s = kc * jnp.exp(-b)
        k_end = kc * jnp.exp(b_last - b)
        att = jnp.where(causal, jnp.einsum('bhik,bhjk->bhij', q_s, k_s), 0.0)
        o = jnp.einsum('bhik,bhkv->bhiv', q_s, S) + jnp.einsum('bhij,bhjv->bhiv', att, vc)
        S = jnp.swapaxes(jnp.exp(b_last), -1, -2) * S + jnp.einsum('bhjk,bhjv->bhkv', k_end, vc)
        return S, o

    S, o = lax.scan(step, S0.astype(jnp.float32), (to_chunks(q), to_chunks(k), to_chunks(v), to_chunks(g)))
    o = o.transpose(1, 0, 3, 2, 4).reshape(B, L, H, GLA_DV)
    return o, S


def mixer(hn, conv_buf, h0, S0, pos, n_meta, w_in, conv_w, conv_b, wa, ba, wx, bx, lam,
          w_gate2, b_gate, gla_norm, w_out):
    B, L, _ = hn.shape
    u = hn @ w_in
    hk = GLA_HEADS * GLA_DK
    hv = GLA_HEADS * GLA_DV
    splits = [D_LRU, 2 * D_LRU, 2 * D_LRU + hk, 2 * D_LRU + 2 * hk,
              2 * D_LRU + 2 * hk + hv, 2 * D_LRU + 2 * hk + 2 * hv]
    xl, gl, q, k, v, go, lr = jnp.split(u, splits, axis=-1)
    xc, new_buf = causal_conv(xl, conv_buf, conv_w, conv_b)
    h, h_last = rg_lru(xc, h0, pos, wa, ba, wx, bx, lam)
    lru_out = h * jax.nn.gelu(gl)
    q = q.reshape(B, L, GLA_HEADS, GLA_DK) * (GLA_DK ** -0.5)
    k = k.reshape(B, L, GLA_HEADS, GLA_DK)
    v = v.reshape(B, L, GLA_HEADS, GLA_DV)
    g = jax.nn.log_sigmoid((lr @ w_gate2 + b_gate).astype(jnp.float32)) / GLA_GATE_NORM
    g = g.reshape(B, L, GLA_HEADS, GLA_DK)
    if n_meta > 0:
        o_m, S = gla_chunked(q[:, :n_meta], k[:, :n_meta], v[:, :n_meta], g[:, :n_meta], S0)
        o_r, S = gla_chunked(q[:, n_meta:], k[:, n_meta:], v[:, n_meta:], g[:, n_meta:], S)
        o = jnp.concatenate([o_m, o_r], axis=1)
    else:
        o, S = gla_chunked(q, k, v, g, S0)
    o = rmsnorm(o.astype(hn.dtype), gla_norm).reshape(B, L, hv)
    gla_out = o * jax.nn.silu(go)
    y = jnp.concatenate([lru_out, gla_out], axis=-1) @ w_out
    return y, new_buf, h_last, S.astype(hn.dtype)


def run_trunk(x, conv0, h0, S0, pos, n_meta, weights):
    (norm_ffn1, w_ffn1_gu, w_ffn1_down, norm_mix, w_in, lru_conv_w, lru_conv_b,
     lru_wa, lru_ba, lru_wx, lru_bx, lru_lambda, gla_w_gate2, gla_b_gate, gla_norm,
     w_out, norm_ffn2, w_ffn2_gu, w_ffn2_down, norm_final) = weights
    convs, hs, Ss = [], [], []
    for l in range(DEPTH):
        x = x + 0.5 * swiglu(rmsnorm(x, norm_ffn1[l]), w_ffn1_gu[l], w_ffn1_down[l])
        m, cb, hl, S = mixer(rmsnorm(x, norm_mix[l]), conv0[l], h0[l], S0[l], pos, n_meta,
                             w_in[l], lru_conv_w[l], lru_conv_b[l], lru_wa[l], lru_ba[l],
                             lru_wx[l], lru_bx[l], lru_lambda[l], gla_w_gate2[l], gla_b_gate[l],
                             gla_norm[l], w_out[l])
        x = x + m
        x = x + 0.5 * swiglu(rmsnorm(x, norm_ffn2[l]), w_ffn2_gu[l], w_ffn2_down[l])
        convs.append(cb)
        hs.append(hl)
        Ss.append(S)
    return rmsnorm(x, norm_final), jnp.stack(hs), jnp.stack(convs), jnp.stack(Ss)


def setup_inputs(seed: int = 0) -> dict:
    key = jax.random.key(seed)
    ks = jax.random.split(key, 40)
    f32 = jnp.float32

    def nrm(k, shape, scale):
        return jax.random.normal(k, shape, f32) * scale

    a0 = jax.random.uniform(ks[14], (DEPTH, D_LRU), f32, minval=0.9, maxval=0.999)
    return {
        "x_prompt": nrm(ks[0], (BATCH, SEQ, D_MODEL), 1.0),
        "x_sample": nrm(ks[1], (DEC_BATCH, DEC_SEQ, D_MODEL), 1.0),
        "state_lru_h": nrm(ks[2], (DEPTH, DEC_BATCH, D_LRU), 0.5),
        "state_lru_conv": nrm(ks[3], (DEPTH, DEC_BATCH, CONV_W - 1, D_LRU), 1.0),
        "state_gla_S": nrm(ks[4], (DEPTH, DEC_BATCH, GLA_HEADS, GLA_DK, GLA_DV), 0.3),
        "meta": nrm(ks[5], (N_META, D_MODEL), 1.0),
        "norm_ffn1": 1.0 + nrm(ks[6], (DEPTH, D_MODEL), 0.02),
        "w_ffn1_gu": nrm(ks[7], (DEPTH, D_MODEL, 2 * D_FF), D_MODEL ** -0.5),
        "w_ffn1_down": nrm(ks[8], (DEPTH, D_FF, D_MODEL), D_FF ** -0.5),
        "norm_mix": 1.0 + nrm(ks[9], (DEPTH, D_MODEL), 0.02),
        "w_in": nrm(ks[10], (DEPTH, D_MODEL, D_IN), D_MODEL ** -0.5),
        "lru_conv_w": nrm(ks[11], (DEPTH, CONV_W, D_LRU), CONV_W ** -0.5),
        "lru_conv_b": nrm(ks[12], (DEPTH, D_LRU), 0.01),
        "lru_wa": nrm(ks[13], (DEPTH, LRU_BLOCKS, LRU_BW, LRU_BW), LRU_BW ** -0.5),
        "lru_ba": nrm(ks[15], (DEPTH, D_LRU), 0.01),
        "lru_wx": nrm(ks[16], (DEPTH, LRU_BLOCKS, LRU_BW, LRU_BW), LRU_BW ** -0.5),
        "lru_bx": nrm(ks[17], (DEPTH, D_LRU), 0.01),
        "lru_lambda": jnp.log(a0) - jnp.log1p(-a0),
        "gla_w_gate2": nrm(ks[18], (DEPTH, GLA_RANK, GLA_HEADS * GLA_DK), GLA_RANK ** -0.5),
        "gla_b_gate": nrm(ks[19], (DEPTH, GLA_HEADS * GLA_DK), 0.1),
        "gla_norm": 1.0 + nrm(ks[20], (DEPTH, GLA_DV), 0.02),
        "w_out": nrm(ks[21], (DEPTH, D_MIX, D_MODEL), D_MIX ** -0.5),
        "norm_ffn2": 1.0 + nrm(ks[22], (DEPTH, D_MODEL), 0.02),
        "w_ffn2_gu": nrm(ks[23], (DEPTH, D_MODEL, 2 * D_FF), D_MODEL ** -0.5),
        "w_ffn2_down": nrm(ks[24], (DEPTH, D_FF, D_MODEL), D_FF ** -0.5),
        "norm_final": 1.0 + nrm(ks[25], (D_MODEL,), 0.02),
    }


def reference(x_prompt, x_sample, state_lru_h, state_lru_conv, state_gla_S, meta,
              norm_ffn1, w_ffn1_gu, w_ffn1_down, norm_mix, w_in, lru_conv_w, lru_conv_b,
              lru_wa, lru_ba, lru_wx, lru_bx, lru_lambda, gla_w_gate2, gla_b_gate, gla_norm,
              w_out, norm_ffn2, w_ffn2_gu, w_ffn2_down, norm_final):
    weights = (norm_ffn1, w_ffn1_gu, w_ffn1_down, norm_mix, w_in, lru_conv_w, lru_conv_b,
               lru_wa, lru_ba, lru_wx, lru_bx, lru_lambda, gla_w_gate2, gla_b_gate, gla_norm,
               w_out, norm_ffn2, w_ffn2_gu, w_ffn2_down, norm_final)
    B, S_len, D = x_prompt.shape
    xp = jnp.concatenate([jnp.broadcast_to(meta.astype(x_prompt.dtype), (B, N_META, D)), x_prompt], axis=1)
    pos_p = jnp.arange(N_META + S_len)
    conv0 = jnp.zeros((DEPTH, B, CONV_W - 1, D_LRU), x_prompt.dtype)
    h0 = jnp.zeros((DEPTH, B, D_LRU), x_prompt.dtype)
    S0 = jnp.zeros((DEPTH, B, GLA_HEADS, GLA_DK, GLA_DV), x_prompt.dtype)
    yp, h_p, conv_p, S_p = run_trunk(xp, conv0, h0, S0, pos_p, N_META, weights)
    y_prompt = yp[:, N_META:]
    pos_s = PAST_LEN + jnp.arange(x_sample.shape[1])
    y_sample, h_s, conv_s, S_s = run_trunk(x_sample, state_lru_conv, state_lru_h, state_gla_S,
                                           pos_s, 0, weights)
    return (y_prompt, y_sample, h_p, conv_p, S_p, h_s, conv_s, S_s)
```

```python
import functools

import jax
import jax.numpy as jnp
from jax import lax
from jax.experimental import pallas as pl
from jax.experimental.pallas import tpu as pltpu

F32 = jnp.float32
BF16 = jnp.bfloat16

D_MODEL = 1024
D_FF = 2816
D_LRU = 512
LRU_BLOCKS = 8
LRU_BW = 64
CONV_W = 4
LRU_C = 8.0
GLA_HEADS = 4
GLA_DV = 128
GLA_DK = 64
GLA_RANK = 16
GLA_GATE_NORM = 16.0
GLA_CHUNK = 64
EPS = 1e-6
HK = GLA_HEADS * GLA_DK
HV = GLA_HEADS * GLA_DV
D_MAIN = 2 * D_LRU + 2 * HK + 2 * HV
D_U = D_MAIN + HK
SUBLANES = 8
LANES = 128
VMEM_LIMIT = 56 * 1024 * 1024


def _dot(a, b):
    return jnp.dot(a, b, preferred_element_type=F32)


def _dot_nt(a, b):
    return lax.dot_general(a, b, (((1,), (1,)), ((), ())), preferred_element_type=F32)


def _dot_tn(a, b):
    return lax.dot_general(a, b, (((0,), (0,)), ((), ())), preferred_element_type=F32)


def _rms(x, g):
    ms = jnp.mean(x * x, axis=-1, keepdims=True)
    return (x * lax.rsqrt(ms + EPS)) * g


def _token_kernel(*refs, has_pre, has_post, has_final, tf):
    it = iter(refs)
    x_ref = next(it)
    if has_pre:
        z_ref, wout_ref = next(it), next(it)
    g_ref, wgu_ref, wd_ref = next(it), next(it), next(it)
    if has_post:
        gmix_ref, wmain_ref, wlr_ref, wg2_ref, bg_ref = (next(it) for _ in range(5))
    if has_final:
        gfin_ref = next(it)
    xo_ref = next(it)
    if has_post:
        u_ref = next(it)

    x = x_ref[...]
    if has_pre:
        x = x + _dot(z_ref[...].astype(BF16), wout_ref[...])
    xn = _rms(x, g_ref[...]).astype(BF16)
    acc = None
    for c in range(D_FF // tf):
        gate = _dot(xn, wgu_ref[:, c * tf:(c + 1) * tf])
        up = _dot(xn, wgu_ref[:, D_FF + c * tf:D_FF + (c + 1) * tf])
        h = (jax.nn.silu(gate) * up).astype(BF16)
        d = _dot(h, wd_ref[c * tf:(c + 1) * tf, :])
        acc = d if acc is None else acc + d
    x = x + 0.5 * acc
    if has_final:
        xo_ref[...] = _rms(x, gfin_ref[...])
    else:
        xo_ref[...] = x
    if has_post:
        hn = _rms(x, gmix_ref[...]).astype(BF16)
        u_ref[:, 0:D_MAIN] = _dot(hn, wmain_ref[...])
        lr = _dot(hn, wlr_ref[...]).astype(BF16)
        zg = _dot(lr, wg2_ref[...]) + bg_ref[...]
        u_ref[:, D_MAIN:D_U] = jax.nn.log_sigmoid(zg) / GLA_GATE_NORM


def _resident(shape):
    nd = len(shape)
    return pl.BlockSpec(shape, lambda *_: (0,) * nd, pipeline_mode=pl.Buffered(1))


def _token_call(x, ffn, *, pre=None, post=None, final=None, tm, tf, name):
    t, d = x.shape
    assert t % tm == 0 and D_FF % tf == 0
    row = lambda w: pl.BlockSpec((tm, w), lambda i: (i, 0))
    args, specs = [x], [row(d)]
    if pre is not None:
        z, wout = pre
        args += [z, wout]
        specs += [row(z.shape[1]), _resident(wout.shape)]
    args += list(ffn)
    specs += [_resident(a.shape) for a in ffn]
    if post is not None:
        args += list(post)
        specs += [_resident(a.shape) for a in post]
    if final is not None:
        args.append(final)
        specs.append(_resident(final.shape))
    out_shape = [jax.ShapeDtypeStruct((t, d), F32)]
    out_specs = [row(d)]
    if post is not None:
        out_shape.append(jax.ShapeDtypeStruct((t, D_U), F32))
        out_specs.append(row(D_U))
    kern = functools.partial(_token_kernel, has_pre=pre is not None, has_post=post is not None,
                             has_final=final is not None, tf=tf)
    outs = pl.pallas_call(
        kern, grid=(t // tm,), in_specs=specs, out_specs=out_specs, out_shape=out_shape,
        compiler_params=pltpu.CompilerParams(dimension_semantics=("parallel",),
                                             vmem_limit_bytes=VMEM_LIMIT),
        name=name,
    )(*args)
    return outs if post is not None else outs[0]


def _scan_rows(a, b, n):
    row = lax.broadcasted_iota(jnp.int32, a.shape, 0)
    s = 1
    while s < n:
        keep = row >= s
        b = jnp.where(keep, a * pltpu.roll(b, s, 0) + b, b)
        if 2 * s < n:
            a = jnp.where(keep, a * pltpu.roll(a, s, 0), a)
        s *= 2
    return b


def _cumsum_rows(x, n):
    row = lax.broadcasted_iota(jnp.int32, x.shape, 0)
    s = 1
    while s < n:
        x = x + jnp.where(row >= s, pltpu.roll(x, s, 0), 0.0)
        s *= 2
    return x


def _core_kernel(u_ref, frame0_ref, h0_ref, s0_ref, convw_ref, convb_ref, wg_ref, ba_ref, bx_ref,
                 lam_ref, gn_ref, z_ref, hl_ref, so_ref, tail_sc, h_sc, s_sc, *, tc, cg, reset_first):
    j = pl.program_id(1)

    @pl.when(j == 0)
    def _():
        tail_sc[...] = frame0_ref[...]
        h_sc[...] = h0_ref[...]
        s_sc[...] = s0_ref[...]

    xl = u_ref[:, 0:D_LRU]
    tail = tail_sc[...]
    row8 = lax.broadcasted_iota(jnp.int32, (SUBLANES, D_LRU), 0)
    xc = convb_ref[...]
    for s in range(CONV_W - 1, -1, -1):
        if s == 0:
            sh = xl
        else:
            rolled = pltpu.roll(xl, s, 0)
            top = jnp.where(row8 < s, pltpu.roll(tail, s, 0), rolled[0:SUBLANES])
            sh = top if tc == SUBLANES else jnp.concatenate([top, rolled[SUBLANES:]], axis=0)
        xc = xc + convw_ref[CONV_W - 1 - s:CONV_W - s, :] * sh
    tail_sc[...] = xl[tc - SUBLANES:tc]

    xcb = xc.astype(BF16)
    half = D_LRU // 2
    p0 = _dot(xcb[:, 0:half], wg_ref[0])
    p1 = _dot(xcb[:, half:D_LRU], wg_ref[1])
    r = jax.nn.sigmoid(jnp.concatenate([p0[:, 0:half], p1[:, 0:half]], axis=1) + ba_ref[...])
    i = jax.nn.sigmoid(jnp.concatenate([p0[:, half:], p1[:, half:]], axis=1) + bx_ref[...])
    log_a = -LRU_C * r * jax.nn.softplus(-lam_ref[...])
    a = jnp.exp(log_a)
    mult = jnp.sqrt(-jnp.tanh(log_a) * (a * a + 1.0))
    row = lax.broadcasted_iota(jnp.int32, (tc, D_LRU), 0)
    if reset_first:
        first = jnp.logical_and(row == 0, j == 0)
        mult = jnp.where(first, 1.0, mult)
        a = jnp.where(first, 0.0, a)
    bt = mult * (i * xc)
    bt = bt + jnp.where(row == 0, a * h_sc[...], 0.0)
    h = _scan_rows(a, bt, tc)
    h_sc[...] = h[tc - 1:tc]
    hl_ref[...] = h[tc - 1:tc]
    z_ref[:, 0:D_LRU] = h * jax.nn.gelu(u_ref[:, D_LRU:2 * D_LRU])

    o0 = 2 * D_LRU
    causal = (lax.broadcasted_iota(jnp.int32, (cg, cg), 0)
              >= lax.broadcasted_iota(jnp.int32, (cg, cg), 1))
    gn = gn_ref[...]
    for c in range(tc // cg):
        rs = slice(c * cg, (c + 1) * cg)
        b = _cumsum_rows(u_ref[rs, D_MAIN:D_U], cg)
        bl = b[cg - 1:cg]
        q = u_ref[rs, o0:o0 + HK] * (GLA_DK ** -0.5)
        k = u_ref[rs, o0 + HK:o0 + 2 * HK]
        q_s = (q * jnp.exp(b)).astype(BF16)
        k_s = (k * jnp.exp(-b)).astype(BF16)
        k_end = (k * jnp.exp(bl - b)).astype(BF16)
        el_t = jnp.broadcast_to(jnp.exp(bl), (LANES, HK)).T
        for hd in range(GLA_HEADS):
            ks = slice(hd * GLA_DK, (hd + 1) * GLA_DK)
            vs = slice(o0 + 2 * HK + hd * GLA_DV, o0 + 2 * HK + (hd + 1) * GLA_DV)
            v = u_ref[rs, vs].astype(BF16)
            att = jnp.where(causal, _dot_nt(q_s[:, ks], k_s[:, ks]), 0.0)
            s_h = s_sc[hd]
            o = _dot(q_s[:, ks], s_h.astype(BF16)) + _dot(att.astype(BF16), v)
            s_sc[hd] = el_t[ks, :] * s_h + _dot_tn(k_end[:, ks], v)
            on = _rms(o, gn)
            go = u_ref[rs, o0 + 2 * HK + HV + hd * GLA_DV:o0 + 2 * HK + HV + (hd + 1) * GLA_DV]
            z_ref[rs, D_LRU + hd * GLA_DV:D_LRU + (hd + 1) * GLA_DV] = on * jax.nn.silu(go)
    so_ref[...] = s_sc[...]


def _core_call(u, frame0, h0, s0, params, *, tc, cg, reset_first, shared_state, name):
    nb, sl, _ = u.shape
    assert sl % tc == 0 and tc % cg == 0 and tc % SUBLANES == 0
    if shared_state:
        st = lambda b, j: (0, 0, 0)
        st4 = lambda b, j: (0, 0, 0, 0)
    else:
        st = lambda b, j: (b, 0, 0)
        st4 = lambda b, j: (b, 0, 0, 0)
    const = lambda a: pl.BlockSpec(a.shape, lambda b, j: (0,) * a.ndim)
    in_specs = [
        pl.BlockSpec((None, tc, D_U), lambda b, j: (b, j, 0)),
        pl.BlockSpec((None, SUBLANES, D_LRU), st),
        pl.BlockSpec((None, 1, D_LRU), st),
        pl.BlockSpec((None, GLA_HEADS, GLA_DK, GLA_DV), st4),
    ] + [const(a) for a in params]
    out_shape = [
        jax.ShapeDtypeStruct((nb, sl, D_MODEL), F32),
        jax.ShapeDtypeStruct((nb, 1, D_LRU), F32),
        jax.ShapeDtypeStruct((nb, GLA_HEADS, GLA_DK, GLA_DV), F32),
    ]
    out_specs = [
        pl.BlockSpec((None, tc, D_MODEL), lambda b, j: (b, j, 0)),
        pl.BlockSpec((None, 1, D_LRU), lambda b, j: (b, 0, 0)),
        pl.BlockSpec((None, GLA_HEADS, GLA_DK, GLA_DV), lambda b, j: (b, 0, 0, 0)),
    ]
    scratch = [
        pltpu.VMEM((SUBLANES, D_LRU), F32),
        pltpu.VMEM((1, D_LRU), F32),
        pltpu.VMEM((GLA_HEADS, GLA_DK, GLA_DV), F32),
    ]
    kern = functools.partial(_core_kernel, tc=tc, cg=cg, reset_first=reset_first)
    return pl.pallas_call(
        kern, grid=(nb, sl // tc), in_specs=in_specs, out_specs=out_specs, out_shape=out_shape,
        scratch_shapes=scratch,
        compiler_params=pltpu.CompilerParams(dimension_semantics=("parallel", "arbitrary"),
                                             vmem_limit_bytes=VMEM_LIMIT),
        name=name,
    )(u, frame0, h0, s0, *params)


def _block_diag_gates(wa, wx):
    def bd(w4):
        return jax.scipy.linalg.block_diag(*[w4[n] for n in range(w4.shape[0])])
    halves = []
    per = LRU_BLOCKS // 2
    for hf in range(2):
        halves.append(jnp.concatenate([bd(wa[hf * per:(hf + 1) * per]),
                                       bd(wx[hf * per:(hf + 1) * per])], axis=1))
    return jnp.stack(halves).astype(BF16)


def _conv_frame(buf):
    return jnp.pad(buf, ((0, 0), (SUBLANES - (CONV_W - 1), 0), (0, 0)))


def kernel(x_prompt, x_sample, state_lru_h, state_lru_conv, state_gla_S, meta, norm_ffn1, w_ffn1_gu,
           w_ffn1_down, norm_mix, w_in, lru_conv_w, lru_conv_b, lru_wa, lru_ba, lru_wx, lru_bx,
           lru_lambda, gla_w_gate2, gla_b_gate, gla_norm, w_out, norm_ffn2, w_ffn2_gu, w_ffn2_down,
           norm_final):
    bp, lp, d = x_prompt.shape
    bs, ls, _ = x_sample.shape
    n_meta = meta.shape[0]
    depth = w_in.shape[0]
    assert d == D_MODEL and ls == SUBLANES and n_meta % SUBLANES == 0
    tp, ts = bp * lp, bs * ls
    tm_p = 512 if tp % 512 == 0 else lp
    tc_p = 256 if lp % 256 == 0 else GLA_CHUNK

    xp = x_prompt.reshape(tp, d)
    xs = jnp.concatenate([x_sample.reshape(ts, d), meta.astype(F32)], axis=0)
    tsm = ts + n_meta
    row = lambda a: a.reshape(1, -1)

    hs_p, convs_p, ss_p, hs_s, convs_s, ss_s = [], [], [], [], [], []
    for l in range(depth):
        ffn1 = (row(norm_ffn1[l]), w_ffn1_gu[l].astype(BF16), w_ffn1_down[l].astype(BF16))
        ffn2 = (row(norm_ffn2[l]), w_ffn2_gu[l].astype(BF16), w_ffn2_down[l].astype(BF16))
        w_lr = jnp.pad(w_in[l][:, D_MAIN:], ((0, 0), (0, LANES - GLA_RANK))).astype(BF16)
        w_g2 = jnp.pad(gla_w_gate2[l], ((0, LANES - GLA_RANK), (0, 0))).astype(BF16)
        post = (row(norm_mix[l]), w_in[l][:, :D_MAIN].astype(BF16), w_lr, w_g2, row(gla_b_gate[l]))
        core_params = (lru_conv_w[l], row(lru_conv_b[l]), _block_diag_gates(lru_wa[l], lru_wx[l]),
                       row(lru_ba[l]), row(lru_bx[l]), row(lru_lambda[l]), row(gla_norm[l]))
        wout = w_out[l].astype(BF16)
        fin = row(norm_final) if l == depth - 1 else None

        xp1, up = _token_call(xp, ffn1, post=post, tm=tm_p, tf=256, name=f"tokA_p{l}")
        xs1, us = _token_call(xs, ffn1, post=post, tm=tsm, tf=256, name=f"tokA_s{l}")

        u_m = us[ts:].reshape(1, n_meta, D_U)
        z_m, h_m, s_m = _core_call(
            u_m, jnp.zeros((1, SUBLANES, D_LRU), F32), jnp.zeros((1, 1, D_LRU), F32),
            jnp.zeros((1, GLA_HEADS, GLA_DK, GLA_DV), F32), core_params,
            tc=n_meta, cg=n_meta, reset_first=True, shared_state=False, name=f"core_m{l}")
        frame_m = _conv_frame(u_m[:, n_meta - (CONV_W - 1):, :D_LRU])

        up3 = up.reshape(bp, lp, D_U)
        z_p, h_p, s_p = _core_call(up3, frame_m, h_m, s_m, core_params, tc=tc_p, cg=GLA_CHUNK,
                                   reset_first=False, shared_state=True, name=f"core_p{l}")
        hs_p.append(h_p[:, 0])
        convs_p.append(up3[:, lp - (CONV_W - 1):, :D_LRU])
        ss_p.append(s_p)

        us3 = us[:ts].reshape(bs, ls, D_U)
        z_s, h_s, s_s = _core_call(us3, _conv_frame(state_lru_conv[l]), state_lru_h[l][:, None, :],
                                   state_gla_S[l], core_params, tc=ls, cg=ls, reset_first=False,
                                   shared_state=False, name=f"core_s{l}")
        hs_s.append(h_s[:, 0])
        convs_s.append(us3[:, ls - (CONV_W - 1):, :D_LRU])
        ss_s.append(s_s)

        z_sm = jnp.concatenate([z_s.reshape(ts, d), z_m.reshape(n_meta, d)], axis=0)
        xp = _token_call(xp1, ffn2, pre=(z_p.reshape(tp, d), wout), final=fin, tm=tm_p, tf=256,
                         name=f"tokB_p{l}")
        xs = _token_call(xs1, ffn2, pre=(z_sm, wout), final=fin, tm=tsm, tf=256, name=f"tokB_s{l}")

    return (xp.reshape(bp, lp, d), xs[:ts].reshape(bs, ls, d),
            jnp.stack(hs_p), jnp.stack(convs_p), jnp.stack(ss_p),
            jnp.stack(hs_s), jnp.stack(convs_s), jnp.stack(ss_s))
```

```python
import functools

import jax
import jax.numpy as jnp
from jax import lax
from jax.experimental import pallas as pl
from jax.experimental.pallas import tpu as pltpu

F32 = jnp.float32

D_MODEL = 1024
D_FF = 2816
D_LRU = 512
LRU_BLOCKS = 8
CONV_W = 4
LRU_C = 8.0
GLA_HEADS = 4
GLA_DV = 128
GLA_DK = 64
GLA_RANK = 16
GLA_GATE_NORM = 16.0
GLA_CHUNK = 64
EPS = 1e-6
HK = GLA_HEADS * GLA_DK
HV = GLA_HEADS * GLA_DV
O_GL, O_Q, O_K, O_V, O_GO = D_LRU, 2 * D_LRU, 2 * D_LRU + HK, 2 * D_LRU + 2 * HK, 2 * D_LRU + 2 * HK + HV
D_MAIN = O_GO + HV
D_U = D_MAIN + HK
SUBLANES = 8
LANES = 128
VMEM_LIMIT = 58 * 1024 * 1024


def _dot(a, b):
    return jnp.dot(a, b, preferred_element_type=F32)


def _dot_nt(a, b):
    return lax.dot_general(a, b, (((1,), (1,)), ((), ())), preferred_element_type=F32)


def _dot_tn(a, b):
    return lax.dot_general(a, b, (((0,), (0,)), ((), ())), preferred_element_type=F32)


def _rms(x, g):
    ms = jnp.mean(x * x, axis=-1, keepdims=True)
    return (x * lax.rsqrt(ms + EPS)) * g


def _layer_spec(a, l, single=True):
    nd = a.ndim
    mode = dict(pipeline_mode=pl.Buffered(1)) if single else {}
    return pl.BlockSpec((None,) + a.shape[1:], lambda *_: (l,) + (0,) * (nd - 1), **mode)


def _whole_spec(a):
    nd = a.ndim
    return pl.BlockSpec(a.shape, lambda *_: (0,) * nd, pipeline_mode=pl.Buffered(1))


def _ffn_kernel(*refs, tf, has_final):
    if has_final:
        x_ref, g_ref, wgu_ref, wd_ref, gfin_ref, xo_ref = refs
    else:
        x_ref, g_ref, wgu_ref, wd_ref, xo_ref = refs
    x = x_ref[...]
    xn = _rms(x, g_ref[...])
    acc = None
    for c in range(D_FF // tf):
        gate = _dot(xn, wgu_ref[:, c * tf:(c + 1) * tf])
        up = _dot(xn, wgu_ref[:, D_FF + c * tf:D_FF + (c + 1) * tf])
        d = _dot(jax.nn.silu(gate) * up, wd_ref[c * tf:(c + 1) * tf, :])
        acc = d if acc is None else acc + d
    x = x + 0.5 * acc
    xo_ref[...] = _rms(x, gfin_ref[...]) if has_final else x


def _ffn_call(x, norm, wgu, wd, l, *, final=None, tm, tf, name):
    t, d = x.shape
    assert t % tm == 0 and D_FF % tf == 0
    row = pl.BlockSpec((tm, d), lambda i: (i, 0))
    args = [x, norm, wgu, wd]
    specs = [row, _layer_spec(norm, l), _layer_spec(wgu, l), _layer_spec(wd, l)]
    if final is not None:
        args.append(final)
        specs.append(_whole_spec(final))
    return pl.pallas_call(
        functools.partial(_ffn_kernel, tf=tf, has_final=final is not None),
        grid=(t // tm,), in_specs=specs, out_specs=row,
        out_shape=jax.ShapeDtypeStruct((t, d), F32),
        compiler_params=pltpu.CompilerParams(dimension_semantics=("parallel",),
                                             vmem_limit_bytes=VMEM_LIMIT),
        name=name,
    )(*args)


def _in_projection(x, gmix_ref, wmain_ref, wlr_ref, wg2_ref, bg_ref, u_sc):
    hn = _rms(x, gmix_ref[...])
    u_sc[:, 0:D_MAIN] = _dot(hn, wmain_ref[...])
    lr = _dot(hn, wlr_ref[...])
    zg = _dot(lr, wg2_ref[...]) + bg_ref[...]
    u_sc[:, D_MAIN:D_U] = jax.nn.log_sigmoid(zg) / GLA_GATE_NORM


def _lru_gates(xc, wg_ref, ba_ref, bx_ref, lam_ref):
    half = D_LRU // 2
    p0 = _dot(xc[:, 0:half], wg_ref[0])
    p1 = _dot(xc[:, half:D_LRU], wg_ref[1])
    r = jax.nn.sigmoid(jnp.concatenate([p0[:, 0:half], p1[:, 0:half]], axis=1) + ba_ref[...])
    i = jax.nn.sigmoid(jnp.concatenate([p0[:, half:], p1[:, half:]], axis=1) + bx_ref[...])
    log_a = -LRU_C * r * jax.nn.softplus(-lam_ref[...])
    a = jnp.exp(log_a)
    mult = jnp.sqrt(-jnp.tanh(log_a) * (a * a + 1.0))
    return a, mult, i * xc


def _scan_rows(a, b, t, n):
    s = 1
    while s < n:
        keep = t >= s
        b = jnp.where(keep, a * pltpu.roll(b, s, 0) + b, b)
        if 2 * s < n:
            a = jnp.where(keep, a * pltpu.roll(a, s, 0), a)
        s *= 2
    return b


def _cumsum_rows(x, t, n):
    s = 1
    while s < n:
        x = x + jnp.where(t >= s, pltpu.roll(x, s, 0), 0.0)
        s *= 2
    return x


def _head_stack(q_s, lane_head):
    return jnp.concatenate([jnp.where(lane_head == hd, q_s, 0.0) for hd in range(GLA_HEADS)], axis=0)


def _gla_chunk(q_s, k_s, k_end, el, v, s_all, causal, lane_head, n):
    stack = _head_stack(q_s, lane_head)
    o_inter = _dot(stack, s_all)
    att = jnp.where(causal, _dot_nt(stack, k_s), 0.0)
    o_full = _dot(att, v)
    kv = _dot_tn(k_end, v)
    el_t = jnp.broadcast_to(el, (LANES, HK)).T
    o, s_new = [], []
    for hd in range(GLA_HEADS):
        rows = slice(hd * n, (hd + 1) * n)
        ks = slice(hd * GLA_DK, (hd + 1) * GLA_DK)
        vs = slice(hd * GLA_DV, (hd + 1) * GLA_DV)
        o.append(o_inter[rows] + o_full[rows, vs])
        s_new.append(el_t[ks] * s_all[ks] + kv[ks, vs])
    return o, s_new


def _gla_output(o, go, gn):
    return _rms(o, gn) * jax.nn.silu(go)


def _mixer_seq_kernel(*refs, tc, cg, reset_first, aliased):
    (x_ref, frame0_ref, h0_ref, s0_ref, gmix_ref, wmain_ref, wlr_ref, wg2_ref, bg_ref, convw_ref,
     convb_ref, wg_ref, ba_ref, bx_ref, lam_ref, gn_ref, wout_ref) = refs[:17]
    refs = refs[17 + (1 if aliased else 0):]
    xo_ref, hl_ref, so_ref, tailo_ref, u_sc, z_sc, tail_sc, h_sc, s_sc = refs
    j = pl.program_id(1)

    @pl.when(j == 0)
    def _():
        tail_sc[...] = frame0_ref[...]
        h_sc[...] = h0_ref[...]
        s_sc[...] = s0_ref[...].reshape(HK, GLA_DV)

    x = x_ref[...]
    _in_projection(x, gmix_ref, wmain_ref, wlr_ref, wg2_ref, bg_ref, u_sc)

    xl = u_sc[:, 0:D_LRU]
    tail = tail_sc[...]
    row8 = lax.broadcasted_iota(jnp.int32, (SUBLANES, D_LRU), 0)
    xc = convb_ref[...]
    for s in range(CONV_W - 1, -1, -1):
        if s == 0:
            sh = xl
        else:
            rolled = pltpu.roll(xl, s, 0)
            top = jnp.where(row8 < s, pltpu.roll(tail, s, 0), rolled[0:SUBLANES])
            sh = top if tc == SUBLANES else jnp.concatenate([top, rolled[SUBLANES:]], axis=0)
        xc = xc + convw_ref[CONV_W - 1 - s:CONV_W - s, :] * sh
    tail_sc[...] = xl[tc - SUBLANES:tc]
    tailo_ref[...] = xl[tc - SUBLANES:tc]

    a, mult, ix = _lru_gates(xc, wg_ref, ba_ref, bx_ref, lam_ref)
    row = lax.broadcasted_iota(jnp.int32, (tc, D_LRU), 0)
    if reset_first:
        first = jnp.logical_and(row == 0, j == 0)
        mult = jnp.where(first, 1.0, mult)
        a = jnp.where(first, 0.0, a)
    bt = mult * ix + jnp.where(row == 0, a * h_sc[...], 0.0)
    h = _scan_rows(a, bt, row, tc)
    h_sc[...] = h[tc - 1:tc]
    hl_ref[...] = h[tc - 1:tc]
    z_sc[:, 0:D_LRU] = h * jax.nn.gelu(u_sc[:, O_GL:O_GL + D_LRU])

    tq = lax.broadcasted_iota(jnp.int32, (cg, HK), 0)
    lane_head = lax.broadcasted_iota(jnp.int32, (cg, HK), 1) // GLA_DK
    causal = ((lax.broadcasted_iota(jnp.int32, (GLA_HEADS * cg, cg), 0) % cg)
              >= lax.broadcasted_iota(jnp.int32, (GLA_HEADS * cg, cg), 1))
    gn = gn_ref[...]
    for c in range(tc // cg):
        rs = slice(c * cg, (c + 1) * cg)
        b = _cumsum_rows(u_sc[rs, D_MAIN:D_U], tq, cg)
        bl = b[cg - 1:cg]
        q = u_sc[rs, O_Q:O_Q + HK] * (GLA_DK ** -0.5)
        k = u_sc[rs, O_K:O_K + HK]
        o, s_new = _gla_chunk(q * jnp.exp(b), k * jnp.exp(-b), k * jnp.exp(bl - b), jnp.exp(bl),
                              u_sc[rs, O_V:O_V + HV], s_sc[...], causal, lane_head, cg)
        for hd in range(GLA_HEADS):
            s_sc[hd * GLA_DK:(hd + 1) * GLA_DK, :] = s_new[hd]
            vs = slice(hd * GLA_DV, (hd + 1) * GLA_DV)
            z_sc[rs, D_LRU + hd * GLA_DV:D_LRU + (hd + 1) * GLA_DV] = _gla_output(
                o[hd], u_sc[rs, O_GO + hd * GLA_DV:O_GO + (hd + 1) * GLA_DV], gn)
    so_ref[...] = s_sc[...].reshape(GLA_HEADS, GLA_DK, GLA_DV)

    xo_ref[...] = x + _dot(z_sc[...], wout_ref[...])


def _mixer_seq_call(x, frame0, h0, s0, mix, l, *, nb, sl, tc, cg, row_block0, reset_first,
                    shared_state, alias_into, name):
    assert sl % tc == 0 and tc % cg == 0 and tc % SUBLANES == 0
    nj = sl // tc
    xrow = pl.BlockSpec((tc, D_MODEL), lambda b, j: (row_block0 + b * nj + j, 0))
    if shared_state:
        st = lambda b, j: (0, 0, 0)
        st4 = lambda b, j: (0, 0, 0, 0)
    else:
        st = lambda b, j: (b, 0, 0)
        st4 = lambda b, j: (b, 0, 0, 0)
    (gmix, w_in, wlr, wg2, bg, convw, convb, wg, ba, bx, lam, gn, wout) = mix
    wmain_spec = pl.BlockSpec((None, D_MODEL, D_MAIN), lambda b, j: (l, 0, 0),
                              pipeline_mode=pl.Buffered(1))
    args = [x, frame0, h0, s0, gmix, w_in, wlr, wg2, bg, convw, convb, wg, ba, bx, lam, gn, wout]
    in_specs = [
        xrow,
        pl.BlockSpec((None, SUBLANES, D_LRU), st),
        pl.BlockSpec((None, 1, D_LRU), st),
        pl.BlockSpec((None, GLA_HEADS, GLA_DK, GLA_DV), st4),
        _layer_spec(gmix, l), wmain_spec, _layer_spec(wlr, l), _layer_spec(wg2, l), _layer_spec(bg, l),
        _layer_spec(convw, l), _layer_spec(convb, l), _layer_spec(wg, l), _layer_spec(ba, l),
        _layer_spec(bx, l), _layer_spec(lam, l), _layer_spec(gn, l), _layer_spec(wout, l),
    ]
    aliases = {}
    if alias_into is not None:
        args.append(alias_into)
        in_specs.append(pl.BlockSpec(memory_space=pl.ANY))
        aliases = {len(args) - 1: 0}
    out_shape = [
        jax.ShapeDtypeStruct(x.shape, F32),
        jax.ShapeDtypeStruct((nb, 1, D_LRU), F32),
        jax.ShapeDtypeStruct((nb, GLA_HEADS, GLA_DK, GLA_DV), F32),
        jax.ShapeDtypeStruct((nb, SUBLANES, D_LRU), F32),
    ]
    out_specs = [
        xrow,
        pl.BlockSpec((None, 1, D_LRU), lambda b, j: (b, 0, 0)),
        pl.BlockSpec((None, GLA_HEADS, GLA_DK, GLA_DV), lambda b, j: (b, 0, 0, 0)),
        pl.BlockSpec((None, SUBLANES, D_LRU), lambda b, j: (b, 0, 0)),
    ]
    scratch = [
        pltpu.VMEM((tc, D_U), F32),
        pltpu.VMEM((tc, D_MODEL), F32),
        pltpu.VMEM((SUBLANES, D_LRU), F32),
        pltpu.VMEM((1, D_LRU), F32),
        pltpu.VMEM((HK, GLA_DV), F32),
    ]
    kern = functools.partial(_mixer_seq_kernel, tc=tc, cg=cg, reset_first=reset_first,
                             aliased=alias_into is not None)
    return pl.pallas_call(
        kern, grid=(nb, nj), in_specs=in_specs, out_specs=out_specs, out_shape=out_shape,
        scratch_shapes=scratch, input_output_aliases=aliases,
        compiler_params=pltpu.CompilerParams(dimension_semantics=("arbitrary", "arbitrary"),
                                             vmem_limit_bytes=VMEM_LIMIT),
        name=name,
    )(*args)


def _mixer_dec_kernel(x_ref, frame_ref, h0_ref, s0_ref, gmix_ref, wmain_ref, wlr_ref, wg2_ref, bg_ref,
                      convw_ref, convb_ref, wg_ref, ba_ref, bx_ref, lam_ref, gn_ref, wout_ref,
                      xo_ref, h_ref, xl_ref, so_ref,
                      u_sc, z_sc, qs_sc, ks_sc, ke_sc, el_sc, o_sc, *, nb):
    ls = SUBLANES
    rows = nb * ls
    x = x_ref[...]
    _in_projection(x, gmix_ref, wmain_ref, wlr_ref, wg2_ref, bg_ref, u_sc)

    t = lax.broadcasted_iota(jnp.int32, (rows, D_LRU), 0) % ls
    xl = u_sc[:, 0:D_LRU]
    xl_ref[...] = xl
    frame = frame_ref[...]
    xc = convb_ref[...]
    for s in range(CONV_W - 1, -1, -1):
        if s == 0:
            sh = xl
        else:
            sh = jnp.where(t >= s, pltpu.roll(xl, s, 0), pltpu.roll(frame, (rows - ls + s) % rows, 0))
        xc = xc + convw_ref[CONV_W - 1 - s:CONV_W - s, :] * sh
    a, mult, ix = _lru_gates(xc, wg_ref, ba_ref, bx_ref, lam_ref)
    bt = mult * ix + jnp.where(t == 0, a * h0_ref[...], 0.0)
    h = _scan_rows(a, bt, t, ls)
    h_ref[...] = h
    z_sc[:, 0:D_LRU] = h * jax.nn.gelu(u_sc[:, O_GL:O_GL + D_LRU])

    tq = lax.broadcasted_iota(jnp.int32, (rows, HK), 0) % ls
    b = _cumsum_rows(u_sc[:, D_MAIN:D_U], tq, ls)
    bl = jnp.where(tq == ls - 1, b, 0.0)
    s = 1
    while s < ls:
        bl = bl + jnp.where(tq + s < ls, pltpu.roll(bl, rows - s, 0), 0.0)
        s *= 2
    k = u_sc[:, O_K:O_K + HK]
    qs_sc[...] = (u_sc[:, O_Q:O_Q + HK] * (GLA_DK ** -0.5)) * jnp.exp(b)
    ks_sc[...] = k * jnp.exp(-b)
    ke_sc[...] = k * jnp.exp(bl - b)
    el_sc[...] = jnp.exp(bl)

    lane_head = lax.broadcasted_iota(jnp.int32, (ls, HK), 1) // GLA_DK
    causal = ((lax.broadcasted_iota(jnp.int32, (GLA_HEADS * ls, ls), 0) % ls)
              >= lax.broadcasted_iota(jnp.int32, (GLA_HEADS * ls, ls), 1))

    def body(bi, carry):
        rs = pl.ds(pl.multiple_of(bi * ls, ls), ls)
        s_all = s0_ref[bi].reshape(HK, GLA_DV)
        o, s_new = _gla_chunk(qs_sc[rs, :], ks_sc[rs, :], ke_sc[rs, :], el_sc[rs, :][0:1],
                              u_sc[rs, O_V:O_V + HV], s_all, causal, lane_head, ls)
        for hd in range(GLA_HEADS):
            so_ref[bi, hd] = s_new[hd]
            o_sc[rs, hd * GLA_DV:(hd + 1) * GLA_DV] = o[hd]
        return carry

    lax.fori_loop(0, nb, body, 0)

    gn = gn_ref[...]
    for hd in range(GLA_HEADS):
        vs = slice(hd * GLA_DV, (hd + 1) * GLA_DV)
        z_sc[:, D_LRU + hd * GLA_DV:D_LRU + (hd + 1) * GLA_DV] = _gla_output(
            o_sc[:, vs], u_sc[:, O_GO + hd * GLA_DV:O_GO + (hd + 1) * GLA_DV], gn)
    xo_ref[...] = x + _dot(z_sc[...], wout_ref[...])


def _mixer_dec_call(x, frame, h0, s0, mix, l, *, nseq, nb, name):
    ls = SUBLANES
    assert nseq % nb == 0
    rows = nb * ls
    (gmix, w_in, wlr, wg2, bg, convw, convb, wg, ba, bx, lam, gn, wout) = mix
    rowspec = lambda w: pl.BlockSpec((rows, w), lambda i: (i, 0))
    sspec = pl.BlockSpec((None, nb, GLA_HEADS, GLA_DK, GLA_DV), lambda i: (l, i, 0, 0, 0))
    wmain_spec = pl.BlockSpec((None, D_MODEL, D_MAIN), lambda i: (l, 0, 0), pipeline_mode=pl.Buffered(1))
    in_specs = [
        rowspec(D_MODEL), rowspec(D_LRU), rowspec(D_LRU), sspec,
        _layer_spec(gmix, l), wmain_spec, _layer_spec(wlr, l), _layer_spec(wg2, l), _layer_spec(bg, l),
        _layer_spec(convw, l), _layer_spec(convb, l), _layer_spec(wg, l), _layer_spec(ba, l),
        _layer_spec(bx, l), _layer_spec(lam, l), _layer_spec(gn, l), _layer_spec(wout, l),
    ]
    out_shape = [
        jax.ShapeDtypeStruct(x.shape, F32),
        jax.ShapeDtypeStruct((nseq * ls, D_LRU), F32),
        jax.ShapeDtypeStruct((nseq * ls, D_LRU), F32),
        jax.ShapeDtypeStruct((nseq, GLA_HEADS, GLA_DK, GLA_DV), F32),
    ]
    out_specs = [
        rowspec(D_MODEL), rowspec(D_LRU), rowspec(D_LRU),
        pl.BlockSpec((nb, GLA_HEADS, GLA_DK, GLA_DV), lambda i: (i, 0, 0, 0)),
    ]
    scratch = [
        pltpu.VMEM((rows, D_U), F32), pltpu.VMEM((rows, D_MODEL), F32),
        pltpu.VMEM((rows, HK), F32), pltpu.VMEM((rows, HK), F32), pltpu.VMEM((rows, HK), F32),
        pltpu.VMEM((rows, HK), F32), pltpu.VMEM((rows, HV), F32),
    ]
    return pl.pallas_call(
        functools.partial(_mixer_dec_kernel, nb=nb),
        grid=(nseq // nb,), in_specs=in_specs, out_specs=out_specs, out_shape=out_shape,
        scratch_shapes=scratch,
        compiler_params=pltpu.CompilerParams(dimension_semantics=("arbitrary",),
                                             vmem_limit_bytes=VMEM_LIMIT),
        name=name,
    )(x, frame, h0, s0, gmix, w_in, wlr, wg2, bg, convw, convb, wg, ba, bx, lam, gn, wout)


def _block_diag_gates(wa, wx):
    per = LRU_BLOCKS // 2
    bw = wa.shape[-1]
    eye = jnp.eye(per, dtype=wa.dtype)

    def bd(w):
        return jnp.einsum("lncd,nm->lncmd", w, eye).reshape(w.shape[0], per * bw, per * bw)

    halves = [jnp.concatenate([bd(wa[:, hf * per:(hf + 1) * per]), bd(wx[:, hf * per:(hf + 1) * per])],
                              axis=-1) for hf in range(2)]
    return jnp.stack(halves, axis=1)


def kernel(x_prompt, x_sample, state_lru_h, state_lru_conv, state_gla_S, meta, norm_ffn1, w_ffn1_gu,
           w_ffn1_down, norm_mix, w_in, lru_conv_w, lru_conv_b, lru_wa, lru_ba, lru_wx, lru_bx,
           lru_lambda, gla_w_gate2, gla_b_gate, gla_norm, w_out, norm_ffn2, w_ffn2_gu, w_ffn2_down,
           norm_final):
    bp, lp, d = x_prompt.shape
    bs, ls, _ = x_sample.shape
    n_meta = meta.shape[0]
    depth = w_in.shape[0]
    assert d == D_MODEL and ls == SUBLANES and n_meta % SUBLANES == 0
    tp, ts = bp * lp, bs * ls
    tsm = ts + n_meta
    assert ts % n_meta == 0
    tm_p = 512 if tp % 512 == 0 else lp
    tm_s = tsm // 2 if (tsm // 2) % SUBLANES == 0 else tsm
    tc_p = 256 if lp % 256 == 0 else GLA_CHUNK
    nb_s = 32 if bs % 32 == 0 else bs
    tail = CONV_W - 1

    vec = lambda a: a.reshape(a.shape[0], 1, a.shape[-1])
    mix = (vec(norm_mix), w_in,
           jnp.pad(w_in[:, :, D_MAIN:], ((0, 0), (0, 0), (0, LANES - GLA_RANK))),
           jnp.pad(gla_w_gate2, ((0, 0), (0, LANES - GLA_RANK), (0, 0))),
           vec(gla_b_gate), lru_conv_w, vec(lru_conv_b), _block_diag_gates(lru_wa, lru_wx),
           vec(lru_ba), vec(lru_bx), vec(lru_lambda), vec(gla_norm), w_out)
    n1, n2, nfin = vec(norm_ffn1), vec(norm_ffn2), norm_final.reshape(1, d)
    frames_s = jnp.pad(state_lru_conv, ((0, 0), (0, 0), (SUBLANES - tail, 0), (0, 0)))
    frames_s = frames_s.reshape(depth, ts, D_LRU)
    h0_s = jnp.broadcast_to(state_lru_h[:, :, None, :], (depth, bs, ls, D_LRU)).reshape(depth, ts, D_LRU)
    zero_frame = jnp.zeros((1, SUBLANES, D_LRU), F32)
    zero_h = jnp.zeros((1, 1, D_LRU), F32)
    zero_s = jnp.zeros((1, GLA_HEADS, GLA_DK, GLA_DV), F32)

    xp = x_prompt.reshape(tp, d)
    xs = jnp.concatenate([x_sample.reshape(ts, d), meta.astype(F32)], axis=0)

    hs_p, convs_p, ss_p, hs_s, convs_s, ss_s = [], [], [], [], [], []
    for l in range(depth):
        xp1 = _ffn_call(xp, n1, w_ffn1_gu, w_ffn1_down, l, tm=tm_p, tf=256, name=f"ffn1_p{l}")
        xs1 = _ffn_call(xs, n1, w_ffn1_gu, w_ffn1_down, l, tm=tm_s, tf=256, name=f"ffn1_s{l}")

        xs2, h_s, xl_s, s_s = _mixer_dec_call(xs1, frames_s[l], h0_s[l], state_gla_S, mix, l,
                                              nseq=bs, nb=nb_s, name=f"mix_s{l}")
        hs_s.append(h_s.reshape(bs, ls, D_LRU)[:, ls - 1])
        convs_s.append(xl_s.reshape(bs, ls, D_LRU)[:, ls - tail:])
        ss_s.append(s_s)

        xs2, h_m, s_m, tail_m = _mixer_seq_call(
            xs1, zero_frame, zero_h, zero_s, mix, l, nb=1, sl=n_meta, tc=n_meta, cg=n_meta,
            row_block0=ts // n_meta, reset_first=True, shared_state=False, alias_into=xs2,
            name=f"mix_m{l}")

        xp2, h_p, s_p, tail_p = _mixer_seq_call(
            xp1, tail_m, h_m, s_m, mix, l, nb=bp, sl=lp, tc=tc_p, cg=GLA_CHUNK, row_block0=0,
            reset_first=False, shared_state=True, alias_into=None, name=f"mix_p{l}")
        hs_p.append(h_p[:, 0])
        convs_p.append(tail_p[:, SUBLANES - tail:])
        ss_p.append(s_p)

        fin = nfin if l == depth - 1 else None
        xp = _ffn_call(xp2, n2, w_ffn2_gu, w_ffn2_down, l, final=fin, tm=tm_p, tf=256, name=f"ffn2_p{l}")
        xs = _ffn_call(xs2, n2, w_ffn2_gu, w_ffn2_down, l, final=fin, tm=tm_s, tf=256, name=f"ffn2_s{l}")

    return (xp.reshape(bp, lp, d), xs[:ts].reshape(bs, ls, d),
            jnp.stack(hs_p), jnp.stack(convs_p), jnp.stack(ss_p),
            jnp.stack(hs_s), jnp.stack(convs_s), jnp.stack(ss_s))
```

```python
import functools

import jax
import jax.numpy as jnp
from jax import lax
from jax.experimental import pallas as pl
from jax.experimental.pallas import tpu as pltpu

F32 = jnp.float32
BF16 = jnp.bfloat16

D_MODEL = 1024
D_FF = 2816
D_LRU = 512
LRU_BLOCKS = 8
CONV_W = 4
LRU_C = 8.0
GLA_HEADS = 4
GLA_DV = 128
GLA_DK = 64
GLA_RANK = 16
GLA_GATE_NORM = 16.0
GLA_CHUNK = 64
EPS = 1e-6
HK = GLA_HEADS * GLA_DK
HV = GLA_HEADS * GLA_DV
O_GL, O_Q, O_K, O_V, O_GO = D_LRU, 2 * D_LRU, 2 * D_LRU + HK, 2 * D_LRU + 2 * HK, 2 * D_LRU + 2 * HK + HV
D_MAIN = O_GO + HV
D_U = D_MAIN + HK
SUBLANES = 8
LANES = 128
VMEM_LIMIT = 58 * 1024 * 1024


def _dot(a, b):
    if a.dtype != b.dtype:
        a = a.astype(b.dtype)
    return jnp.dot(a, b, preferred_element_type=F32)


def _dot_nt(a, b):
    return lax.dot_general(a, b, (((1,), (1,)), ((), ())), preferred_element_type=F32)


def _dot_tn(a, b):
    return lax.dot_general(a, b, (((0,), (0,)), ((), ())), preferred_element_type=F32)


def _rms(x, g):
    ms = jnp.mean(x * x, axis=-1, keepdims=True)
    return (x * lax.rsqrt(ms + EPS)) * g


def _layer_spec(a, l, single=True):
    nd = a.ndim
    mode = dict(pipeline_mode=pl.Buffered(1)) if single else {}
    return pl.BlockSpec((None,) + a.shape[1:], lambda *_: (l,) + (0,) * (nd - 1), **mode)


def _whole_spec(a):
    nd = a.ndim
    return pl.BlockSpec(a.shape, lambda *_: (0,) * nd, pipeline_mode=pl.Buffered(1))


def _ffn_kernel(*refs, tf, has_final):
    if has_final:
        x_ref, g_ref, wgu_ref, wd_ref, gfin_ref, xo_ref = refs
    else:
        x_ref, g_ref, wgu_ref, wd_ref, xo_ref = refs
    x = x_ref[...]
    xn = _rms(x, g_ref[...])
    acc = None
    for c in range(D_FF // tf):
        gate = _dot(xn, wgu_ref[:, c * tf:(c + 1) * tf])
        up = _dot(xn, wgu_ref[:, D_FF + c * tf:D_FF + (c + 1) * tf])
        d = _dot(jax.nn.silu(gate) * up, wd_ref[c * tf:(c + 1) * tf, :])
        acc = d if acc is None else acc + d
    x = x + 0.5 * acc
    xo_ref[...] = _rms(x, gfin_ref[...]) if has_final else x


def _ffn_call(x, norm, wgu, wd, l, *, final=None, tm, tf, name):
    t, d = x.shape
    assert t % tm == 0 and D_FF % tf == 0
    row = pl.BlockSpec((tm, d), lambda i: (i, 0))
    args = [x, norm, wgu, wd]
    specs = [row, _layer_spec(norm, l), _layer_spec(wgu, l), _layer_spec(wd, l)]
    if final is not None:
        args.append(final)
        specs.append(_whole_spec(final))
    return pl.pallas_call(
        functools.partial(_ffn_kernel, tf=tf, has_final=final is not None),
        grid=(t // tm,), in_specs=specs, out_specs=row,
        out_shape=jax.ShapeDtypeStruct((t, d), F32),
        compiler_params=pltpu.CompilerParams(dimension_semantics=("parallel",),
                                             vmem_limit_bytes=VMEM_LIMIT),
        name=name,
    )(*args)


def _in_projection(x, gmix_ref, wmain_ref, wlr_ref, wg2_ref, bg_ref, u_sc):
    hn = _rms(x, gmix_ref[...]).astype(wmain_ref.dtype)
    u_sc[:, 0:D_MAIN] = _dot(hn, wmain_ref[...])
    lr = _dot(hn, wlr_ref[...])
    zg = _dot(lr, wg2_ref[...]) + bg_ref[...]
    u_sc[:, D_MAIN:D_U] = jax.nn.log_sigmoid(zg) / GLA_GATE_NORM


def _lru_gates(xc, wg, ba, bx, lam):
    half = xc.shape[1]
    p = _dot(xc, wg)
    r = jax.nn.sigmoid(p[:, 0:half] + ba)
    i = jax.nn.sigmoid(p[:, half:] + bx)
    log_a = -LRU_C * r * jax.nn.softplus(-lam)
    a = jnp.exp(log_a)
    y = -jnp.tanh(log_a) * (a * a + 1.0)
    mult = jnp.where(y > 0.0, y * lax.rsqrt(y), 0.0)
    return a, mult, i * xc


def _scan_groups(a, b):
    rows, cols = a.shape
    a3 = a.reshape(rows // SUBLANES, SUBLANES, cols)
    b3 = b.reshape(rows // SUBLANES, SUBLANES, cols)
    t = lax.broadcasted_iota(jnp.int32, a3.shape, 1)
    s = 1
    while s < SUBLANES:
        keep = t >= s
        b3 = jnp.where(keep, a3 * pltpu.roll(b3, s, 1) + b3, b3)
        a3 = jnp.where(keep, a3 * pltpu.roll(a3, s, 1), a3)
        s *= 2
    return a3, b3


def _scan_rows(a, b, h0):
    a3, b3 = _scan_groups(a, b)
    carry, out = h0, []
    for g in range(a3.shape[0]):
        hg = a3[g] * carry + b3[g]
        out.append(hg)
        carry = hg[SUBLANES - 1:SUBLANES]
    return out[0] if len(out) == 1 else jnp.concatenate(out, axis=0)


def _cumsum_rows(x, t, n):
    s = 1
    while s < n:
        x = x + jnp.where(t >= s, pltpu.roll(x, s, 0), 0.0)
        s *= 2
    return x


def _head_stack(q_s, lane_head):
    return jnp.concatenate([jnp.where(lane_head == hd, q_s, 0.0) for hd in range(GLA_HEADS)], axis=0)


def _gla_chunk(q_s, k_s, k_end, el, v, s_all, causal, lane_head, n):
    stack = _head_stack(q_s, lane_head)
    o_inter = _dot(stack, s_all)
    att = jnp.where(causal, _dot_nt(stack, k_s), 0.0)
    kv = _dot_tn(k_end, v)
    el_t = jnp.broadcast_to(el, (LANES, HK)).T
    o, s_new = [], []
    for hd in range(GLA_HEADS):
        rows = slice(hd * n, (hd + 1) * n)
        ks = slice(hd * GLA_DK, (hd + 1) * GLA_DK)
        vs = slice(hd * GLA_DV, (hd + 1) * GLA_DV)
        o.append(o_inter[rows] + _dot(att[rows], v[:, vs]))
        s_new.append(el_t[ks] * s_all[ks] + kv[ks, vs])
    return o, s_new


def _gla_output(o, go, gn):
    return _rms(o, gn) * jax.nn.silu(go)


def _mixer_seq_kernel(*refs, tc, cg, nj, n_chunks, reset_first, aliased):
    (xin_ref, xres_ref, frame0_ref, h0_ref, s0_ref, gmix_ref, wmain_ref, wlr_ref, wg2_ref, bg_ref,
     convw_ref, convb_ref, wg_ref, ba_ref, bx_ref, lam_ref, gn_ref, wout_ref) = refs[:18]
    refs = refs[18 + (1 if aliased else 0):]
    xo_ref, hl_ref, so_ref, tailo_ref, u0_sc, u1_sc, z0_sc, z1_sc, tail_sc, h_sc, s_sc = refs
    s = pl.program_id(0)
    c = s - 1

    @pl.when(s == 0)
    def _():
        for ref in (u0_sc, u1_sc, z0_sc, z1_sc):
            ref[...] = jnp.zeros_like(ref)
        tail_sc[...] = jnp.zeros_like(tail_sc)
        h_sc[...] = jnp.zeros_like(h_sc)
        s_sc[...] = jnp.zeros_like(s_sc)

    @pl.when(c % nj == 0)
    def _():
        tail_sc[...] = frame0_ref[...]
        h_sc[...] = h0_ref[...]
        s_sc[...] = s0_ref[...].reshape(HK, GLA_DV)

    def stages(u_in, u, z, z_out):
        hn = _rms(xin_ref[...], gmix_ref[...]).astype(wmain_ref.dtype)
        z_prev = z_out[...].astype(wout_ref.dtype)
        proj_w = 2 * LANES

        def in_piece(cb):
            cols = slice(cb * proj_w, (cb + 1) * proj_w)
            u_in[:, cols] = _dot(hn, wmain_ref[:, cols])

        def gate_piece():
            lr = _dot(hn, wlr_ref[...])
            zg = _dot(lr, wg2_ref[...]) + bg_ref[...]
            u_in[:, D_MAIN:D_U] = jax.nn.log_sigmoid(zg) / GLA_GATE_NORM

        def out_piece(cb):
            cols = slice(cb * proj_w, (cb + 1) * proj_w)
            xo_ref[:, cols] = xres_ref[:, cols] + _dot(z_prev, wout_ref[:, cols])

        carry = {}
        half = D_LRU // 2

        def lru_front(hf):
            cs = slice(hf * half, (hf + 1) * half)
            xl = u[:, cs]
            row8 = lax.broadcasted_iota(jnp.int32, (SUBLANES, half), 0)
            xc = convb_ref[:, cs]
            for sft in range(CONV_W - 1, -1, -1):
                if sft == 0:
                    sh = xl
                else:
                    rolled = pltpu.roll(xl, sft, 0)
                    top = jnp.where(row8 < sft, pltpu.roll(tail_sc[:, cs], sft, 0), rolled[0:SUBLANES])
                    sh = top if tc == SUBLANES else jnp.concatenate([top, rolled[SUBLANES:]], axis=0)
                xc = xc + convw_ref[CONV_W - 1 - sft:CONV_W - sft, cs] * sh
            carry["tail", hf] = xl[tc - SUBLANES:tc]
            tail_sc[:, cs] = carry["tail", hf]
            carry["xc", hf] = xc

        def lru_mid(hf):
            cs = slice(hf * half, (hf + 1) * half)
            a, mult, ix = _lru_gates(carry["xc", hf], wg_ref[hf], ba_ref[:, cs], bx_ref[:, cs],
                                     lam_ref[:, cs])
            if reset_first:
                row = lax.broadcasted_iota(jnp.int32, (tc, half), 0)
                first = jnp.logical_and(row == 0, c % nj == 0)
                mult = jnp.where(first, 1.0, mult)
                a = jnp.where(first, 0.0, a)
            carry["a", hf], carry["b", hf] = a, mult * ix

        def lru_back(hf):
            cs = slice(hf * half, (hf + 1) * half)
            h = _scan_rows(carry["a", hf], carry["b", hf], h_sc[:, cs])
            carry["h", hf] = h[tc - 1:tc]
            h_sc[:, cs] = carry["h", hf]
            z[:, cs] = h * jax.nn.gelu(u[:, O_GL + hf * half:O_GL + (hf + 1) * half])

        def gla_piece(ci):
            tq = lax.broadcasted_iota(jnp.int32, (cg, HK), 0)
            lane_head = lax.broadcasted_iota(jnp.int32, (cg, HK), 1) // GLA_DK
            causal = ((lax.broadcasted_iota(jnp.int32, (GLA_HEADS * cg, cg), 0) % cg)
                      >= lax.broadcasted_iota(jnp.int32, (GLA_HEADS * cg, cg), 1))
            rs = slice(ci * cg, (ci + 1) * cg)
            b = _cumsum_rows(u[rs, D_MAIN:D_U], tq, cg)
            bl = b[cg - 1:cg]
            q = u[rs, O_Q:O_Q + HK] * (GLA_DK ** -0.5)
            k = u[rs, O_K:O_K + HK]
            o, s_new = _gla_chunk(q * jnp.exp(b), k * jnp.exp(-b), k * jnp.exp(bl - b), jnp.exp(bl),
                                  u[rs, O_V:O_V + HV], s_sc[...], causal, lane_head, cg)
            for hd in range(GLA_HEADS):
                s_sc[hd * GLA_DK:(hd + 1) * GLA_DK, :] = s_new[hd]
                z[rs, D_LRU + hd * GLA_DV:D_LRU + (hd + 1) * GLA_DV] = _gla_output(
                    o[hd], u[rs, O_GO + hd * GLA_DV:O_GO + (hd + 1) * GLA_DV], gn_ref[...])

        mxu = ([functools.partial(in_piece, cb) for cb in range(D_MAIN // proj_w)] + [gate_piece]
               + [functools.partial(out_piece, cb) for cb in range(D_MODEL // proj_w)])
        vpu = [functools.partial(f, hf) for hf in range(2) for f in (lru_front, lru_mid, lru_back)]
        vpu +=[functools.partial(gla_piece, ci) for ci in range(tc // cg)]
        order, vi = [], 0
        for mi, piece in enumerate(mxu):
            order.append(piece)
            while vi < len(vpu) and (vi + 1) * len(mxu) <= (mi + 1) * len(vpu):
                order.append(vpu[vi])
                vi += 1
        order += vpu[vi:]
        for piece in order:
            piece()

        @pl.when(jnp.logical_and(c >= 0, c < n_chunks))
        def _():
            for hf in range(2):
                cs = slice(hf * half, (hf + 1) * half)
                hl_ref[:, cs] = carry["h", hf]
                tailo_ref[:, cs] = carry["tail", hf]
            so_ref[...] = s_sc[...].reshape(GLA_HEADS, GLA_DK, GLA_DV)

    @pl.when(s % 2 == 0)
    def _():
        stages(u0_sc, u1_sc, z1_sc, z0_sc)

    @pl.when(s % 2 == 1)
    def _():
        stages(u1_sc, u0_sc, z0_sc, z1_sc)


def _mixer_seq_call(x, frame0, h0, s0, mix, l, *, nb, sl, tc, cg, row_block0, reset_first,
                    shared_state, alias_into, name):
    assert sl % tc == 0 and tc % cg == 0 and tc % SUBLANES == 0
    nj = sl // tc
    n_chunks = nb * nj
    xin = pl.BlockSpec((tc, D_MODEL), lambda s: (row_block0 + jnp.minimum(s, n_chunks - 1), 0))
    xres = pl.BlockSpec((tc, D_MODEL), lambda s: (row_block0 + jnp.clip(s - 2, 0, n_chunks - 1), 0))
    seq = lambda s: jnp.clip((s - 1) // nj, 0, nb - 1)
    if shared_state:
        st = lambda s: (0, 0, 0)
        st4 = lambda s: (0, 0, 0, 0)
    else:
        st = lambda s: (seq(s), 0, 0)
        st4 = lambda s: (seq(s), 0, 0, 0)
    (gmix, w_in, wlr, wg2, bg, convw, convb, wg, ba, bx, lam, gn, wout) = mix
    wmain_spec = pl.BlockSpec((None, D_MODEL, D_MAIN), lambda s: (l, 0, 0), pipeline_mode=pl.Buffered(1))
    args = [x, x, frame0, h0, s0, gmix, w_in, wlr, wg2, bg, convw, convb, wg, ba, bx, lam, gn, wout]
    in_specs = [
        xin, xres,
        pl.BlockSpec((None, SUBLANES, D_LRU), st),
        pl.BlockSpec((None, 1, D_LRU), st),
        pl.BlockSpec((None, GLA_HEADS, GLA_DK, GLA_DV), st4),
        _layer_spec(gmix, l), wmain_spec, _layer_spec(wlr, l), _layer_spec(wg2, l), _layer_spec(bg, l),
        _layer_spec(convw, l), _layer_spec(convb, l), _layer_spec(wg, l), _layer_spec(ba, l),
        _layer_spec(bx, l), _layer_spec(lam, l), _layer_spec(gn, l), _layer_spec(wout, l),
    ]
    aliases = {}
    if alias_into is not None:
        args.append(alias_into)
        in_specs.append(pl.BlockSpec(memory_space=pl.ANY))
        aliases = {len(args) - 1: 0}
    out_shape = [
        jax.ShapeDtypeStruct(x.shape, F32),
        jax.ShapeDtypeStruct((nb, 1, D_LRU), F32),
        jax.ShapeDtypeStruct((nb, GLA_HEADS, GLA_DK, GLA_DV), F32),
        jax.ShapeDtypeStruct((nb, SUBLANES, D_LRU), F32),
    ]
    out_specs = [
        xres,
        pl.BlockSpec((None, 1, D_LRU), lambda s: (seq(s), 0, 0)),
        pl.BlockSpec((None, GLA_HEADS, GLA_DK, GLA_DV), lambda s: (seq(s), 0, 0, 0)),
        pl.BlockSpec((None, SUBLANES, D_LRU), lambda s: (seq(s), 0, 0)),
    ]
    scratch = [
        pltpu.VMEM((tc, D_U), F32), pltpu.VMEM((tc, D_U), F32),
        pltpu.VMEM((tc, D_MODEL), F32), pltpu.VMEM((tc, D_MODEL), F32),
        pltpu.VMEM((SUBLANES, D_LRU), F32),
        pltpu.VMEM((1, D_LRU), F32),
        pltpu.VMEM((HK, GLA_DV), F32),
    ]
    kern = functools.partial(_mixer_seq_kernel, tc=tc, cg=cg, nj=nj, n_chunks=n_chunks,
                             reset_first=reset_first, aliased=alias_into is not None)
    return pl.pallas_call(
        kern, grid=(n_chunks + 2,), in_specs=in_specs, out_specs=out_specs, out_shape=out_shape,
        scratch_shapes=scratch, input_output_aliases=aliases,
        compiler_params=pltpu.CompilerParams(dimension_semantics=("arbitrary",),
                                             vmem_limit_bytes=VMEM_LIMIT),
        name=name,
    )(*args)


def _mixer_dec_kernel(x_ref, frame_ref, h0_ref, s0_ref, gmix_ref, wmain_ref, wlr_ref, wg2_ref, bg_ref,
                      convw_ref, convb_ref, wg_ref, ba_ref, bx_ref, lam_ref, gn_ref, wout_ref,
                      xo_ref, h_ref, xl_ref, so_ref,
                      u_sc, z_sc, qs_sc, ks_sc, ke_sc, el_sc, o_sc, *, nb):
    ls = SUBLANES
    rows = nb * ls
    x = x_ref[...]
    _in_projection(x, gmix_ref, wmain_ref, wlr_ref, wg2_ref, bg_ref, u_sc)

    t = lax.broadcasted_iota(jnp.int32, (rows, D_LRU), 0) % ls
    xl = u_sc[:, 0:D_LRU]
    xl_ref[...] = xl
    frame = frame_ref[...]
    xc = convb_ref[...]
    for s in range(CONV_W - 1, -1, -1):
        if s == 0:
            sh = xl
        else:
            sh = jnp.where(t >= s, pltpu.roll(xl, s, 0), pltpu.roll(frame, (rows - ls + s) % rows, 0))
        xc = xc + convw_ref[CONV_W - 1 - s:CONV_W - s, :] * sh
    half = D_LRU // 2
    parts = [_lru_gates(xc[:, hf * half:(hf + 1) * half], wg_ref[hf], ba_ref[:, hf * half:(hf + 1) * half],
                        bx_ref[:, hf * half:(hf + 1) * half], lam_ref[:, hf * half:(hf + 1) * half])
             for hf in range(2)]
    a, mult, ix = (jnp.concatenate([p[n] for p in parts], axis=1) for n in range(3))
    a3, b3 = _scan_groups(a, mult * ix)
    h = (a3 * h0_ref[...].reshape(a3.shape) + b3).reshape(rows, D_LRU)
    h_ref[...] = h
    z_sc[:, 0:D_LRU] = h * jax.nn.gelu(u_sc[:, O_GL:O_GL + D_LRU])

    tq = lax.broadcasted_iota(jnp.int32, (rows, HK), 0) % ls
    b = _cumsum_rows(u_sc[:, D_MAIN:D_U], tq, ls)
    bl = jnp.where(tq == ls - 1, b, 0.0)
    s = 1
    while s < ls:
        bl = bl + jnp.where(tq + s < ls, pltpu.roll(bl, rows - s, 0), 0.0)
        s *= 2
    k = u_sc[:, O_K:O_K + HK]
    qs_sc[...] = (u_sc[:, O_Q:O_Q + HK] * (GLA_DK ** -0.5)) * jnp.exp(b)
    ks_sc[...] = k * jnp.exp(-b)
    ke_sc[...] = k * jnp.exp(bl - b)
    el_sc[...] = jnp.exp(bl)

    lane_head = lax.broadcasted_iota(jnp.int32, (ls, HK), 1) // GLA_DK
    causal = ((lax.broadcasted_iota(jnp.int32, (GLA_HEADS * ls, ls), 0) % ls)
              >= lax.broadcasted_iota(jnp.int32, (GLA_HEADS * ls, ls), 1))

    def body(bi, carry):
        rs = pl.ds(pl.multiple_of(bi * ls, ls), ls)
        s_all = s0_ref[bi].reshape(HK, GLA_DV)
        o, s_new = _gla_chunk(qs_sc[rs, :], ks_sc[rs, :], ke_sc[rs, :], el_sc[rs, :][0:1],
                              u_sc[rs, O_V:O_V + HV], s_all, causal, lane_head, ls)
        for hd in range(GLA_HEADS):
            so_ref[bi, hd] = s_new[hd]
            o_sc[rs, hd * GLA_DV:(hd + 1) * GLA_DV] = o[hd]
        return carry

    lax.fori_loop(0, nb, body, 0)

    gn = gn_ref[...]
    for hd in range(GLA_HEADS):
        vs = slice(hd * GLA_DV, (hd + 1) * GLA_DV)
        z_sc[:, D_LRU + hd * GLA_DV:D_LRU + (hd + 1) * GLA_DV] = _gla_output(
            o_sc[:, vs], u_sc[:, O_GO + hd * GLA_DV:O_GO + (hd + 1) * GLA_DV], gn)
    xo_ref[...] = x + _dot(z_sc[...], wout_ref[...])


def _mixer_dec_call(x, frame, h0, s0, mix, l, *, nseq, nb, name):
    ls = SUBLANES
    assert nseq % nb == 0
    rows = nb * ls
    (gmix, w_in, wlr, wg2, bg, convw, convb, wg, ba, bx, lam, gn, wout) = mix
    rowspec = lambda w: pl.BlockSpec((rows, w), lambda i: (i, 0))
    sspec = pl.BlockSpec((None, nb, GLA_HEADS, GLA_DK, GLA_DV), lambda i: (l, i, 0, 0, 0))
    wmain_spec = pl.BlockSpec((None, D_MODEL, D_MAIN), lambda i: (l, 0, 0), pipeline_mode=pl.Buffered(1))
    in_specs = [
        rowspec(D_MODEL), rowspec(D_LRU), rowspec(D_LRU), sspec,
        _layer_spec(gmix, l), wmain_spec, _layer_spec(wlr, l), _layer_spec(wg2, l), _layer_spec(bg, l),
        _layer_spec(convw, l), _layer_spec(convb, l), _layer_spec(wg, l), _layer_spec(ba, l),
        _layer_spec(bx, l), _layer_spec(lam, l), _layer_spec(gn, l), _layer_spec(wout, l),
    ]
    out_shape = [
        jax.ShapeDtypeStruct(x.shape, F32),
        jax.ShapeDtypeStruct((nseq * ls, D_LRU), F32),
        jax.ShapeDtypeStruct((nseq * ls, D_LRU), F32),
        jax.ShapeDtypeStruct((nseq, GLA_HEADS, GLA_DK, GLA_DV), F32),
    ]
    out_specs = [
        rowspec(D_MODEL), rowspec(D_LRU), rowspec(D_LRU),
        pl.BlockSpec((nb, GLA_HEADS, GLA_DK, GLA_DV), lambda i: (i, 0, 0, 0)),
    ]
    scratch = [
        pltpu.VMEM((rows, D_U), F32), pltpu.VMEM((rows, D_MODEL), F32),
        pltpu.VMEM((rows, HK), F32), pltpu.VMEM((rows, HK), F32), pltpu.VMEM((rows, HK), F32),
        pltpu.VMEM((rows, HK), F32), pltpu.VMEM((rows, HV), F32),
    ]
    return pl.pallas_call(
        functools.partial(_mixer_dec_kernel, nb=nb),
        grid=(nseq // nb,), in_specs=in_specs, out_specs=out_specs, out_shape=out_shape,
        scratch_shapes=scratch,
        compiler_params=pltpu.CompilerParams(dimension_semantics=("arbitrary",),
                                             vmem_limit_bytes=VMEM_LIMIT),
        name=name,
    )(x, frame, h0, s0, gmix, w_in, wlr, wg2, bg, convw, convb, wg, ba, bx, lam, gn, wout)


def _block_diag_gates(wa, wx):
    per = LRU_BLOCKS // 2
    bw = wa.shape[-1]
    eye = jnp.eye(per, dtype=wa.dtype)

    def bd(w):
        return jnp.einsum("lncd,nm->lncmd", w, eye).reshape(w.shape[0], per * bw, per * bw)

    halves = [jnp.concatenate([bd(wa[:, hf * per:(hf + 1) * per]), bd(wx[:, hf * per:(hf + 1) * per])],
                              axis=-1) for hf in range(2)]
    return jnp.stack(halves, axis=1)


def kernel(x_prompt, x_sample, state_lru_h, state_lru_conv, state_gla_S, meta, norm_ffn1, w_ffn1_gu,
           w_ffn1_down, norm_mix, w_in, lru_conv_w, lru_conv_b, lru_wa, lru_ba, lru_wx, lru_bx,
           lru_lambda, gla_w_gate2, gla_b_gate, gla_norm, w_out, norm_ffn2, w_ffn2_gu, w_ffn2_down,
           norm_final):
    bp, lp, d = x_prompt.shape
    bs, ls, _ = x_sample.shape
    n_meta = meta.shape[0]
    depth = w_in.shape[0]
    assert d == D_MODEL and ls == SUBLANES and n_meta % SUBLANES == 0
    tp, ts = bp * lp, bs * ls
    tsm = ts + n_meta
    assert ts % n_meta == 0
    tm_p = 512 if tp % 512 == 0 else lp
    tm_s = tsm // 2 if (tsm // 2) % SUBLANES == 0 else tsm
    tc_p = 256 if lp % 256 == 0 else GLA_CHUNK
    nb_s = 32 if bs % 32 == 0 else bs
    tail = CONV_W - 1

    vec = lambda a: a.reshape(a.shape[0], 1, a.shape[-1])
    mix = (vec(norm_mix), w_in.astype(BF16),
           jnp.pad(w_in[:, :, D_MAIN:], ((0, 0), (0, 0), (0, LANES - GLA_RANK))).astype(BF16),
           jnp.pad(gla_w_gate2, ((0, 0), (0, LANES - GLA_RANK), (0, 0))).astype(BF16),
           vec(gla_b_gate), lru_conv_w, vec(lru_conv_b), _block_diag_gates(lru_wa, lru_wx).astype(BF16),
           vec(lru_ba), vec(lru_bx), vec(lru_lambda), vec(gla_norm), w_out.astype(BF16))
    n1, n2, nfin = vec(norm_ffn1), vec(norm_ffn2), norm_final.reshape(1, d)
    frames_s = jnp.pad(state_lru_conv, ((0, 0), (0, 0), (SUBLANES - tail, 0), (0, 0)))
    frames_s = frames_s.reshape(depth, ts, D_LRU)
    h0_s = jnp.broadcast_to(state_lru_h[:, :, None, :], (depth, bs, ls, D_LRU)).reshape(depth, ts, D_LRU)
    zero_frame = jnp.zeros((1, SUBLANES, D_LRU), F32)
    zero_h = jnp.zeros((1, 1, D_LRU), F32)
    zero_s = jnp.zeros((1, GLA_HEADS, GLA_DK, GLA_DV), F32)

    xp = x_prompt.reshape(tp, d)
    xs = jnp.concatenate([x_sample.reshape(ts, d), meta.astype(F32)], axis=0)

    hs_p, convs_p, ss_p, hs_s, convs_s, ss_s = [], [], [], [], [], []
    for l in range(depth):
        xp1 = _ffn_call(xp, n1, w_ffn1_gu, w_ffn1_down, l, tm=tm_p, tf=256, name=f"ffn1_p{l}")
        xs1 = _ffn_call(xs, n1, w_ffn1_gu, w_ffn1_down, l, tm=tm_s, tf=256, name=f"ffn1_s{l}")

        xs2, h_s, xl_s, s_s = _mixer_dec_call(xs1, frames_s[l], h0_s[l], state_gla_S, mix, l,
                                              nseq=bs, nb=nb_s, name=f"mix_s{l}")
        hs_s.append(h_s.reshape(bs, ls, D_LRU)[:, ls - 1])
        convs_s.append(xl_s.reshape(bs, ls, D_LRU)[:, ls - tail:])
        ss_s.append(s_s)

        xs2, h_m, s_m, tail_m = _mixer_seq_call(
            xs1, zero_frame, zero_h, zero_s, mix, l, nb=1, sl=n_meta, tc=n_meta, cg=n_meta,
            row_block0=ts // n_meta, reset_first=True, shared_state=False, alias_into=xs2,
            name=f"mix_m{l}")

        xp2, h_p, s_p, tail_p = _mixer_seq_call(
            xp1, tail_m, h_m, s_m, mix, l, nb=bp, sl=lp, tc=tc_p, cg=GLA_CHUNK, row_block0=0,
            reset_first=False, shared_state=True, alias_into=None, name=f"mix_p{l}")
        hs_p.append(h_p[:, 0])
        convs_p.append(tail_p[:, SUBLANES - tail:])
        ss_p.append(s_p)

        fin = nfin if l == depth - 1 else None
        xp = _ffn_call(xp2, n2, w_ffn2_gu, w_ffn2_down, l, final=fin, tm=tm_p, tf=256, name=f"ffn2_p{l}")
        xs = _ffn_call(xs2, n2, w_ffn2_gu, w_ffn2_down, l, final=fin, tm=tm_s, tf=256, name=f"ffn2_s{l}")

    return (xp.reshape(bp, lp, d), xs[:ts].reshape(bs, ls, d),
            jnp.stack(hs_p), jnp.stack(convs_p), jnp.stack(ss_p),
            jnp.stack(hs_s), jnp.stack(convs_s), jnp.stack(ss_s))
```

```python
import functools

import jax
import jax.numpy as jnp
from jax import lax
from jax.experimental import pallas as pl
from jax.experimental.pallas import tpu as pltpu

F32 = jnp.float32
BF16 = jnp.bfloat16

D_MODEL = 1024
D_FF = 2816
D_LRU = 512
LRU_BLOCKS = 8
CONV_W = 4
LRU_C = 8.0
GLA_HEADS = 4
GLA_DV = 128
GLA_DK = 64
GLA_RANK = 16
GLA_GATE_NORM = 16.0
GLA_CHUNK = 64
EPS = 1e-6
HK = GLA_HEADS * GLA_DK
HV = GLA_HEADS * GLA_DV
O_GL, O_Q, O_K, O_V, O_GO = D_LRU, 2 * D_LRU, 2 * D_LRU + HK, 2 * D_LRU + 2 * HK, 2 * D_LRU + 2 * HK + HV
D_MAIN = O_GO + HV
D_U = D_MAIN + HK
SUBLANES = 8
LANES = 128
VMEM_LIMIT = 58 * 1024 * 1024


def _dot(a, b):
    if a.dtype != b.dtype:
        a = a.astype(b.dtype)
    return jnp.dot(a, b, preferred_element_type=F32)


def _dot_nt(a, b):
    return lax.dot_general(a, b, (((1,), (1,)), ((), ())), preferred_element_type=F32)


def _dot_tn(a, b):
    return lax.dot_general(a, b, (((0,), (0,)), ((), ())), preferred_element_type=F32)


def _rms(x, g):
    ms = jnp.mean(x * x, axis=-1, keepdims=True)
    return (x * lax.rsqrt(ms + EPS)) * g


def _layer_spec(a, l, single=True):
    nd = a.ndim
    mode = dict(pipeline_mode=pl.Buffered(1)) if single else {}
    return pl.BlockSpec((None,) + a.shape[1:], lambda *_: (l,) + (0,) * (nd - 1), **mode)


def _whole_spec(a):
    nd = a.ndim
    return pl.BlockSpec(a.shape, lambda *_: (0,) * nd, pipeline_mode=pl.Buffered(1))


def _ffn_kernel(*refs, tf, has_final, n_a):
    if has_final:
        xa_ref, xb_ref, g_ref, wgu_ref, wd_ref, gfin_ref, oa_ref, ob_ref = refs
    else:
        xa_ref, xb_ref, g_ref, wgu_ref, wd_ref, oa_ref, ob_ref = refs

    def tile(x_ref, xo_ref):
        x = x_ref[...]
        xn = _rms(x, g_ref[...])
        acc = None
        for c in range(D_FF // tf):
            gate = _dot(xn, wgu_ref[:, c * tf:(c + 1) * tf])
            up = _dot(xn, wgu_ref[:, D_FF + c * tf:D_FF + (c + 1) * tf])
            d = _dot(jax.nn.silu(gate) * up, wd_ref[c * tf:(c + 1) * tf, :])
            acc = d if acc is None else acc + d
        x = x + 0.5 * acc
        xo_ref[...] = _rms(x, gfin_ref[...]) if has_final else x

    i = pl.program_id(0)

    @pl.when(i < n_a)
    def _():
        tile(xa_ref, oa_ref)

    @pl.when(i >= n_a)
    def _():
        tile(xb_ref, ob_ref)


def _ffn_call(xa, xb, norm, wgu, wd, l, *, final=None, tm_a, tm_b, tf, name):
    (ta, d), (tb, _) = xa.shape, xb.shape
    assert ta % tm_a == 0 and tb % tm_b == 0 and D_FF % tf == 0
    n_a, n_b = ta // tm_a, tb // tm_b
    rows_a = pl.BlockSpec((tm_a, d), lambda i: (jnp.minimum(i, n_a - 1), 0))
    rows_b = pl.BlockSpec((tm_b, d), lambda i: (jnp.maximum(i - n_a, 0), 0))
    args = [xa, xb, norm, wgu, wd]
    specs = [rows_a, rows_b, _layer_spec(norm, l), _layer_spec(wgu, l), _layer_spec(wd, l)]
    if final is not None:
        args.append(final)
        specs.append(_whole_spec(final))
    return pl.pallas_call(
        functools.partial(_ffn_kernel, tf=tf, has_final=final is not None, n_a=n_a),
        grid=(n_a + n_b,), in_specs=specs, out_specs=[rows_a, rows_b],
        out_shape=[jax.ShapeDtypeStruct((ta, d), F32), jax.ShapeDtypeStruct((tb, d), F32)],
        compiler_params=pltpu.CompilerParams(dimension_semantics=("arbitrary",),
                                             vmem_limit_bytes=VMEM_LIMIT),
        name=name,
    )(*args)


def _in_projection(x, gmix_ref, wmain_ref, wlr_ref, wg2_ref, bg_ref, u_sc):
    hn = _rms(x, gmix_ref[...]).astype(wmain_ref.dtype)
    u_sc[:, 0:D_MAIN] = _dot(hn, wmain_ref[...])
    lr = _dot(hn, wlr_ref[...])
    zg = _dot(lr, wg2_ref[...]) + bg_ref[...]
    u_sc[:, D_MAIN:D_U] = jax.nn.log_sigmoid(zg) / GLA_GATE_NORM


def _lru_gates(xc, wg, ba, bx, lam):
    half = xc.shape[1]
    p = _dot(xc, wg)
    r = jax.nn.sigmoid(p[:, 0:half] + ba)
    i = jax.nn.sigmoid(p[:, half:] + bx)
    log_a = -LRU_C * r * jax.nn.softplus(-lam)
    a = jnp.exp(log_a)
    y = -jnp.tanh(log_a) * (a * a + 1.0)
    mult = jnp.where(y > 0.0, y * lax.rsqrt(y), 0.0)
    return a, mult, i * xc


def _scan_groups(a, b):
    rows, cols = a.shape
    a3 = a.reshape(rows // SUBLANES, SUBLANES, cols)
    b3 = b.reshape(rows // SUBLANES, SUBLANES, cols)
    t = lax.broadcasted_iota(jnp.int32, a3.shape, 1)
    s = 1
    while s < SUBLANES:
        keep = t >= s
        b3 = jnp.where(keep, a3 * pltpu.roll(b3, s, 1) + b3, b3)
        a3 = jnp.where(keep, a3 * pltpu.roll(a3, s, 1), a3)
        s *= 2
    return a3, b3


def _scan_rows(a, b, h0):
    a3, b3 = _scan_groups(a, b)
    carry, out = h0, []
    for g in range(a3.shape[0]):
        hg = a3[g] * carry + b3[g]
        out.append(hg)
        carry = hg[SUBLANES - 1:SUBLANES]
    return out[0] if len(out) == 1 else jnp.concatenate(out, axis=0)


def _cumsum_rows(x, t, n):
    s = 1
    while s < n:
        x = x + jnp.where(t >= s, pltpu.roll(x, s, 0), 0.0)
        s *= 2
    return x


def _head_stack(q_s, lane_head):
    return jnp.concatenate([jnp.where(lane_head == hd, q_s, 0.0) for hd in range(GLA_HEADS)], axis=0)


def _gla_chunk(q_s, k_s, k_end, el, v, s_all, causal, lane_head, n):
    stack = _head_stack(q_s, lane_head)
    o_inter = _dot(stack, s_all)
    att = jnp.where(causal, _dot_nt(stack, k_s), 0.0)
    kv = _dot_tn(k_end, v)
    el_t = jnp.broadcast_to(el, (LANES, HK)).T
    o, s_new = [], []
    for hd in range(GLA_HEADS):
        rows = slice(hd * n, (hd + 1) * n)
        ks = slice(hd * GLA_DK, (hd + 1) * GLA_DK)
        vs = slice(hd * GLA_DV, (hd + 1) * GLA_DV)
        o.append(o_inter[rows] + _dot(att[rows], v[:, vs]))
        s_new.append(el_t[ks] * s_all[ks] + kv[ks, vs])
    return o, s_new


def _gla_output(o, go, gn):
    return _rms(o, gn) * jax.nn.silu(go)


def _mixer_seq_kernel(*refs, tc, cg, nj, n_chunks, reset_first, aliased):
    (xin_ref, xres_ref, frame0_ref, h0_ref, s0_ref, gmix_ref, wmain_ref, wlr_ref, wg2_ref, bg_ref,
     convw_ref, convb_ref, wg_ref, ba_ref, bx_ref, lam_ref, gn_ref, wout_ref) = refs[:18]
    refs = refs[18 + (1 if aliased else 0):]
    xo_ref, hl_ref, so_ref, tailo_ref, u0_sc, u1_sc, z0_sc, z1_sc, tail_sc, h_sc, s_sc = refs
    s = pl.program_id(0)
    c = s - 1

    @pl.when(s == 0)
    def _():
        for ref in (u0_sc, u1_sc, z0_sc, z1_sc):
            ref[...] = jnp.zeros_like(ref)
        tail_sc[...] = jnp.zeros_like(tail_sc)
        h_sc[...] = jnp.zeros_like(h_sc)
        s_sc[...] = jnp.zeros_like(s_sc)

    @pl.when(c % nj == 0)
    def _():
        tail_sc[...] = frame0_ref[...]
        h_sc[...] = h0_ref[...]
        s_sc[...] = s0_ref[...].reshape(HK, GLA_DV)

    def stages(u_in, u, z, z_out):
        hn = _rms(xin_ref[...], gmix_ref[...]).astype(wmain_ref.dtype)
        z_prev = z_out[...].astype(wout_ref.dtype)
        proj_w = 2 * LANES

        def in_piece(cb):
            cols = slice(cb * proj_w, (cb + 1) * proj_w)
            u_in[:, cols] = _dot(hn, wmain_ref[:, cols])

        def gate_piece():
            lr = _dot(hn, wlr_ref[...])
            zg = _dot(lr, wg2_ref[...]) + bg_ref[...]
            u_in[:, D_MAIN:D_U] = jax.nn.log_sigmoid(zg) / GLA_GATE_NORM

        def out_piece(cb):
            cols = slice(cb * proj_w, (cb + 1) * proj_w)
            xo_ref[:, cols] = xres_ref[:, cols] + _dot(z_prev, wout_ref[:, cols])

        carry = {}
        half = D_LRU // 2

        def lru_front(hf):
            cs = slice(hf * half, (hf + 1) * half)
            xl = u[:, cs]
            row8 = lax.broadcasted_iota(jnp.int32, (SUBLANES, half), 0)
            xc = convb_ref[:, cs]
            for sft in range(CONV_W - 1, -1, -1):
                if sft == 0:
                    sh = xl
                else:
                    rolled = pltpu.roll(xl, sft, 0)
                    top = jnp.where(row8 < sft, pltpu.roll(tail_sc[:, cs], sft, 0), rolled[0:SUBLANES])
                    sh = top if tc == SUBLANES else jnp.concatenate([top, rolled[SUBLANES:]], axis=0)
                xc = xc + convw_ref[CONV_W - 1 - sft:CONV_W - sft, cs] * sh
            carry["tail", hf] = xl[tc - SUBLANES:tc]
            tail_sc[:, cs] = carry["tail", hf]
            carry["xc", hf] = xc

        def lru_mid(hf):
            cs = slice(hf * half, (hf + 1) * half)
            a, mult, ix = _lru_gates(carry["xc", hf], wg_ref[hf], ba_ref[:, cs], bx_ref[:, cs],
                                     lam_ref[:, cs])
            if reset_first:
                row = lax.broadcasted_iota(jnp.int32, (tc, half), 0)
                first = jnp.logical_and(row == 0, c % nj == 0)
                mult = jnp.where(first, 1.0, mult)
                a = jnp.where(first, 0.0, a)
            carry["a", hf], carry["b", hf] = a, mult * ix

        def lru_back(hf):
            cs = slice(hf * half, (hf + 1) * half)
            h = _scan_rows(carry["a", hf], carry["b", hf], h_sc[:, cs])
            carry["h", hf] = h[tc - 1:tc]
            h_sc[:, cs] = carry["h", hf]
            z[:, cs] = h * jax.nn.gelu(u[:, O_GL + hf * half:O_GL + (hf + 1) * half])

        def gla_piece(ci):
            tq = lax.broadcasted_iota(jnp.int32, (cg, HK), 0)
            lane_head = lax.broadcasted_iota(jnp.int32, (cg, HK), 1) // GLA_DK
            causal = ((lax.broadcasted_iota(jnp.int32, (GLA_HEADS * cg, cg), 0) % cg)
                      >= lax.broadcasted_iota(jnp.int32, (GLA_HEADS * cg, cg), 1))
            rs = slice(ci * cg, (ci + 1) * cg)
            b = _cumsum_rows(u[rs, D_MAIN:D_U], tq, cg)
            bl = b[cg - 1:cg]
            q = u[rs, O_Q:O_Q + HK] * (GLA_DK ** -0.5)
            k = u[rs, O_K:O_K + HK]
            o, s_new = _gla_chunk(q * jnp.exp(b), k * jnp.exp(-b), k * jnp.exp(bl - b), jnp.exp(bl),
                                  u[rs, O_V:O_V + HV], s_sc[...], causal, lane_head, cg)
            for hd in range(GLA_HEADS):
                s_sc[hd * GLA_DK:(hd + 1) * GLA_DK, :] = s_new[hd]
                z[rs, D_LRU + hd * GLA_DV:D_LRU + (hd + 1) * GLA_DV] = _gla_output(
                    o[hd], u[rs, O_GO + hd * GLA_DV:O_GO + (hd + 1) * GLA_DV], gn_ref[...])

        mxu = ([functools.partial(in_piece, cb) for cb in range(D_MAIN // proj_w)] + [gate_piece]
               + [functools.partial(out_piece, cb) for cb in range(D_MODEL // proj_w)])
        vpu = [functools.partial(f, hf) for hf in range(2) for f in (lru_front, lru_mid, lru_back)]
        vpu +=[functools.partial(gla_piece, ci) for ci in range(tc // cg)]
        order, vi = [], 0
        for mi, piece in enumerate(mxu):
            order.append(piece)
            while vi < len(vpu) and (vi + 1) * len(mxu) <= (mi + 1) * len(vpu):
                order.append(vpu[vi])
                vi += 1
        order += vpu[vi:]
        for piece in order:
            piece()

        @pl.when(jnp.logical_and(c >= 0, c < n_chunks))
        def _():
            for hf in range(2):
                cs = slice(hf * half, (hf + 1) * half)
                hl_ref[:, cs] = carry["h", hf]
                tailo_ref[:, cs] = carry["tail", hf]
            so_ref[...] = s_sc[...].reshape(GLA_HEADS, GLA_DK, GLA_DV)

    @pl.when(s % 2 == 0)
    def _():
        stages(u0_sc, u1_sc, z1_sc, z0_sc)

    @pl.when(s % 2 == 1)
    def _():
        stages(u1_sc, u0_sc, z0_sc, z1_sc)


def _mixer_seq_call(x, frame0, h0, s0, mix, l, *, nb, sl, tc, cg, row_block0, reset_first,
                    shared_state, alias_into, name):
    assert sl % tc == 0 and tc % cg == 0 and tc % SUBLANES == 0
    nj = sl // tc
    n_chunks = nb * nj
    xin = pl.BlockSpec((tc, D_MODEL), lambda s: (row_block0 + jnp.minimum(s, n_chunks - 1), 0))
    xres = pl.BlockSpec((tc, D_MODEL), lambda s: (row_block0 + jnp.clip(s - 2, 0, n_chunks - 1), 0))
    seq = lambda s: jnp.clip((s - 1) // nj, 0, nb - 1)
    if shared_state:
        st = lambda s: (0, 0, 0)
        st4 = lambda s: (0, 0, 0, 0)
    else:
        st = lambda s: (seq(s), 0, 0)
        st4 = lambda s: (seq(s), 0, 0, 0)
    (gmix, w_in, wlr, wg2, bg, convw, convb, wg, ba, bx, lam, gn, wout) = mix
    wmain_spec = pl.BlockSpec((None, D_MODEL, D_MAIN), lambda s: (l, 0, 0), pipeline_mode=pl.Buffered(1))
    args = [x, x, frame0, h0, s0, gmix, w_in, wlr, wg2, bg, convw, convb, wg, ba, bx, lam, gn, wout]
    in_specs = [
        xin, xres,
        pl.BlockSpec((None, SUBLANES, D_LRU), st),
        pl.BlockSpec((None, 1, D_LRU), st),
        pl.BlockSpec((None, GLA_HEADS, GLA_DK, GLA_DV), st4),
        _layer_spec(gmix, l), wmain_spec, _layer_spec(wlr, l), _layer_spec(wg2, l), _layer_spec(bg, l),
        _layer_spec(convw, l), _layer_spec(convb, l), _layer_spec(wg, l), _layer_spec(ba, l),
        _layer_spec(bx, l), _layer_spec(lam, l), _layer_spec(gn, l), _layer_spec(wout, l),
    ]
    aliases = {}
    if alias_into is not None:
        args.append(alias_into)
        in_specs.append(pl.BlockSpec(memory_space=pl.ANY))
        aliases = {len(args) - 1: 0}
    out_shape = [
        jax.ShapeDtypeStruct(x.shape, F32),
        jax.ShapeDtypeStruct((nb, 1, D_LRU), F32),
        jax.ShapeDtypeStruct((nb, GLA_HEADS, GLA_DK, GLA_DV), F32),
        jax.ShapeDtypeStruct((nb, SUBLANES, D_LRU), F32),
    ]
    out_specs = [
        xres,
        pl.BlockSpec((None, 1, D_LRU), lambda s: (seq(s), 0, 0)),
        pl.BlockSpec((None, GLA_HEADS, GLA_DK, GLA_DV), lambda s: (seq(s), 0, 0, 0)),
        pl.BlockSpec((None, SUBLANES, D_LRU), lambda s: (seq(s), 0, 0)),
    ]
    scratch = [
        pltpu.VMEM((tc, D_U), F32), pltpu.VMEM((tc, D_U), F32),
        pltpu.VMEM((tc, D_MODEL), F32), pltpu.VMEM((tc, D_MODEL), F32),
        pltpu.VMEM((SUBLANES, D_LRU), F32),
        pltpu.VMEM((1, D_LRU), F32),
        pltpu.VMEM((HK, GLA_DV), F32),
    ]
    kern = functools.partial(_mixer_seq_kernel, tc=tc, cg=cg, nj=nj, n_chunks=n_chunks,
                             reset_first=reset_first, aliased=alias_into is not None)
    return pl.pallas_call(
        kern, grid=(n_chunks + 2,), in_specs=in_specs, out_specs=out_specs, out_shape=out_shape,
        scratch_shapes=scratch, input_output_aliases=aliases,
        compiler_params=pltpu.CompilerParams(dimension_semantics=("arbitrary",),
                                             vmem_limit_bytes=VMEM_LIMIT),
        name=name,
    )(*args)


def _mixer_dec_kernel(*refs, nb, aliased):
    (x_ref, frame_ref, h0_ref, s0_ref, gmix_ref, wmain_ref, wlr_ref, wg2_ref, bg_ref,
     convw_ref, convb_ref, wg_ref, ba_ref, bx_ref, lam_ref, gn_ref, wout_ref) = refs[:17]
    (xo_ref, h_ref, xl_ref, so_ref,
     u_sc, z_sc, qs_sc, ks_sc, ke_sc, el_sc, o_sc) = refs[17 + (1 if aliased else 0):]
    ls = SUBLANES
    rows = nb * ls
    x = x_ref[...]
    _in_projection(x, gmix_ref, wmain_ref, wlr_ref, wg2_ref, bg_ref, u_sc)

    t = lax.broadcasted_iota(jnp.int32, (rows, D_LRU), 0) % ls
    xl = u_sc[:, 0:D_LRU]
    xl_ref[...] = xl
    frame = frame_ref[...]
    xc = convb_ref[...]
    for s in range(CONV_W - 1, -1, -1):
        if s == 0:
            sh = xl
        else:
            sh = jnp.where(t >= s, pltpu.roll(xl, s, 0), pltpu.roll(frame, (rows - ls + s) % rows, 0))
        xc = xc + convw_ref[CONV_W - 1 - s:CONV_W - s, :] * sh
    half = D_LRU // 2
    parts = [_lru_gates(xc[:, hf * half:(hf + 1) * half], wg_ref[hf], ba_ref[:, hf * half:(hf + 1) * half],
                        bx_ref[:, hf * half:(hf + 1) * half], lam_ref[:, hf * half:(hf + 1) * half])
             for hf in range(2)]
    a, mult, ix = (jnp.concatenate([p[n] for p in parts], axis=1) for n in range(3))
    a3, b3 = _scan_groups(a, mult * ix)
    h = (a3 * h0_ref[...].reshape(a3.shape) + b3).reshape(rows, D_LRU)
    h_ref[...] = h
    z_sc[:, 0:D_LRU] = h * jax.nn.gelu(u_sc[:, O_GL:O_GL + D_LRU])

    tq = lax.broadcasted_iota(jnp.int32, (rows, HK), 0) % ls
    b = _cumsum_rows(u_sc[:, D_MAIN:D_U], tq, ls)
    bl = jnp.where(tq == ls - 1, b, 0.0)
    s = 1
    while s < ls:
        bl = bl + jnp.where(tq + s < ls, pltpu.roll(bl, rows - s, 0), 0.0)
        s *= 2
    k = u_sc[:, O_K:O_K + HK]
    qs_sc[...] = (u_sc[:, O_Q:O_Q + HK] * (GLA_DK ** -0.5)) * jnp.exp(b)
    ks_sc[...] = k * jnp.exp(-b)
    ke_sc[...] = k * jnp.exp(bl - b)
    el_sc[...] = jnp.exp(bl)

    lane_head = lax.broadcasted_iota(jnp.int32, (ls, HK), 1) // GLA_DK
    causal = ((lax.broadcasted_iota(jnp.int32, (GLA_HEADS * ls, ls), 0) % ls)
              >= lax.broadcasted_iota(jnp.int32, (GLA_HEADS * ls, ls), 1))

    def body(bi, carry):
        rs = pl.ds(pl.multiple_of(bi * ls, ls), ls)
        s_all = s0_ref[bi].reshape(HK, GLA_DV)
        o, s_new = _gla_chunk(qs_sc[rs, :], ks_sc[rs, :], ke_sc[rs, :], el_sc[rs, :][0:1],
                              u_sc[rs, O_V:O_V + HV], s_all, causal, lane_head, ls)
        for hd in range(GLA_HEADS):
            so_ref[bi, hd] = s_new[hd]
            o_sc[rs, hd * GLA_DV:(hd + 1) * GLA_DV] = o[hd]
        return carry

    lax.fori_loop(0, nb, body, 0, unroll=4)

    gn = gn_ref[...]
    for hd in range(GLA_HEADS):
        vs = slice(hd * GLA_DV, (hd + 1) * GLA_DV)
        z_sc[:, D_LRU + hd * GLA_DV:D_LRU + (hd + 1) * GLA_DV] = _gla_output(
            o_sc[:, vs], u_sc[:, O_GO + hd * GLA_DV:O_GO + (hd + 1) * GLA_DV], gn)
    xo_ref[...] = x + _dot(z_sc[...], wout_ref[...])


def _mixer_dec_call(x, frame, h0, s0, s_out_prev, mix, l, *, nseq, nb, name):
    ls = SUBLANES
    assert nseq % nb == 0
    rows = nb * ls
    (gmix, w_in, wlr, wg2, bg, convw, convb, wg, ba, bx, lam, gn, wout) = mix
    rowspec = lambda w: pl.BlockSpec((rows, w), lambda i: (i, 0))
    sspec = pl.BlockSpec((None, nb, GLA_HEADS, GLA_DK, GLA_DV), lambda i: (l, i, 0, 0, 0))
    wmain_spec = pl.BlockSpec((None, D_MODEL, D_MAIN), lambda i: (l, 0, 0), pipeline_mode=pl.Buffered(1))
    in_specs = [
        rowspec(D_MODEL), rowspec(D_LRU), rowspec(D_LRU), sspec,
        _layer_spec(gmix, l), wmain_spec, _layer_spec(wlr, l), _layer_spec(wg2, l), _layer_spec(bg, l),
        _layer_spec(convw, l), _layer_spec(convb, l), _layer_spec(wg, l), _layer_spec(ba, l),
        _layer_spec(bx, l), _layer_spec(lam, l), _layer_spec(gn, l), _layer_spec(wout, l),
    ]
    out_shape = [
        jax.ShapeDtypeStruct(x.shape, F32),
        jax.ShapeDtypeStruct((nseq * ls, D_LRU), F32),
        jax.ShapeDtypeStruct((nseq * ls, D_LRU), F32),
        jax.ShapeDtypeStruct(s0.shape, F32),
    ]
    out_specs = [rowspec(D_MODEL), rowspec(D_LRU), rowspec(D_LRU), sspec]
    args = [x, frame, h0, s0, gmix, w_in, wlr, wg2, bg, convw, convb, wg, ba, bx, lam, gn, wout]
    aliases = {}
    if s_out_prev is not None:
        args.append(s_out_prev)
        in_specs.append(pl.BlockSpec(memory_space=pl.ANY))
        aliases = {len(args) - 1: 3}
    scratch = [
        pltpu.VMEM((rows, D_U), F32), pltpu.VMEM((rows, D_MODEL), F32),
        pltpu.VMEM((rows, HK), F32), pltpu.VMEM((rows, HK), F32), pltpu.VMEM((rows, HK), F32),
        pltpu.VMEM((rows, HK), F32), pltpu.VMEM((rows, HV), F32),
    ]
    return pl.pallas_call(
        functools.partial(_mixer_dec_kernel, nb=nb, aliased=s_out_prev is not None),
        grid=(nseq // nb,), in_specs=in_specs, out_specs=out_specs, out_shape=out_shape,
        scratch_shapes=scratch, input_output_aliases=aliases,
        compiler_params=pltpu.CompilerParams(dimension_semantics=("arbitrary",),
                                             vmem_limit_bytes=VMEM_LIMIT),
        name=name,
    )(*args)


def _block_diag_gates(wa, wx):
    per = LRU_BLOCKS // 2
    bw = wa.shape[-1]
    eye = jnp.eye(per, dtype=wa.dtype)

    def bd(w):
        return jnp.einsum("lncd,nm->lncmd", w, eye).reshape(w.shape[0], per * bw, per * bw)

    halves = [jnp.concatenate([bd(wa[:, hf * per:(hf + 1) * per]), bd(wx[:, hf * per:(hf + 1) * per])],
                              axis=-1) for hf in range(2)]
    return jnp.stack(halves, axis=1)


def kernel(x_prompt, x_sample, state_lru_h, state_lru_conv, state_gla_S, meta, norm_ffn1, w_ffn1_gu,
           w_ffn1_down, norm_mix, w_in, lru_conv_w, lru_conv_b, lru_wa, lru_ba, lru_wx, lru_bx,
           lru_lambda, gla_w_gate2, gla_b_gate, gla_norm, w_out, norm_ffn2, w_ffn2_gu, w_ffn2_down,
           norm_final):
    bp, lp, d = x_prompt.shape
    bs, ls, _ = x_sample.shape
    n_meta = meta.shape[0]
    depth = w_in.shape[0]
    assert d == D_MODEL and ls == SUBLANES and n_meta % SUBLANES == 0
    tp, ts = bp * lp, bs * ls
    tsm = ts + n_meta
    assert ts % n_meta == 0
    tm_p = 512 if tp % 512 == 0 else lp
    tm_s = tsm // 2 if (tsm // 2) % SUBLANES == 0 else tsm
    tc_p = 256 if lp % 256 == 0 else GLA_CHUNK
    nb_s = 32 if bs % 32 == 0 else bs
    tail = CONV_W - 1

    vec = lambda a: a.reshape(a.shape[0], 1, a.shape[-1])
    mix = (vec(norm_mix), w_in.astype(BF16),
           jnp.pad(w_in[:, :, D_MAIN:], ((0, 0), (0, 0), (0, LANES - GLA_RANK))).astype(BF16),
           jnp.pad(gla_w_gate2, ((0, 0), (0, LANES - GLA_RANK), (0, 0))).astype(BF16),
           vec(gla_b_gate), lru_conv_w, vec(lru_conv_b), _block_diag_gates(lru_wa, lru_wx).astype(BF16),
           vec(lru_ba), vec(lru_bx), vec(lru_lambda), vec(gla_norm), w_out.astype(BF16))
    n1, n2, nfin = vec(norm_ffn1), vec(norm_ffn2), norm_final.reshape(1, d)
    frames_s = jnp.pad(state_lru_conv, ((0, 0), (0, 0), (SUBLANES - tail, 0), (0, 0)))
    frames_s = frames_s.reshape(depth, ts, D_LRU)
    h0_s = jnp.broadcast_to(state_lru_h[:, :, None, :], (depth, bs, ls, D_LRU)).reshape(depth, ts, D_LRU)
    zero_frame = jnp.zeros((1, SUBLANES, D_LRU), F32)
    zero_h = jnp.zeros((1, 1, D_LRU), F32)
    zero_s = jnp.zeros((1, GLA_HEADS, GLA_DK, GLA_DV), F32)

    xp = x_prompt.reshape(tp, d)
    xs = jnp.concatenate([x_sample.reshape(ts, d), meta.astype(F32)], axis=0)

    hs_p, convs_p, ss_p, hs_s, convs_s, ss_s = [], [], [], [], [], None
    for l in range(depth):
        xp1, xs1 = _ffn_call(xp, xs, n1, w_ffn1_gu, w_ffn1_down, l, tm_a=tm_p, tm_b=tm_s, tf=256,
                             name=f"ffn1_{l}")

        xs2, h_s, xl_s, ss_s = _mixer_dec_call(xs1, frames_s[l], h0_s[l], state_gla_S, ss_s, mix, l,
                                               nseq=bs, nb=nb_s, name=f"mix_s{l}")
        hs_s.append(h_s.reshape(bs, ls, D_LRU)[:, ls - 1])
        convs_s.append(xl_s.reshape(bs, ls, D_LRU)[:, ls - tail:])

        xs2, h_m, s_m, tail_m = _mixer_seq_call(
            xs1, zero_frame, zero_h, zero_s, mix, l, nb=1, sl=n_meta, tc=n_meta, cg=n_meta,
            row_block0=ts // n_meta, reset_first=True, shared_state=False, alias_into=xs2,
            name=f"mix_m{l}")

        xp2, h_p, s_p, tail_p = _mixer_seq_call(
            xp1, tail_m, h_m, s_m, mix, l, nb=bp, sl=lp, tc=tc_p, cg=GLA_CHUNK, row_block0=0,
            reset_first=False, shared_state=True, alias_into=None, name=f"mix_p{l}")
        hs_p.append(h_p[:, 0])
        convs_p.append(tail_p[:, SUBLANES - tail:])
        ss_p.append(s_p)

        fin = nfin if l == depth - 1 else None
        xp, xs = _ffn_call(xp2, xs2, n2, w_ffn2_gu, w_ffn2_down, l, final=fin, tm_a=tm_p, tm_b=tm_s,
                           tf=256, name=f"ffn2_{l}")

    return (xp.reshape(bp, lp, d), xs[:ts].reshape(bs, ls, d),
            jnp.stack(hs_p), jnp.stack(convs_p), jnp.stack(ss_p),
            jnp.stack(hs_s), jnp.stack(convs_s), ss_s)
```

```python
import functools

import jax
import jax.numpy as jnp
from jax import lax
from jax.experimental import pallas as pl
from jax.experimental.pallas import tpu as pltpu

F32 = jnp.float32
BF16 = jnp.bfloat16

D_MODEL = 1024
D_FF = 2816
D_LRU = 512
LRU_BLOCKS = 8
CONV_W = 4
LRU_C = 8.0
GLA_HEADS = 4
GLA_DV = 128
GLA_DK = 64
GLA_RANK = 16
GLA_GATE_NORM = 16.0
GLA_CHUNK = 64
EPS = 1e-6
HK = GLA_HEADS * GLA_DK
HV = GLA_HEADS * GLA_DV
O_GL, O_Q, O_K, O_V, O_GO = D_LRU, 2 * D_LRU, 2 * D_LRU + HK, 2 * D_LRU + 2 * HK, 2 * D_LRU + 2 * HK + HV
D_MAIN = O_GO + HV
D_U = D_MAIN + HK
SUBLANES = 8
LANES = 128
VMEM_LIMIT = 58 * 1024 * 1024


def _dot(a, b):
    if a.dtype != b.dtype:
        a = a.astype(b.dtype)
    return jnp.dot(a, b, preferred_element_type=F32)


def _dot_nt(a, b):
    return lax.dot_general(a, b, (((1,), (1,)), ((), ())), preferred_element_type=F32)


def _dot_tn(a, b):
    return lax.dot_general(a, b, (((0,), (0,)), ((), ())), preferred_element_type=F32)


def _rms(x, g):
    ms = jnp.mean(x * x, axis=-1, keepdims=True)
    return (x * lax.rsqrt(ms + EPS)) * g


def _layer_spec(a, l, single=True):
    nd = a.ndim
    mode = dict(pipeline_mode=pl.Buffered(1)) if single else {}
    return pl.BlockSpec((None,) + a.shape[1:], lambda *_: (l,) + (0,) * (nd - 1), **mode)


def _whole_spec(a):
    nd = a.ndim
    return pl.BlockSpec(a.shape, lambda *_: (0,) * nd, pipeline_mode=pl.Buffered(1))


def _ffn_kernel(*refs, tf, has_final, n_a):
    if has_final:
        xa_ref, xb_ref, g_ref, wgu_ref, wd_ref, gfin_ref, oa_ref, ob_ref = refs
    else:
        xa_ref, xb_ref, g_ref, wgu_ref, wd_ref, oa_ref, ob_ref = refs

    def tile(x_ref, xo_ref):
        x = x_ref[...]
        xn = _rms(x, g_ref[...])
        acc = None
        for c in range(D_FF // tf):
            gate = _dot(xn, wgu_ref[:, c * tf:(c + 1) * tf])
            up = _dot(xn, wgu_ref[:, D_FF + c * tf:D_FF + (c + 1) * tf])
            d = _dot(jax.nn.silu(gate) * up, wd_ref[c * tf:(c + 1) * tf, :])
            acc = d if acc is None else acc + d
        x = x + 0.5 * acc
        xo_ref[...] = _rms(x, gfin_ref[...]) if has_final else x

    i = pl.program_id(0)

    @pl.when(i < n_a)
    def _():
        tile(xa_ref, oa_ref)

    @pl.when(i >= n_a)
    def _():
        tile(xb_ref, ob_ref)


def _ffn_call(xa, xb, norm, wgu, wd, l, *, final=None, tm_a, tm_b, tf, name):
    (ta, d), (tb, _) = xa.shape, xb.shape
    assert ta % tm_a == 0 and tb % tm_b == 0 and D_FF % tf == 0
    n_a, n_b = ta // tm_a, tb // tm_b
    rows_a = pl.BlockSpec((tm_a, d), lambda i: (jnp.minimum(i, n_a - 1), 0))
    rows_b = pl.BlockSpec((tm_b, d), lambda i: (jnp.maximum(i - n_a, 0), 0))
    args = [xa, xb, norm, wgu, wd]
    specs = [rows_a, rows_b, _layer_spec(norm, l), _layer_spec(wgu, l), _layer_spec(wd, l)]
    if final is not None:
        args.append(final)
        specs.append(_whole_spec(final))
    return pl.pallas_call(
        functools.partial(_ffn_kernel, tf=tf, has_final=final is not None, n_a=n_a),
        grid=(n_a + n_b,), in_specs=specs, out_specs=[rows_a, rows_b],
        out_shape=[jax.ShapeDtypeStruct((ta, d), F32), jax.ShapeDtypeStruct((tb, d), F32)],
        compiler_params=pltpu.CompilerParams(dimension_semantics=("arbitrary",),
                                             vmem_limit_bytes=VMEM_LIMIT),
        name=name,
    )(*args)


def _in_projection(x, gmix_ref, wmain_ref, wlr_ref, wg2_ref, bg_ref, u_sc):
    hn = _rms(x, gmix_ref[...]).astype(wmain_ref.dtype)
    u_sc[:, 0:D_MAIN] = _dot(hn, wmain_ref[...])
    lr = _dot(hn, wlr_ref[...])
    zg = _dot(lr, wg2_ref[...]) + bg_ref[...]
    u_sc[:, D_MAIN:D_U] = jax.nn.log_sigmoid(zg) / GLA_GATE_NORM


def _lru_gates(xc, p, ba, bx, lam):
    half = xc.shape[1]
    r = jax.nn.sigmoid(p[:, 0:half] + ba)
    i = jax.nn.sigmoid(p[:, half:] + bx)
    log_a = -LRU_C * r * jax.nn.softplus(-lam)
    a = jnp.exp(log_a)
    y = -jnp.tanh(log_a) * (a * a + 1.0)
    mult = jnp.where(y > 0.0, y * lax.rsqrt(y), 0.0)
    return a, mult, i * xc


def _scan_groups(a, b):
    rows, cols = a.shape
    a3 = a.reshape(rows // SUBLANES, SUBLANES, cols)
    b3 = b.reshape(rows // SUBLANES, SUBLANES, cols)
    t = lax.broadcasted_iota(jnp.int32, a3.shape, 1)
    s = 1
    while s < SUBLANES:
        keep = t >= s
        b3 = jnp.where(keep, a3 * pltpu.roll(b3, s, 1) + b3, b3)
        a3 = jnp.where(keep, a3 * pltpu.roll(a3, s, 1), a3)
        s *= 2
    return a3, b3


def _scan_rows(a, b, h0):
    a3, b3 = _scan_groups(a, b)
    carry, out = h0, []
    for g in range(a3.shape[0]):
        hg = a3[g] * carry + b3[g]
        out.append(hg)
        carry = hg[SUBLANES - 1:SUBLANES]
    return out[0] if len(out) == 1 else jnp.concatenate(out, axis=0)


def _cumsum_rows(x, t, n):
    s = 1
    while s < n:
        x = x + jnp.where(t >= s, pltpu.roll(x, s, 0), 0.0)
        s *= 2
    return x


def _head_stack(q_s, lane_head):
    return jnp.concatenate([jnp.where(lane_head == hd, q_s, 0.0) for hd in range(GLA_HEADS)], axis=0)


def _gla_chunk(q_s, k_s, k_end, el, v, s_all, causal, lane_head, n):
    stack = _head_stack(q_s, lane_head)
    o_inter = _dot(stack, s_all)
    att = jnp.where(causal, _dot_nt(stack, k_s), 0.0)
    kv = _dot_tn(k_end, v)
    el_t = jnp.broadcast_to(el, (LANES, HK)).T
    o, s_new = [], []
    for hd in range(GLA_HEADS):
        rows = slice(hd * n, (hd + 1) * n)
        ks = slice(hd * GLA_DK, (hd + 1) * GLA_DK)
        vs = slice(hd * GLA_DV, (hd + 1) * GLA_DV)
        o.append(o_inter[rows] + _dot(att[rows], v[:, vs]))
        s_new.append(el_t[ks] * s_all[ks] + kv[ks, vs])
    return o, s_new


def _gla_output(o, go, gn):
    return _rms(o, gn) * jax.nn.silu(go)


def _mixer_seq_kernel(*refs, tc, cg, nj, reset_first, aliased):
    (xin_ref, frame0_ref, h0_ref, s0_ref, gmix_ref, wmain_ref, wlr_ref, wg2_ref, bg_ref,
     convw_ref, convb_ref, wg_ref, ba_ref, bx_ref, lam_ref, gn_ref, wout_ref) = refs[:17]
    refs = refs[17 + (1 if aliased else 0):]
    xo_ref, hl_ref, so_ref, tailo_ref, u, z, tail_sc, h_sc, s_sc = refs
    c = pl.program_id(0)

    @pl.when(c % nj == 0)
    def _():
        tail_sc[...] = frame0_ref[...]
        h_sc[...] = h0_ref[...]
        s_sc[...] = s0_ref[...].reshape(HK, GLA_DV)

    def stages():
        x = xin_ref[...]
        hn = _rms(x, gmix_ref[...]).astype(wmain_ref.dtype)
        tile_w = 2 * LANES
        carry = {}
        half = D_LRU // 2

        def in_tile(t):
            cols = slice(t * tile_w, (t + 1) * tile_w)
            u[:, cols] = _dot(hn, wmain_ref[:, cols])

        def gate_tile():
            lr = _dot(hn, wlr_ref[...])
            zg = _dot(lr, wg2_ref[...]) + bg_ref[...]
            u[:, D_MAIN:D_U] = jax.nn.log_sigmoid(zg) / GLA_GATE_NORM

        def out_lru():
            xo_ref[...] = x + _dot(z[:, 0:D_LRU], wout_ref[0:D_LRU, :])

        def out_gla():
            xo_ref[...] += _dot(z[:, D_LRU:D_MODEL], wout_ref[D_LRU:D_MODEL, :])

        def lru_front(hf):
            cs = slice(hf * half, (hf + 1) * half)
            xl = u[:, cs]
            row8 = lax.broadcasted_iota(jnp.int32, (SUBLANES, half), 0)
            xc = convb_ref[:, cs]
            for sft in range(CONV_W - 1, -1, -1):
                if sft == 0:
                    sh = xl
                else:
                    rolled = pltpu.roll(xl, sft, 0)
                    top = jnp.where(row8 < sft, pltpu.roll(tail_sc[:, cs], sft, 0), rolled[0:SUBLANES])
                    sh = top if tc == SUBLANES else jnp.concatenate([top, rolled[SUBLANES:]], axis=0)
                xc = xc + convw_ref[CONV_W - 1 - sft:CONV_W - sft, cs] * sh
            carry["tail", hf] = xl[tc - SUBLANES:tc]
            tail_sc[:, cs] = carry["tail", hf]
            carry["xc", hf] = xc

        def lru_gate_mm(hf):
            carry["p", hf] = _dot(carry["xc", hf], wg_ref[hf])

        def lru_mid(hf):
            cs = slice(hf * half, (hf + 1) * half)
            a, mult, ix = _lru_gates(carry["xc", hf], carry["p", hf], ba_ref[:, cs], bx_ref[:, cs],
                                     lam_ref[:, cs])
            if reset_first:
                row = lax.broadcasted_iota(jnp.int32, (tc, half), 0)
                first = jnp.logical_and(row == 0, c % nj == 0)
                mult = jnp.where(first, 1.0, mult)
                a = jnp.where(first, 0.0, a)
            carry["a", hf], carry["b", hf] = a, mult * ix

        def lru_back(hf):
            cs = slice(hf * half, (hf + 1) * half)
            h = _scan_rows(carry["a", hf], carry["b", hf], h_sc[:, cs])
            carry["h", hf] = h[tc - 1:tc]
            h_sc[:, cs] = carry["h", hf]
            z[:, cs] = h * jax.nn.gelu(u[:, O_GL + hf * half:O_GL + (hf + 1) * half])

        def gla_piece(ci):
            tq = lax.broadcasted_iota(jnp.int32, (cg, HK), 0)
            lane_head = lax.broadcasted_iota(jnp.int32, (cg, HK), 1) // GLA_DK
            causal = ((lax.broadcasted_iota(jnp.int32, (GLA_HEADS * cg, cg), 0) % cg)
                      >= lax.broadcasted_iota(jnp.int32, (GLA_HEADS * cg, cg), 1))
            rs = slice(ci * cg, (ci + 1) * cg)
            b = _cumsum_rows(u[rs, D_MAIN:D_U], tq, cg)
            bl = b[cg - 1:cg]
            q = u[rs, O_Q:O_Q + HK] * (GLA_DK ** -0.5)
            k = u[rs, O_K:O_K + HK]
            o, s_new = _gla_chunk(q * jnp.exp(b), k * jnp.exp(-b), k * jnp.exp(bl - b), jnp.exp(bl),
                                  u[rs, O_V:O_V + HV], s_sc[...], causal, lane_head, cg)
            for hd in range(GLA_HEADS):
                s_sc[hd * GLA_DK:(hd + 1) * GLA_DK, :] = s_new[hd]
                z[rs, D_LRU + hd * GLA_DV:D_LRU + (hd + 1) * GLA_DV] = _gla_output(
                    o[hd], u[rs, O_GO + hd * GLA_DV:O_GO + (hd + 1) * GLA_DV], gn_ref[...])

        lru = {n: functools.partial(f, hf) for hf in range(2)
               for n, f in ((f"front{hf}", lru_front), (f"gmm{hf}", lru_gate_mm), (f"mid{hf}", lru_mid),
                            (f"back{hf}", lru_back))}
        tiles = [functools.partial(in_tile, t) for t in range(D_MAIN // tile_w)]
        order = (tiles[0:3] + [lru["front0"], tiles[3], lru["front1"], lru["gmm0"], lru["gmm1"],
                               tiles[4], lru["mid0"], tiles[5], lru["back0"], tiles[6], tiles[7],
                               lru["mid1"], tiles[8], lru["back1"], tiles[9], gate_tile, out_lru]
                 + [functools.partial(gla_piece, ci) for ci in range(tc // cg)] + [out_gla])
        assert len(tiles) == 10
        for piece in order:
            piece()

        for hf in range(2):
            cs = slice(hf * half, (hf + 1) * half)
            hl_ref[:, cs] = carry["h", hf]
            tailo_ref[:, cs] = carry["tail", hf]
        so_ref[...] = s_sc[...].reshape(GLA_HEADS, GLA_DK, GLA_DV)

    stages()


def _mixer_seq_call(x, frame0, h0, s0, mix, l, *, nb, sl, tc, cg, row_block0, reset_first,
                    shared_state, alias_into, name):
    assert sl % tc == 0 and tc % cg == 0 and tc % SUBLANES == 0
    nj = sl // tc
    xrow = pl.BlockSpec((tc, D_MODEL), lambda s: (row_block0 + s, 0))
    seq = lambda s: s // nj
    if shared_state:
        st = lambda s: (0, 0, 0)
        st4 = lambda s: (0, 0, 0, 0)
    else:
        st = lambda s: (seq(s), 0, 0)
        st4 = lambda s: (seq(s), 0, 0, 0)
    (gmix, w_in, wlr, wg2, bg, convw, convb, wg, ba, bx, lam, gn, wout) = mix
    wmain_spec = pl.BlockSpec((None, D_MODEL, D_MAIN), lambda s: (l, 0, 0), pipeline_mode=pl.Buffered(1))
    args = [x, frame0, h0, s0, gmix, w_in, wlr, wg2, bg, convw, convb, wg, ba, bx, lam, gn, wout]
    in_specs = [
        xrow,
        pl.BlockSpec((None, SUBLANES, D_LRU), st),
        pl.BlockSpec((None, 1, D_LRU), st),
        pl.BlockSpec((None, GLA_HEADS, GLA_DK, GLA_DV), st4),
        _layer_spec(gmix, l), wmain_spec, _layer_spec(wlr, l), _layer_spec(wg2, l), _layer_spec(bg, l),
        _layer_spec(convw, l), _layer_spec(convb, l), _layer_spec(wg, l), _layer_spec(ba, l),
        _layer_spec(bx, l), _layer_spec(lam, l), _layer_spec(gn, l), _layer_spec(wout, l),
    ]
    aliases = {}
    if alias_into is not None:
        args.append(alias_into)
        in_specs.append(pl.BlockSpec(memory_space=pl.ANY))
        aliases = {len(args) - 1: 0}
    out_shape = [
        jax.ShapeDtypeStruct(x.shape, F32),
        jax.ShapeDtypeStruct((nb, 1, D_LRU), F32),
        jax.ShapeDtypeStruct((nb, GLA_HEADS, GLA_DK, GLA_DV), F32),
        jax.ShapeDtypeStruct((nb, SUBLANES, D_LRU), F32),
    ]
    out_specs = [
        xrow,
        pl.BlockSpec((None, 1, D_LRU), lambda s: (seq(s), 0, 0)),
        pl.BlockSpec((None, GLA_HEADS, GLA_DK, GLA_DV), lambda s: (seq(s), 0, 0, 0)),
        pl.BlockSpec((None, SUBLANES, D_LRU), lambda s: (seq(s), 0, 0)),
    ]
    scratch = [
        pltpu.VMEM((tc, D_U), F32),
        pltpu.VMEM((tc, D_MODEL), F32),
        pltpu.VMEM((SUBLANES, D_LRU), F32),
        pltpu.VMEM((1, D_LRU), F32),
        pltpu.VMEM((HK, GLA_DV), F32),
    ]
    kern = functools.partial(_mixer_seq_kernel, tc=tc, cg=cg, nj=nj, reset_first=reset_first,
                             aliased=alias_into is not None)
    return pl.pallas_call(
        kern, grid=(nb * nj,), in_specs=in_specs, out_specs=out_specs, out_shape=out_shape,
        scratch_shapes=scratch, input_output_aliases=aliases,
        compiler_params=pltpu.CompilerParams(dimension_semantics=("arbitrary",),
                                             vmem_limit_bytes=VMEM_LIMIT),
        name=name,
    )(*args)


def _mixer_dec_kernel(*refs, nb, aliased):
    (x_ref, frame_ref, h0_ref, s0_ref, gmix_ref, wmain_ref, wlr_ref, wg2_ref, bg_ref,
     convw_ref, convb_ref, wg_ref, ba_ref, bx_ref, lam_ref, gn_ref, wout_ref) = refs[:17]
    (xo_ref, h_ref, xl_ref, so_ref,
     u_sc, z_sc, qs_sc, ks_sc, ke_sc, el_sc, o_sc) = refs[17 + (1 if aliased else 0):]
    ls = SUBLANES
    rows = nb * ls
    x = x_ref[...]
    _in_projection(x, gmix_ref, wmain_ref, wlr_ref, wg2_ref, bg_ref, u_sc)

    t = lax.broadcasted_iota(jnp.int32, (rows, D_LRU), 0) % ls
    xl = u_sc[:, 0:D_LRU]
    xl_ref[...] = xl
    frame = frame_ref[...]
    xc = convb_ref[...]
    for s in range(CONV_W - 1, -1, -1):
        if s == 0:
            sh = xl
        else:
            sh = jnp.where(t >= s, pltpu.roll(xl, s, 0), pltpu.roll(frame, (rows - ls + s) % rows, 0))
        xc = xc + convw_ref[CONV_W - 1 - s:CONV_W - s, :] * sh
    half = D_LRU // 2
    parts = []
    for hf in range(2):
        cs = slice(hf * half, (hf + 1) * half)
        parts.append(_lru_gates(xc[:, cs], _dot(xc[:, cs], wg_ref[hf]), ba_ref[:, cs], bx_ref[:, cs],
                                lam_ref[:, cs]))
    a, mult, ix = (jnp.concatenate([p[n] for p in parts], axis=1) for n in range(3))
    a3, b3 = _scan_groups(a, mult * ix)
    h = (a3 * h0_ref[...].reshape(a3.shape) + b3).reshape(rows, D_LRU)
    h_ref[...] = h
    z_sc[:, 0:D_LRU] = h * jax.nn.gelu(u_sc[:, O_GL:O_GL + D_LRU])

    tq = lax.broadcasted_iota(jnp.int32, (rows, HK), 0) % ls
    b = _cumsum_rows(u_sc[:, D_MAIN:D_U], tq, ls)
    bl = jnp.where(tq == ls - 1, b, 0.0)
    s = 1
    while s < ls:
        bl = bl + jnp.where(tq + s < ls, pltpu.roll(bl, rows - s, 0), 0.0)
        s *= 2
    k = u_sc[:, O_K:O_K + HK]
    qs_sc[...] = (u_sc[:, O_Q:O_Q + HK] * (GLA_DK ** -0.5)) * jnp.exp(b)
    ks_sc[...] = k * jnp.exp(-b)
    ke_sc[...] = k * jnp.exp(bl - b)
    el_sc[...] = jnp.exp(bl)

    lane_head = lax.broadcasted_iota(jnp.int32, (ls, HK), 1) // GLA_DK
    causal = ((lax.broadcasted_iota(jnp.int32, (GLA_HEADS * ls, ls), 0) % ls)
              >= lax.broadcasted_iota(jnp.int32, (GLA_HEADS * ls, ls), 1))

    def body(bi, carry):
        rs = pl.ds(pl.multiple_of(bi * ls, ls), ls)
        s_all = s0_ref[bi].reshape(HK, GLA_DV)
        o, s_new = _gla_chunk(qs_sc[rs, :], ks_sc[rs, :], ke_sc[rs, :], el_sc[rs, :][0:1],
                              u_sc[rs, O_V:O_V + HV], s_all, causal, lane_head, ls)
        for hd in range(GLA_HEADS):
            so_ref[bi, hd] = s_new[hd]
            o_sc[rs, hd * GLA_DV:(hd + 1) * GLA_DV] = o[hd]
        return carry

    lax.fori_loop(0, nb, body, 0, unroll=4)

    gn = gn_ref[...]
    for hd in range(GLA_HEADS):
        vs = slice(hd * GLA_DV, (hd + 1) * GLA_DV)
        z_sc[:, D_LRU + hd * GLA_DV:D_LRU + (hd + 1) * GLA_DV] = _gla_output(
            o_sc[:, vs], u_sc[:, O_GO + hd * GLA_DV:O_GO + (hd + 1) * GLA_DV], gn)
    xo_ref[...] = x + _dot(z_sc[...], wout_ref[...])


def _mixer_dec_call(x, frame, h0, s0, s_out_prev, mix, l, *, nseq, nb, name):
    ls = SUBLANES
    assert nseq % nb == 0
    rows = nb * ls
    (gmix, w_in, wlr, wg2, bg, convw, convb, wg, ba, bx, lam, gn, wout) = mix
    rowspec = lambda w: pl.BlockSpec((rows, w), lambda i: (i, 0))
    sspec = pl.BlockSpec((None, nb, GLA_HEADS, GLA_DK, GLA_DV), lambda i: (l, i, 0, 0, 0))
    wmain_spec = pl.BlockSpec((None, D_MODEL, D_MAIN), lambda i: (l, 0, 0), pipeline_mode=pl.Buffered(1))
    in_specs = [
        rowspec(D_MODEL), rowspec(D_LRU), rowspec(D_LRU), sspec,
        _layer_spec(gmix, l), wmain_spec, _layer_spec(wlr, l), _layer_spec(wg2, l), _layer_spec(bg, l),
        _layer_spec(convw, l), _layer_spec(convb, l), _layer_spec(wg, l), _layer_spec(ba, l),
        _layer_spec(bx, l), _layer_spec(lam, l), _layer_spec(gn, l), _layer_spec(wout, l),
    ]
    out_shape = [
        jax.ShapeDtypeStruct(x.shape, F32),
        jax.ShapeDtypeStruct((nseq * ls, D_LRU), F32),
        jax.ShapeDtypeStruct((nseq * ls, D_LRU), F32),
        jax.ShapeDtypeStruct(s0.shape, F32),
    ]
    out_specs = [rowspec(D_MODEL), rowspec(D_LRU), rowspec(D_LRU), sspec]
    args = [x, frame, h0, s0, gmix, w_in, wlr, wg2, bg, convw, convb, wg, ba, bx, lam, gn, wout]
    aliases = {}
    if s_out_prev is not None:
        args.append(s_out_prev)
        in_specs.append(pl.BlockSpec(memory_space=pl.ANY))
        aliases = {len(args) - 1: 3}
    scratch = [
        pltpu.VMEM((rows, D_U), F32), pltpu.VMEM((rows, D_MODEL), F32),
        pltpu.VMEM((rows, HK), F32), pltpu.VMEM((rows, HK), F32), pltpu.VMEM((rows, HK), F32),
        pltpu.VMEM((rows, HK), F32), pltpu.VMEM((rows, HV), F32),
    ]
    return pl.pallas_call(
        functools.partial(_mixer_dec_kernel, nb=nb, aliased=s_out_prev is not None),
        grid=(nseq // nb,), in_specs=in_specs, out_specs=out_specs, out_shape=out_shape,
        scratch_shapes=scratch, input_output_aliases=aliases,
        compiler_params=pltpu.CompilerParams(dimension_semantics=("arbitrary",),
                                             vmem_limit_bytes=VMEM_LIMIT),
        name=name,
    )(*args)


def _block_diag_gates(wa, wx):
    per = LRU_BLOCKS // 2
    bw = wa.shape[-1]
    eye = jnp.eye(per, dtype=wa.dtype)

    def bd(w):
        return jnp.einsum("lncd,nm->lncmd", w, eye).reshape(w.shape[0], per * bw, per * bw)

    halves = [jnp.concatenate([bd(wa[:, hf * per:(hf + 1) * per]), bd(wx[:, hf * per:(hf + 1) * per])],
                              axis=-1) for hf in range(2)]
    return jnp.stack(halves, axis=1)


def kernel(x_prompt, x_sample, state_lru_h, state_lru_conv, state_gla_S, meta, norm_ffn1, w_ffn1_gu,
           w_ffn1_down, norm_mix, w_in, lru_conv_w, lru_conv_b, lru_wa, lru_ba, lru_wx, lru_bx,
           lru_lambda, gla_w_gate2, gla_b_gate, gla_norm, w_out, norm_ffn2, w_ffn2_gu, w_ffn2_down,
           norm_final):
    bp, lp, d = x_prompt.shape
    bs, ls, _ = x_sample.shape
    n_meta = meta.shape[0]
    depth = w_in.shape[0]
    assert d == D_MODEL and ls == SUBLANES and n_meta % SUBLANES == 0
    tp, ts = bp * lp, bs * ls
    tsm = ts + n_meta
    assert ts % n_meta == 0
    tm_p = 512 if tp % 512 == 0 else lp
    tm_s = tsm // 2 if (tsm // 2) % SUBLANES == 0 else tsm
    tc_p = 256 if lp % 256 == 0 else GLA_CHUNK
    nb_s = 32 if bs % 32 == 0 else bs
    tail = CONV_W - 1

    vec = lambda a: a.reshape(a.shape[0], 1, a.shape[-1])
    mix = (vec(norm_mix), w_in.astype(BF16),
           jnp.pad(w_in[:, :, D_MAIN:], ((0, 0), (0, 0), (0, LANES - GLA_RANK))).astype(BF16),
           jnp.pad(gla_w_gate2, ((0, 0), (0, LANES - GLA_RANK), (0, 0))).astype(BF16),
           vec(gla_b_gate), lru_conv_w, vec(lru_conv_b), _block_diag_gates(lru_wa, lru_wx).astype(BF16),
           vec(lru_ba), vec(lru_bx), vec(lru_lambda), vec(gla_norm), w_out.astype(BF16))
    n1, n2, nfin = vec(norm_ffn1), vec(norm_ffn2), norm_final.reshape(1, d)
    frames_s = jnp.pad(state_lru_conv, ((0, 0), (0, 0), (SUBLANES - tail, 0), (0, 0)))
    frames_s = frames_s.reshape(depth, ts, D_LRU)
    h0_s = jnp.broadcast_to(state_lru_h[:, :, None, :], (depth, bs, ls, D_LRU)).reshape(depth, ts, D_LRU)
    zero_frame = jnp.zeros((1, SUBLANES, D_LRU), F32)
    zero_h = jnp.zeros((1, 1, D_LRU), F32)
    zero_s = jnp.zeros((1, GLA_HEADS, GLA_DK, GLA_DV), F32)

    xp = x_prompt.reshape(tp, d)
    xs = jnp.concatenate([x_sample.reshape(ts, d), meta.astype(F32)], axis=0)

    hs_p, convs_p, ss_p, hs_s, convs_s, ss_s = [], [], [], [], [], None
    for l in range(depth):
        xp1, xs1 = _ffn_call(xp, xs, n1, w_ffn1_gu, w_ffn1_down, l, tm_a=tm_p, tm_b=tm_s, tf=256,
                             name=f"ffn1_{l}")

        xs2, h_s, xl_s, ss_s = _mixer_dec_call(xs1, frames_s[l], h0_s[l], state_gla_S, ss_s, mix, l,
                                               nseq=bs, nb=nb_s, name=f"mix_s{l}")
        hs_s.append(h_s.reshape(bs, ls, D_LRU)[:, ls - 1])
        convs_s.append(xl_s.reshape(bs, ls, D_LRU)[:, ls - tail:])

        xs2, h_m, s_m, tail_m = _mixer_seq_call(
            xs1, zero_frame, zero_h, zero_s, mix, l, nb=1, sl=n_meta, tc=n_meta, cg=n_meta,
            row_block0=ts // n_meta, reset_first=True, shared_state=False, alias_into=xs2,
            name=f"mix_m{l}")

        xp2, h_p, s_p, tail_p = _mixer_seq_call(
            xp1, tail_m, h_m, s_m, mix, l, nb=bp, sl=lp, tc=tc_p, cg=GLA_CHUNK, row_block0=0,
            reset_first=False, shared_state=True, alias_into=None, name=f"mix_p{l}")
        hs_p.append(h_p[:, 0])
        convs_p.append(tail_p[:, SUBLANES - tail:])
        ss_p.append(s_p)

        fin = nfin if l == depth - 1 else None
        xp, xs = _ffn_call(xp2, xs2, n2, w_ffn2_gu, w_ffn2_down, l, final=fin, tm_a=tm_p, tm_b=tm_s,
                           tf=256, name=f"ffn2_{l}")

    return (xp.reshape(bp, lp, d), xs[:ts].reshape(bs, ls, d),
            jnp.stack(hs_p), jnp.stack(convs_p), jnp.stack(ss_p),
            jnp.stack(hs_s), jnp.stack(convs_s), ss_s)
```

```python
import functools

import jax
import jax.numpy as jnp
from jax import lax
from jax.experimental import pallas as pl
from jax.experimental.pallas import tpu as pltpu

F32 = jnp.float32
BF16 = jnp.bfloat16

D_MODEL = 1024
D_FF = 2816
D_LRU = 512
LRU_BLOCKS = 8
CONV_W = 4
LRU_C = 8.0
GLA_HEADS = 4
GLA_DV = 128
GLA_DK = 64
GLA_RANK = 16
GLA_GATE_NORM = 16.0
GLA_CHUNK = 64
EPS = 1e-6
HK = GLA_HEADS * GLA_DK
HV = GLA_HEADS * GLA_DV
O_GL, O_Q, O_K, O_V, O_GO = D_LRU, 2 * D_LRU, 2 * D_LRU + HK, 2 * D_LRU + 2 * HK, 2 * D_LRU + 2 * HK + HV
D_MAIN = O_GO + HV
D_U = D_MAIN + HK
SUBLANES = 8
LANES = 128
VMEM_LIMIT = 58 * 1024 * 1024


def _dot(a, b):
    if a.dtype != b.dtype:
        a = a.astype(b.dtype)
    return jnp.dot(a, b, preferred_element_type=F32)


def _dot_nt(a, b):
    return lax.dot_general(a, b, (((1,), (1,)), ((), ())), preferred_element_type=F32)


def _dot_tn(a, b):
    return lax.dot_general(a, b, (((0,), (0,)), ((), ())), preferred_element_type=F32)


def _rms(x, g):
    ms = jnp.mean(x * x, axis=-1, keepdims=True)
    return (x * lax.rsqrt(ms + EPS)) * g


def _layer_spec(a, l):
    nd = a.ndim
    return pl.BlockSpec((None,) + a.shape[1:], lambda *_: (l,) + (0,) * (nd - 1),
                        pipeline_mode=pl.Buffered(1))


def _whole_spec(a):
    nd = a.ndim
    return pl.BlockSpec(a.shape, lambda *_: (0,) * nd, pipeline_mode=pl.Buffered(1))


def _ffn_kernel(*refs, tf, has_final, n_a):
    if has_final:
        xa_ref, xb_ref, g_ref, wgu_ref, wd_ref, gfin_ref, oa_ref, ob_ref = refs
    else:
        xa_ref, xb_ref, g_ref, wgu_ref, wd_ref, oa_ref, ob_ref = refs

    def tile(x_ref, xo_ref):
        x = x_ref[...]
        xn = _rms(x, g_ref[...])
        acc = None
        for c in range(D_FF // tf):
            gate = _dot(xn, wgu_ref[:, c * tf:(c + 1) * tf])
            up = _dot(xn, wgu_ref[:, D_FF + c * tf:D_FF + (c + 1) * tf])
            d = _dot(jax.nn.silu(gate) * up, wd_ref[c * tf:(c + 1) * tf, :])
            acc = d if acc is None else acc + d
        x = x + 0.5 * acc
        xo_ref[...] = _rms(x, gfin_ref[...]) if has_final else x

    i = pl.program_id(0)

    @pl.when(i < n_a)
    def _():
        tile(xa_ref, oa_ref)

    @pl.when(i >= n_a)
    def _():
        tile(xb_ref, ob_ref)


def _ffn_call(xa, xb, norm, wgu, wd, l, *, final=None, tm_a, tm_b, tf, name):
    (ta, d), (tb, _) = xa.shape, xb.shape
    assert ta % tm_a == 0 and tb % tm_b == 0 and D_FF % tf == 0
    n_a, n_b = ta // tm_a, tb // tm_b
    rows_a = pl.BlockSpec((tm_a, d), lambda i: (jnp.minimum(i, n_a - 1), 0))
    rows_b = pl.BlockSpec((tm_b, d), lambda i: (jnp.maximum(i - n_a, 0), 0))
    args = [xa, xb, norm, wgu, wd]
    specs = [rows_a, rows_b, _layer_spec(norm, l), _layer_spec(wgu, l), _layer_spec(wd, l)]
    if final is not None:
        args.append(final)
        specs.append(_whole_spec(final))
    return pl.pallas_call(
        functools.partial(_ffn_kernel, tf=tf, has_final=final is not None, n_a=n_a),
        grid=(n_a + n_b,), in_specs=specs, out_specs=[rows_a, rows_b],
        out_shape=[jax.ShapeDtypeStruct((ta, d), F32), jax.ShapeDtypeStruct((tb, d), F32)],
        compiler_params=pltpu.CompilerParams(dimension_semantics=("arbitrary",),
                                             vmem_limit_bytes=VMEM_LIMIT),
        name=name,
    )(*args)


def _in_projection(x, gmix_ref, wmain_ref, wlr_ref, wg2_ref, bg_ref, u_sc):
    hn = _rms(x, gmix_ref[...]).astype(wmain_ref.dtype)
    u_sc[:, 0:D_MAIN] = _dot(hn, wmain_ref[...])
    lr = _dot(hn, wlr_ref[...])
    zg = _dot(lr, wg2_ref[...]) + bg_ref[...]
    u_sc[:, D_MAIN:D_U] = jax.nn.log_sigmoid(zg) / GLA_GATE_NORM


def _lru_gates(xc, p, ba, bx, lam):
    half = xc.shape[1]
    r = jax.nn.sigmoid(p[:, 0:half] + ba)
    i = jax.nn.sigmoid(p[:, half:] + bx)
    log_a = -LRU_C * r * jax.nn.softplus(-lam)
    a = jnp.exp(log_a)
    y = -jnp.tanh(log_a) * (a * a + 1.0)
    mult = jnp.where(y > 0.0, y * lax.rsqrt(y), 0.0)
    return a, mult, i * xc


def _scan_groups(a, b):
    rows, cols = a.shape
    a3 = a.reshape(rows // SUBLANES, SUBLANES, cols)
    b3 = b.reshape(rows // SUBLANES, SUBLANES, cols)
    t = lax.broadcasted_iota(jnp.int32, a3.shape, 1)
    s = 1
    while s < SUBLANES:
        keep = t >= s
        b3 = jnp.where(keep, a3 * pltpu.roll(b3, s, 1) + b3, b3)
        a3 = jnp.where(keep, a3 * pltpu.roll(a3, s, 1), a3)
        s *= 2
    return a3, b3


def _scan_rows(a, b, h0):
    a3, b3 = _scan_groups(a, b)
    carry, out = h0, []
    for g in range(a3.shape[0]):
        hg = a3[g] * carry + b3[g]
        out.append(hg)
        carry = hg[SUBLANES - 1:SUBLANES]
    return out[0] if len(out) == 1 else jnp.concatenate(out, axis=0)


def _cumsum_rows(x, t, n):
    s = 1
    while s < n:
        x = x + jnp.where(t >= s, pltpu.roll(x, s, 0), 0.0)
        s *= 2
    return x


def _head_stack(q_s, lane_head):
    return jnp.concatenate([jnp.where(lane_head == hd, q_s, 0.0) for hd in range(GLA_HEADS)], axis=0)


def _gla_kv(k_end, v, per_head):
    if per_head:
        return [_dot_tn(k_end[:, hd * GLA_DK:(hd + 1) * GLA_DK], v[:, hd * GLA_DV:(hd + 1) * GLA_DV])
                for hd in range(GLA_HEADS)]
    kv = _dot_tn(k_end, v)
    return [kv[hd * GLA_DK:(hd + 1) * GLA_DK, hd * GLA_DV:(hd + 1) * GLA_DV] for hd in range(GLA_HEADS)]


def _state_decay(el):
    return jnp.broadcast_to(el, (LANES, HK)).T


def _gla_next_state(el_t, s_all, kv):
    return [el_t[hd * GLA_DK:(hd + 1) * GLA_DK] * s_all[hd * GLA_DK:(hd + 1) * GLA_DK] + kv[hd]
            for hd in range(GLA_HEADS)]


def _gla_intra(att_raw, causal, o_inter, v, n):
    att = jnp.where(causal, att_raw, 0.0)
    return [o_inter[hd * n:(hd + 1) * n] + _dot(att[hd * n:(hd + 1) * n], v[:, hd * GLA_DV:(hd + 1) * GLA_DV])
            for hd in range(GLA_HEADS)]


def _gla_chunk(q_s, k_s, k_end, el, v, s_all, causal, lane_head, n):
    stack = _head_stack(q_s, lane_head)
    o_inter = _dot(stack, s_all)
    att_raw = _dot_nt(stack, k_s)
    kv = _gla_kv(k_end, v, per_head=False)
    el_t = _state_decay(el)
    return _gla_intra(att_raw, causal, o_inter, v, n), _gla_next_state(el_t, s_all, kv)


def _gla_output(o, go, gn):
    return _rms(o, gn) * jax.nn.silu(go)


def _mixer_seq_kernel(*refs, tc, cg, nj, reset_first):
    (xin_ref, frame0_ref, h0_ref, s0_ref, gmix_ref, wmain_ref, wlr_ref, wg2_ref, bg_ref,
     convw_ref, convb_ref, wg_ref, ba_ref, bx_ref, lam_ref, gn_ref, wout_ref) = refs[:17]
    xo_ref, hl_ref, so_ref, tailo_ref, u, z, tail_sc, h_sc, s_sc = refs[17:]
    c = pl.program_id(0)

    @pl.when(c % nj == 0)
    def _():
        tail_sc[...] = frame0_ref[...]
        h_sc[...] = h0_ref[...]
        s_sc[...] = s0_ref[...].reshape(HK, GLA_DV)

    x = xin_ref[...]
    hn = _rms(x, gmix_ref[...]).astype(wmain_ref.dtype)
    tile_w = 2 * LANES
    carry = {}
    half = D_LRU // 2

    def in_tile(t):
        cols = slice(t * tile_w, (t + 1) * tile_w)
        u[:, cols] = _dot(hn, wmain_ref[:, cols])

    def gate_tile():
        lr = _dot(hn, wlr_ref[...])
        zg = _dot(lr, wg2_ref[...]) + bg_ref[...]
        u[:, D_MAIN:D_U] = jax.nn.log_sigmoid(zg) / GLA_GATE_NORM

    def out_lru():
        xo_ref[...] = x + _dot(z[:, 0:D_LRU], wout_ref[0:D_LRU, :])

    def out_gla():
        xo_ref[...] += _dot(z[:, D_LRU:D_MODEL], wout_ref[D_LRU:D_MODEL, :])

    def lru_front(hf):
        cs = slice(hf * half, (hf + 1) * half)
        xl = u[:, cs]
        row8 = lax.broadcasted_iota(jnp.int32, (SUBLANES, half), 0)
        xc = convb_ref[:, cs]
        for sft in range(CONV_W - 1, -1, -1):
            if sft == 0:
                sh = xl
            else:
                rolled = pltpu.roll(xl, sft, 0)
                top = jnp.where(row8 < sft, pltpu.roll(tail_sc[:, cs], sft, 0), rolled[0:SUBLANES])
                sh = top if tc == SUBLANES else jnp.concatenate([top, rolled[SUBLANES:]], axis=0)
            xc = xc + convw_ref[CONV_W - 1 - sft:CONV_W - sft, cs] * sh
        carry["tail", hf] = xl[tc - SUBLANES:tc]
        tail_sc[:, cs] = carry["tail", hf]
        carry["xc", hf] = xc

    def lru_gate_mm(hf):
        carry["p", hf] = _dot(carry["xc", hf], wg_ref[hf])

    def lru_mid(hf):
        cs = slice(hf * half, (hf + 1) * half)
        a, mult, ix = _lru_gates(carry["xc", hf], carry["p", hf], ba_ref[:, cs], bx_ref[:, cs],
                                 lam_ref[:, cs])
        if reset_first:
            row = lax.broadcasted_iota(jnp.int32, (tc, half), 0)
            first = jnp.logical_and(row == 0, c % nj == 0)
            mult = jnp.where(first, 1.0, mult)
            a = jnp.where(first, 0.0, a)
        carry["a", hf], carry["b", hf] = a, mult * ix

    def lru_back(hf):
        cs = slice(hf * half, (hf + 1) * half)
        h = _scan_rows(carry["a", hf], carry["b", hf], h_sc[:, cs])
        carry["h", hf] = h[tc - 1:tc]
        h_sc[:, cs] = carry["h", hf]
        z[:, cs] = h * jax.nn.gelu(u[:, O_GL + hf * half:O_GL + (hf + 1) * half])

    lane_head = lax.broadcasted_iota(jnp.int32, (cg, HK), 1) // GLA_DK
    n_sub = tc // cg
    sub = lambda ci: slice(ci * cg, (ci + 1) * cg)

    def gla_prep(ci):
        tq = lax.broadcasted_iota(jnp.int32, (cg, HK), 0)
        b = _cumsum_rows(u[sub(ci), D_MAIN:D_U], tq, cg)
        bl = b[cg - 1:cg]
        k = u[sub(ci), O_K:O_K + HK]
        carry["stack", ci] = _head_stack(u[sub(ci), O_Q:O_Q + HK] * (GLA_DK ** -0.5) * jnp.exp(b),
                                         lane_head)
        carry["ks", ci], carry["ke", ci], carry["el", ci] = k * jnp.exp(-b), k * jnp.exp(bl - b), jnp.exp(bl)

    def gla_free_mm(ci):
        carry["att", ci] = _dot_nt(carry["stack", ci], carry["ks", ci])
        carry["kv", ci] = _gla_kv(carry["ke", ci], u[sub(ci), O_V:O_V + HV], per_head=True)
        carry["el_t", ci] = _state_decay(carry["el", ci])

    def gla_state(ci):
        s_all = carry["S"]
        carry["oi", ci] = _dot(carry["stack", ci], s_all)
        carry["S"] = jnp.concatenate(_gla_next_state(carry["el_t", ci], s_all, carry["kv", ci]), axis=0)

    def gla_out(ci):
        causal = ((lax.broadcasted_iota(jnp.int32, (GLA_HEADS * cg, cg), 0) % cg)
                  >= lax.broadcasted_iota(jnp.int32, (GLA_HEADS * cg, cg), 1))
        o = _gla_intra(carry["att", ci], causal, carry["oi", ci], u[sub(ci), O_V:O_V + HV], cg)
        for hd in range(GLA_HEADS):
            z[sub(ci), D_LRU + hd * GLA_DV:D_LRU + (hd + 1) * GLA_DV] = _gla_output(
                o[hd], u[sub(ci), O_GO + hd * GLA_DV:O_GO + (hd + 1) * GLA_DV], gn_ref[...])

    def gla_begin():
        carry["S"] = s_sc[...]

    def gla_end():
        s_sc[...] = carry["S"]

    lru = {n: functools.partial(f, hf) for hf in range(2)
           for n, f in ((f"front{hf}", lru_front), (f"gmm{hf}", lru_gate_mm), (f"mid{hf}", lru_mid),
                        (f"back{hf}", lru_back))}
    tiles = [functools.partial(in_tile, t) for t in range(D_MAIN // tile_w)]
    assert len(tiles) == 10
    order = (tiles[0:3] + [lru["front0"], tiles[3], lru["front1"], lru["gmm0"], lru["gmm1"],
                           tiles[4], lru["mid0"], tiles[5], lru["back0"], tiles[6], tiles[7],
                           lru["mid1"], tiles[8], lru["back1"], tiles[9], gate_tile]
             + [functools.partial(gla_prep, 0), out_lru]
             + [functools.partial(gla_prep, ci) for ci in range(1, n_sub)]
             + [functools.partial(gla_free_mm, ci) for ci in range(n_sub)] + [gla_begin]
             + [functools.partial(gla_state, ci) for ci in range(n_sub)] + [gla_end]
             + [functools.partial(gla_out, ci) for ci in range(n_sub)] + [out_gla])
    for piece in order:
        piece()

    for hf in range(2):
        cs = slice(hf * half, (hf + 1) * half)
        hl_ref[:, cs] = carry["h", hf]
        tailo_ref[:, cs] = carry["tail", hf]
    so_ref[...] = s_sc[...].reshape(GLA_HEADS, GLA_DK, GLA_DV)


def _mixer_seq_call(x, frame0, h0, s0, mix, l, *, nb, sl, tc, cg, row_block0, reset_first,
                    shared_state, name):
    assert sl % tc == 0 and tc % cg == 0 and tc % SUBLANES == 0
    nj = sl // tc
    xrow = pl.BlockSpec((tc, D_MODEL), lambda s: (row_block0 + s, 0))
    seq = lambda s: s // nj
    if shared_state:
        st = lambda s: (0, 0, 0)
        st4 = lambda s: (0, 0, 0, 0)
    else:
        st = lambda s: (seq(s), 0, 0)
        st4 = lambda s: (seq(s), 0, 0, 0)
    (gmix, w_in, wlr, wg2, bg, convw, convb, wg, ba, bx, lam, gn, wout) = mix
    wmain_spec = pl.BlockSpec((None, D_MODEL, D_MAIN), lambda s: (l, 0, 0), pipeline_mode=pl.Buffered(1))
    args = [x, frame0, h0, s0, gmix, w_in, wlr, wg2, bg, convw, convb, wg, ba, bx, lam, gn, wout]
    in_specs = [
        xrow,
        pl.BlockSpec((None, SUBLANES, D_LRU), st),
        pl.BlockSpec((None, 1, D_LRU), st),
        pl.BlockSpec((None, GLA_HEADS, GLA_DK, GLA_DV), st4),
        _layer_spec(gmix, l), wmain_spec, _layer_spec(wlr, l), _layer_spec(wg2, l), _layer_spec(bg, l),
        _layer_spec(convw, l), _layer_spec(convb, l), _layer_spec(wg, l), _layer_spec(ba, l),
        _layer_spec(bx, l), _layer_spec(lam, l), _layer_spec(gn, l), _layer_spec(wout, l),
    ]
    out_shape = [
        jax.ShapeDtypeStruct((nb * sl, D_MODEL), F32),
        jax.ShapeDtypeStruct((nb, 1, D_LRU), F32),
        jax.ShapeDtypeStruct((nb, GLA_HEADS, GLA_DK, GLA_DV), F32),
        jax.ShapeDtypeStruct((nb, SUBLANES, D_LRU), F32),
    ]
    out_specs = [
        pl.BlockSpec((tc, D_MODEL), lambda s: (s, 0)),
        pl.BlockSpec((None, 1, D_LRU), lambda s: (seq(s), 0, 0)),
        pl.BlockSpec((None, GLA_HEADS, GLA_DK, GLA_DV), lambda s: (seq(s), 0, 0, 0)),
        pl.BlockSpec((None, SUBLANES, D_LRU), lambda s: (seq(s), 0, 0)),
    ]
    scratch = [
        pltpu.VMEM((tc, D_U), F32),
        pltpu.VMEM((tc, D_MODEL), F32),
        pltpu.VMEM((SUBLANES, D_LRU), F32),
        pltpu.VMEM((1, D_LRU), F32),
        pltpu.VMEM((HK, GLA_DV), F32),
    ]
    kern = functools.partial(_mixer_seq_kernel, tc=tc, cg=cg, nj=nj, reset_first=reset_first)
    return pl.pallas_call(
        kern, grid=(nb * nj,), in_specs=in_specs, out_specs=out_specs, out_shape=out_shape,
        scratch_shapes=scratch,
        compiler_params=pltpu.CompilerParams(dimension_semantics=("arbitrary",),
                                             vmem_limit_bytes=VMEM_LIMIT),
        name=name,
    )(*args)


def _mixer_dec_kernel(*refs, nb, n_prev):
    (x_ref, frame_ref, h0_ref, s0_ref, gmix_ref, wmain_ref, wlr_ref, wg2_ref, bg_ref,
     convw_ref, convb_ref, wg_ref, ba_ref, bx_ref, lam_ref, gn_ref, wout_ref) = refs[:17]
    prev_refs = refs[17:17 + n_prev]
    (xo_ref, h_ref, xl_ref, so_ref,
     u_sc, z_sc, qs_sc, ks_sc, ke_sc, el_sc, o_sc) = refs[17 + n_prev:]
    for j, prev in enumerate(prev_refs):
        so_ref[j] = prev[...]
    so_new = so_ref.at[n_prev] if n_prev else so_ref
    ls = SUBLANES
    rows = nb * ls
    x = x_ref[...]
    _in_projection(x, gmix_ref, wmain_ref, wlr_ref, wg2_ref, bg_ref, u_sc)

    t = lax.broadcasted_iota(jnp.int32, (rows, D_LRU), 0) % ls
    xl = u_sc[:, 0:D_LRU]
    xl_ref[...] = xl
    frame = frame_ref[...]
    xc = convb_ref[...]
    for s in range(CONV_W - 1, -1, -1):
        if s == 0:
            sh = xl
        else:
            sh = jnp.where(t >= s, pltpu.roll(xl, s, 0), pltpu.roll(frame, (rows - ls + s) % rows, 0))
        xc = xc + convw_ref[CONV_W - 1 - s:CONV_W - s, :] * sh
    half = D_LRU // 2
    parts = []
    for hf in range(2):
        cs = slice(hf * half, (hf + 1) * half)
        parts.append(_lru_gates(xc[:, cs], _dot(xc[:, cs], wg_ref[hf]), ba_ref[:, cs], bx_ref[:, cs],
                                lam_ref[:, cs]))
    a, mult, ix = (jnp.concatenate([p[n] for p in parts], axis=1) for n in range(3))
    a3, b3 = _scan_groups(a, mult * ix)
    h = (a3 * h0_ref[...].reshape(a3.shape) + b3).reshape(rows, D_LRU)
    h_ref[...] = h
    z_sc[:, 0:D_LRU] = h * jax.nn.gelu(u_sc[:, O_GL:O_GL + D_LRU])

    tq = lax.broadcasted_iota(jnp.int32, (rows, HK), 0) % ls
    b = _cumsum_rows(u_sc[:, D_MAIN:D_U], tq, ls)
    bl = jnp.where(tq == ls - 1, b, 0.0)
    s = 1
    while s < ls:
        bl = bl + jnp.where(tq + s < ls, pltpu.roll(bl, rows - s, 0), 0.0)
        s *= 2
    k = u_sc[:, O_K:O_K + HK]
    qs_sc[...] = (u_sc[:, O_Q:O_Q + HK] * (GLA_DK ** -0.5)) * jnp.exp(b)
    ks_sc[...] = k * jnp.exp(-b)
    ke_sc[...] = k * jnp.exp(bl - b)
    el_sc[...] = jnp.exp(bl)

    lane_head = lax.broadcasted_iota(jnp.int32, (ls, HK), 1) // GLA_DK
    causal = ((lax.broadcasted_iota(jnp.int32, (GLA_HEADS * ls, ls), 0) % ls)
              >= lax.broadcasted_iota(jnp.int32, (GLA_HEADS * ls, ls), 1))

    def body(bi, carry):
        rs = pl.ds(pl.multiple_of(bi * ls, ls), ls)
        s_all = s0_ref[bi].reshape(HK, GLA_DV)
        o, s_new = _gla_chunk(qs_sc[rs, :], ks_sc[rs, :], ke_sc[rs, :], el_sc[rs, :][0:1],
                              u_sc[rs, O_V:O_V + HV], s_all, causal, lane_head, ls)
        for hd in range(GLA_HEADS):
            so_new[bi, hd] = s_new[hd]
            o_sc[rs, hd * GLA_DV:(hd + 1) * GLA_DV] = o[hd]
        return carry

    lax.fori_loop(0, nb, body, 0, unroll=4)

    gn = gn_ref[...]
    for hd in range(GLA_HEADS):
        vs = slice(hd * GLA_DV, (hd + 1) * GLA_DV)
        z_sc[:, D_LRU + hd * GLA_DV:D_LRU + (hd + 1) * GLA_DV] = _gla_output(
            o_sc[:, vs], u_sc[:, O_GO + hd * GLA_DV:O_GO + (hd + 1) * GLA_DV], gn)
    xo_ref[...] = x + _dot(z_sc[...], wout_ref[...])


def _mixer_dec_call(x, frame, h0, s0, s_prev, mix, l, *, nseq, nb, name):
    ls = SUBLANES
    assert nseq % nb == 0
    rows = nb * ls
    (gmix, w_in, wlr, wg2, bg, convw, convb, wg, ba, bx, lam, gn, wout) = mix
    rowspec = lambda w: pl.BlockSpec((rows, w), lambda i: (i, 0))
    sspec = pl.BlockSpec((None, nb, GLA_HEADS, GLA_DK, GLA_DV), lambda i: (l, i, 0, 0, 0))
    wmain_spec = pl.BlockSpec((None, D_MODEL, D_MAIN), lambda i: (l, 0, 0), pipeline_mode=pl.Buffered(1))
    in_specs = [
        rowspec(D_MODEL), rowspec(D_LRU), rowspec(D_LRU), sspec,
        _layer_spec(gmix, l), wmain_spec, _layer_spec(wlr, l), _layer_spec(wg2, l), _layer_spec(bg, l),
        _layer_spec(convw, l), _layer_spec(convb, l), _layer_spec(wg, l), _layer_spec(ba, l),
        _layer_spec(bx, l), _layer_spec(lam, l), _layer_spec(gn, l), _layer_spec(wout, l),
    ]
    out_shape = [
        jax.ShapeDtypeStruct((nseq * ls, D_MODEL), F32),
        jax.ShapeDtypeStruct((nseq * ls, D_LRU), F32),
        jax.ShapeDtypeStruct((nseq * ls, D_LRU), F32),
    ]
    state = (nseq, GLA_HEADS, GLA_DK, GLA_DV)
    sblock = pl.BlockSpec((nb,) + state[1:], lambda i: (i, 0, 0, 0))
    n_prev = len(s_prev)
    if n_prev:
        out_shape.append(jax.ShapeDtypeStruct((n_prev + 1,) + state, F32))
        s_out_spec = pl.BlockSpec((n_prev + 1, nb) + state[1:], lambda i: (0, i, 0, 0, 0))
    else:
        out_shape.append(jax.ShapeDtypeStruct(state, F32))
        s_out_spec = sblock
    out_specs = [rowspec(D_MODEL), rowspec(D_LRU), rowspec(D_LRU), s_out_spec]
    args = [x, frame, h0, s0, gmix, w_in, wlr, wg2, bg, convw, convb, wg, ba, bx, lam, gn, wout]
    args += list(s_prev)
    in_specs += [sblock] * n_prev
    scratch = [
        pltpu.VMEM((rows, D_U), F32), pltpu.VMEM((rows, D_MODEL), F32),
        pltpu.VMEM((rows, HK), F32), pltpu.VMEM((rows, HK), F32), pltpu.VMEM((rows, HK), F32),
        pltpu.VMEM((rows, HK), F32), pltpu.VMEM((rows, HV), F32),
    ]
    return pl.pallas_call(
        functools.partial(_mixer_dec_kernel, nb=nb, n_prev=n_prev),
        grid=(nseq // nb,), in_specs=in_specs, out_specs=out_specs, out_shape=out_shape,
        scratch_shapes=scratch,
        compiler_params=pltpu.CompilerParams(dimension_semantics=("arbitrary",),
                                             vmem_limit_bytes=VMEM_LIMIT),
        name=name,
    )(*args)


def _block_diag_gates(wa, wx):
    per = LRU_BLOCKS // 2
    bw = wa.shape[-1]
    eye = jnp.eye(per, dtype=wa.dtype)

    def bd(w):
        return jnp.einsum("lncd,nm->lncmd", w, eye).reshape(w.shape[0], per * bw, per * bw)

    halves = [jnp.concatenate([bd(wa[:, hf * per:(hf + 1) * per]), bd(wx[:, hf * per:(hf + 1) * per])],
                              axis=-1) for hf in range(2)]
    return jnp.stack(halves, axis=1)


def kernel(x_prompt, x_sample, state_lru_h, state_lru_conv, state_gla_S, meta, norm_ffn1, w_ffn1_gu,
           w_ffn1_down, norm_mix, w_in, lru_conv_w, lru_conv_b, lru_wa, lru_ba, lru_wx, lru_bx,
           lru_lambda, gla_w_gate2, gla_b_gate, gla_norm, w_out, norm_ffn2, w_ffn2_gu, w_ffn2_down,
           norm_final):
    bp, lp, d = x_prompt.shape
    bs, ls, _ = x_sample.shape
    n_meta = meta.shape[0]
    depth = w_in.shape[0]
    assert d == D_MODEL and ls == SUBLANES and n_meta % SUBLANES == 0
    tp, ts = bp * lp, bs * ls
    tsm = ts + n_meta
    assert ts % n_meta == 0
    tm_p = 512 if tp % 512 == 0 else lp
    tm_s = tsm // 2 if (tsm // 2) % SUBLANES == 0 else tsm
    tc_p = 256 if lp % 256 == 0 else GLA_CHUNK
    nb_s = 32 if bs % 32 == 0 else bs
    tail = CONV_W - 1

    vec = lambda a: a.reshape(a.shape[0], 1, a.shape[-1])
    mix = (vec(norm_mix), w_in.astype(BF16),
           jnp.pad(w_in[:, :, D_MAIN:], ((0, 0), (0, 0), (0, LANES - GLA_RANK))).astype(BF16),
           jnp.pad(gla_w_gate2, ((0, 0), (0, LANES - GLA_RANK), (0, 0))).astype(BF16),
           vec(gla_b_gate), lru_conv_w, vec(lru_conv_b), _block_diag_gates(lru_wa, lru_wx).astype(BF16),
           vec(lru_ba), vec(lru_bx), vec(lru_lambda), vec(gla_norm), w_out.astype(BF16))
    n1, n2, nfin = vec(norm_ffn1), vec(norm_ffn2), norm_final.reshape(1, d)
    frames_s = jnp.pad(state_lru_conv, ((0, 0), (0, 0), (SUBLANES - tail, 0), (0, 0)))
    frames_s = frames_s.reshape(depth, ts, D_LRU)
    h0_s = jnp.broadcast_to(state_lru_h[:, :, None, :], (depth, bs, ls, D_LRU)).reshape(depth, ts, D_LRU)
    zero_frame = jnp.zeros((1, SUBLANES, D_LRU), F32)
    zero_h = jnp.zeros((1, 1, D_LRU), F32)
    zero_s = jnp.zeros((1, GLA_HEADS, GLA_DK, GLA_DV), F32)

    xp = x_prompt.reshape(tp, d)
    xs = jnp.concatenate([x_sample.reshape(ts, d), meta.astype(F32)], axis=0)

    hs_p, convs_p, ss_p, hs_s, convs_s, ss_s = [], [], [], [], [], []
    for l in range(depth):
        xp1, xs1 = _ffn_call(xp, xs, n1, w_ffn1_gu, w_ffn1_down, l, tm_a=tm_p, tm_b=tm_s, tf=256,
                             name=f"ffn1_{l}")

        last = l == depth - 1
        xs2, h_s, xl_s, s_s = _mixer_dec_call(xs1, frames_s[l], h0_s[l], state_gla_S,
                                              ss_s if last else (), mix, l, nseq=bs, nb=nb_s,
                                              name=f"mix_s{l}")
        ss_s.append(s_s)
        hs_s.append(h_s.reshape(bs, ls, D_LRU)[:, ls - 1])
        convs_s.append(xl_s.reshape(bs, ls, D_LRU)[:, ls - tail:])

        xm2, h_m, s_m, tail_m = _mixer_seq_call(
            xs1, zero_frame, zero_h, zero_s, mix, l, nb=1, sl=n_meta, tc=n_meta, cg=n_meta,
            row_block0=ts // n_meta, reset_first=True, shared_state=False, name=f"mix_m{l}")
        xs2 = jnp.concatenate([xs2, xm2], axis=0)

        xp2, h_p, s_p, tail_p = _mixer_seq_call(
            xp1, tail_m, h_m, s_m, mix, l, nb=bp, sl=lp, tc=tc_p, cg=GLA_CHUNK, row_block0=0,
            reset_first=False, shared_state=True, name=f"mix_p{l}")
        hs_p.append(h_p[:, 0])
        convs_p.append(tail_p[:, SUBLANES - tail:])
        ss_p.append(s_p)

        fin = nfin if last else None
        xp, xs = _ffn_call(xp2, xs2, n2, w_ffn2_gu, w_ffn2_down, l, final=fin, tm_a=tm_p, tm_b=tm_s,
                           tf=256, name=f"ffn2_{l}")

    s_stack = ss_s[-1] if depth > 1 else ss_s[0][None]
    return (xp.reshape(bp, lp, d), xs[:ts].reshape(bs, ls, d),
            jnp.stack(hs_p), jnp.stack(convs_p), jnp.stack(ss_p),
            jnp.stack(hs_s), jnp.stack(convs_s), s_stack)
```

```python
import functools

import jax
import jax.numpy as jnp
from jax import lax
from jax.experimental import pallas as pl
from jax.experimental.pallas import tpu as pltpu

F32 = jnp.float32
BF16 = jnp.bfloat16

D_MODEL = 1024
D_FF = 2816
D_LRU = 512
LRU_BLOCKS = 8
CONV_W = 4
LRU_C = 8.0
GLA_HEADS = 4
GLA_DV = 128
GLA_DK = 64
GLA_RANK = 16
GLA_GATE_NORM = 16.0
GLA_CHUNK = 64
EPS = 1e-6
HK = GLA_HEADS * GLA_DK
HV = GLA_HEADS * GLA_DV
O_GL, O_Q, O_K, O_V, O_GO = D_LRU, 2 * D_LRU, 2 * D_LRU + HK, 2 * D_LRU + 2 * HK, 2 * D_LRU + 2 * HK + HV
D_MAIN = O_GO + HV
D_U = D_MAIN + HK
SUBLANES = 8
LANES = 128
VMEM_LIMIT = 58 * 1024 * 1024


def _dot(a, b):
    if a.dtype != b.dtype:
        a = a.astype(b.dtype)
    return jnp.dot(a, b, preferred_element_type=F32)


def _dot_nt(a, b):
    return lax.dot_general(a, b, (((1,), (1,)), ((), ())), preferred_element_type=F32)


def _dot_tn(a, b):
    return lax.dot_general(a, b, (((0,), (0,)), ((), ())), preferred_element_type=F32)


def _rms(x, g):
    ms = jnp.mean(x * x, axis=-1, keepdims=True)
    return (x * lax.rsqrt(ms + EPS)) * g


def _sigmoid(x):
    return 0.5 * jnp.tanh(0.5 * x) + 0.5


def _silu(x):
    return x * _sigmoid(x)


def _log_sigmoid(x):
    return jnp.minimum(x, 0.0) - jnp.log(1.0 + jnp.exp(-jnp.abs(x)))


def _layer_spec(a, l):
    nd = a.ndim
    return pl.BlockSpec((None,) + a.shape[1:], lambda *_: (l,) + (0,) * (nd - 1),
                        pipeline_mode=pl.Buffered(1))


def _whole_spec(a):
    nd = a.ndim
    return pl.BlockSpec(a.shape, lambda *_: (0,) * nd, pipeline_mode=pl.Buffered(1))


def _ffn_kernel(*refs, tf, has_final, n_a):
    if has_final:
        xa_ref, xb_ref, g_ref, wgu_ref, wd_ref, gfin_ref, oa_ref, ob_ref = refs
    else:
        xa_ref, xb_ref, g_ref, wgu_ref, wd_ref, oa_ref, ob_ref = refs

    def tile(x_ref, xo_ref):
        x = x_ref[...]
        xn = _rms(x, g_ref[...])
        acc = None
        for c in range(D_FF // tf):
            gate = _dot(xn, wgu_ref[:, c * tf:(c + 1) * tf])
            up = _dot(xn, wgu_ref[:, D_FF + c * tf:D_FF + (c + 1) * tf])
            d = _dot(_silu(gate) * up, wd_ref[c * tf:(c + 1) * tf, :])
            acc = d if acc is None else acc + d
        x = x + 0.5 * acc
        xo_ref[...] = _rms(x, gfin_ref[...]) if has_final else x

    i = pl.program_id(0)

    @pl.when(i < n_a)
    def _():
        tile(xa_ref, oa_ref)

    @pl.when(i >= n_a)
    def _():
        tile(xb_ref, ob_ref)


def _ffn_call(xa, xb, norm, wgu, wd, l, *, final=None, tm_a, tm_b, tf, name):
    (ta, d), (tb, _) = xa.shape, xb.shape
    assert ta % tm_a == 0 and tb % tm_b == 0 and D_FF % tf == 0
    n_a, n_b = ta // tm_a, tb // tm_b
    rows_a = pl.BlockSpec((tm_a, d), lambda i: (jnp.minimum(i, n_a - 1), 0))
    rows_b = pl.BlockSpec((tm_b, d), lambda i: (jnp.maximum(i - n_a, 0), 0))
    args = [xa, xb, norm, wgu, wd]
    specs = [rows_a, rows_b, _layer_spec(norm, l), _layer_spec(wgu, l), _layer_spec(wd, l)]
    if final is not None:
        args.append(final)
        specs.append(_whole_spec(final))
    return pl.pallas_call(
        functools.partial(_ffn_kernel, tf=tf, has_final=final is not None, n_a=n_a),
        grid=(n_a + n_b,), in_specs=specs, out_specs=[rows_a, rows_b],
        out_shape=[jax.ShapeDtypeStruct((ta, d), F32), jax.ShapeDtypeStruct((tb, d), F32)],
        compiler_params=pltpu.CompilerParams(dimension_semantics=("arbitrary",),
                                             vmem_limit_bytes=VMEM_LIMIT),
        name=name,
    )(*args)


def _in_projection(x, gmix_ref, wmain_ref, wlr_ref, wg2_ref, bg_ref, u_sc):
    hn = _rms(x, gmix_ref[...]).astype(wmain_ref.dtype)
    u_sc[:, 0:D_MAIN] = _dot(hn, wmain_ref[...])
    lr = _dot(hn, wlr_ref[...])
    zg = _dot(lr, wg2_ref[...]) + bg_ref[...]
    u_sc[:, D_MAIN:D_U] = _log_sigmoid(zg) / GLA_GATE_NORM


def _lru_gates(xc, p, ba, bx, lam):
    half = xc.shape[1]
    r = _sigmoid(p[:, 0:half] + ba)
    i = _sigmoid(p[:, half:] + bx)
    log_a = -LRU_C * r * jax.nn.softplus(-lam)
    a = jnp.exp(log_a)
    y = -jnp.tanh(log_a) * (a * a + 1.0)
    mult = jnp.where(y > 0.0, y * lax.rsqrt(y), 0.0)
    return a, mult, i * xc


def _scan_groups(a, b):
    rows, cols = a.shape
    a3 = a.reshape(rows // SUBLANES, SUBLANES, cols)
    b3 = b.reshape(rows // SUBLANES, SUBLANES, cols)
    t = lax.broadcasted_iota(jnp.int32, a3.shape, 1)
    s = 1
    while s < SUBLANES:
        keep = t >= s
        b3 = jnp.where(keep, a3 * pltpu.roll(b3, s, 1) + b3, b3)
        a3 = jnp.where(keep, a3 * pltpu.roll(a3, s, 1), a3)
        s *= 2
    return a3, b3


def _scan_rows(a, b, h0):
    a3, b3 = _scan_groups(a, b)
    carry, out = h0, []
    for g in range(a3.shape[0]):
        hg = a3[g] * carry + b3[g]
        out.append(hg)
        carry = hg[SUBLANES - 1:SUBLANES]
    return out[0] if len(out) == 1 else jnp.concatenate(out, axis=0)


def _cumsum_rows(x, t, n):
    s = 1
    while s < n:
        x = x + jnp.where(t >= s, pltpu.roll(x, s, 0), 0.0)
        s *= 2
    return x


def _head_stack(q_s, lane_head):
    return jnp.concatenate([jnp.where(lane_head == hd, q_s, 0.0) for hd in range(GLA_HEADS)], axis=0)


def _gla_kv(k_end, v, per_head):
    if per_head:
        return [_dot_tn(k_end[:, hd * GLA_DK:(hd + 1) * GLA_DK], v[:, hd * GLA_DV:(hd + 1) * GLA_DV])
                for hd in range(GLA_HEADS)]
    kv = _dot_tn(k_end, v)
    return [kv[hd * GLA_DK:(hd + 1) * GLA_DK, hd * GLA_DV:(hd + 1) * GLA_DV] for hd in range(GLA_HEADS)]


def _state_decay(el):
    return jnp.broadcast_to(el, (LANES, HK)).T


def _gla_next_state(el_t, s_all, kv):
    return [el_t[hd * GLA_DK:(hd + 1) * GLA_DK] * s_all[hd * GLA_DK:(hd + 1) * GLA_DK] + kv[hd]
            for hd in range(GLA_HEADS)]


def _gla_intra(att_raw, causal, o_inter, v, n):
    att = jnp.where(causal, att_raw, 0.0)
    return [o_inter[hd * n:(hd + 1) * n] + _dot(att[hd * n:(hd + 1) * n], v[:, hd * GLA_DV:(hd + 1) * GLA_DV])
            for hd in range(GLA_HEADS)]


def _gla_chunk(q_s, k_s, k_end, el, v, s_all, causal, lane_head, n):
    stack = _head_stack(q_s, lane_head)
    o_inter = _dot(stack, s_all)
    att_raw = _dot_nt(stack, k_s)
    kv = _gla_kv(k_end, v, per_head=False)
    el_t = _state_decay(el)
    return _gla_intra(att_raw, causal, o_inter, v, n), _gla_next_state(el_t, s_all, kv)


def _gla_output(o, go, gn):
    return _rms(o, gn) * _silu(go)


def _mixer_seq_kernel(*refs, tc, cg, nj, reset_first):
    (xin_ref, frame0_ref, h0_ref, s0_ref, gmix_ref, wmain_ref, wlr_ref, wg2_ref, bg_ref,
     convw_ref, convb_ref, wg_ref, ba_ref, bx_ref, lam_ref, gn_ref, wout_ref) = refs[:17]
    xo_ref, hl_ref, so_ref, tailo_ref, u, z, tail_sc, h_sc, s_sc = refs[17:]
    c = pl.program_id(0)

    @pl.when(c % nj == 0)
    def _():
        tail_sc[...] = frame0_ref[...]
        h_sc[...] = h0_ref[...]
        s_sc[...] = s0_ref[...].reshape(HK, GLA_DV)

    x = xin_ref[...]
    hn = _rms(x, gmix_ref[...]).astype(wmain_ref.dtype)
    tile_w = 2 * LANES
    carry = {}
    half = D_LRU // 2

    def in_tile(t):
        cols = slice(t * tile_w, (t + 1) * tile_w)
        u[:, cols] = _dot(hn, wmain_ref[:, cols])

    def gate_tile():
        lr = _dot(hn, wlr_ref[...])
        zg = _dot(lr, wg2_ref[...]) + bg_ref[...]
        u[:, D_MAIN:D_U] = _log_sigmoid(zg) / GLA_GATE_NORM

    def out_lru():
        xo_ref[...] = x + _dot(z[:, 0:D_LRU], wout_ref[0:D_LRU, :])

    def out_gla():
        xo_ref[...] += _dot(z[:, D_LRU:D_MODEL], wout_ref[D_LRU:D_MODEL, :])

    def lru_front(hf):
        cs = slice(hf * half, (hf + 1) * half)
        xl = u[:, cs]
        row8 = lax.broadcasted_iota(jnp.int32, (SUBLANES, half), 0)
        xc = convb_ref[:, cs]
        for sft in range(CONV_W - 1, -1, -1):
            if sft == 0:
                sh = xl
            else:
                rolled = pltpu.roll(xl, sft, 0)
                top = jnp.where(row8 < sft, pltpu.roll(tail_sc[:, cs], sft, 0), rolled[0:SUBLANES])
                sh = top if tc == SUBLANES else jnp.concatenate([top, rolled[SUBLANES:]], axis=0)
            xc = xc + convw_ref[CONV_W - 1 - sft:CONV_W - sft, cs] * sh
        carry["tail", hf] = xl[tc - SUBLANES:tc]
        tail_sc[:, cs] = carry["tail", hf]
        carry["xc", hf] = xc

    def lru_gate_mm(hf):
        carry["p", hf] = _dot(carry["xc", hf], wg_ref[hf])

    def lru_mid(hf):
        cs = slice(hf * half, (hf + 1) * half)
        a, mult, ix = _lru_gates(carry["xc", hf], carry["p", hf], ba_ref[:, cs], bx_ref[:, cs],
                                 lam_ref[:, cs])
        if reset_first:
            row = lax.broadcasted_iota(jnp.int32, (tc, half), 0)
            first = jnp.logical_and(row == 0, c % nj == 0)
            mult = jnp.where(first, 1.0, mult)
            a = jnp.where(first, 0.0, a)
        carry["a", hf], carry["b", hf] = a, mult * ix

    def lru_back(hf):
        cs = slice(hf * half, (hf + 1) * half)
        h = _scan_rows(carry["a", hf], carry["b", hf], h_sc[:, cs])
        carry["h", hf] = h[tc - 1:tc]
        h_sc[:, cs] = carry["h", hf]
        z[:, cs] = h * jax.nn.gelu(u[:, O_GL + hf * half:O_GL + (hf + 1) * half])

    lane_head = lax.broadcasted_iota(jnp.int32, (cg, HK), 1) // GLA_DK
    n_sub = tc // cg
    sub = lambda ci: slice(ci * cg, (ci + 1) * cg)

    def gla_prep(ci):
        tq = lax.broadcasted_iota(jnp.int32, (cg, HK), 0)
        b = _cumsum_rows(u[sub(ci), D_MAIN:D_U], tq, cg)
        bl = b[cg - 1:cg]
        k = u[sub(ci), O_K:O_K + HK]
        carry["stack", ci] = _head_stack(u[sub(ci), O_Q:O_Q + HK] * (GLA_DK ** -0.5) * jnp.exp(b),
                                         lane_head)
        carry["ks", ci], carry["ke", ci], carry["el", ci] = k * jnp.exp(-b), k * jnp.exp(bl - b), jnp.exp(bl)

    def gla_free_mm(ci):
        carry["att", ci] = _dot_nt(carry["stack", ci], carry["ks", ci])
        carry["kv", ci] = _gla_kv(carry["ke", ci], u[sub(ci), O_V:O_V + HV], per_head=True)
        carry["el_t", ci] = _state_decay(carry["el", ci])

    def gla_state(ci):
        s_all = carry["S"]
        carry["oi", ci] = _dot(carry["stack", ci], s_all)
        carry["S"] = jnp.concatenate(_gla_next_state(carry["el_t", ci], s_all, carry["kv", ci]), axis=0)

    def gla_out(ci):
        causal = ((lax.broadcasted_iota(jnp.int32, (GLA_HEADS * cg, cg), 0) % cg)
                  >= lax.broadcasted_iota(jnp.int32, (GLA_HEADS * cg, cg), 1))
        o = _gla_intra(carry["att", ci], causal, carry["oi", ci], u[sub(ci), O_V:O_V + HV], cg)
        for hd in range(GLA_HEADS):
            z[sub(ci), D_LRU + hd * GLA_DV:D_LRU + (hd + 1) * GLA_DV] = _gla_output(
                o[hd], u[sub(ci), O_GO + hd * GLA_DV:O_GO + (hd + 1) * GLA_DV], gn_ref[...])

    def gla_begin():
        carry["S"] = s_sc[...]

    def gla_end():
        s_sc[...] = carry["S"]

    lru = {n: functools.partial(f, hf) for hf in range(2)
           for n, f in ((f"front{hf}", lru_front), (f"gmm{hf}", lru_gate_mm), (f"mid{hf}", lru_mid),
                        (f"back{hf}", lru_back))}
    tiles = [functools.partial(in_tile, t) for t in range(D_MAIN // tile_w)]
    assert len(tiles) == 10
    prep = [functools.partial(gla_prep, ci) for ci in range(n_sub)]
    free = [functools.partial(gla_free_mm, ci) for ci in range(n_sub)]
    head = (tiles[0:3] + [lru["front0"], tiles[3], lru["front1"], lru["gmm0"], lru["gmm1"],
                          tiles[4], lru["mid0"], tiles[5], gate_tile, lru["back0"], tiles[6], lru["mid1"],
                          tiles[7], lru["back1"], tiles[8]])
    mid = [prep[0], tiles[9]] + ([prep[1], out_lru] if n_sub > 1 else [out_lru])
    for ci in range(2, n_sub):
        mid += [prep[ci], free[ci - 2]]
    mid += free[max(n_sub - 2, 0):]
    order = (head + mid + [gla_begin] + [functools.partial(gla_state, ci) for ci in range(n_sub)]
             + [gla_end] + [functools.partial(gla_out, ci) for ci in range(n_sub)] + [out_gla])
    for piece in order:
        piece()

    for hf in range(2):
        cs = slice(hf * half, (hf + 1) * half)
        hl_ref[:, cs] = carry["h", hf]
        tailo_ref[:, cs] = carry["tail", hf]
    so_ref[...] = s_sc[...].reshape(GLA_HEADS, GLA_DK, GLA_DV)


def _mixer_seq_call(x, frame0, h0, s0, mix, l, *, nb, sl, tc, cg, row_block0, reset_first,
                    shared_state, name):
    assert sl % tc == 0 and tc % cg == 0 and tc % SUBLANES == 0
    nj = sl // tc
    xrow = pl.BlockSpec((tc, D_MODEL), lambda s: (row_block0 + s, 0))
    seq = lambda s: s // nj
    if shared_state:
        st = lambda s: (0, 0, 0)
        st4 = lambda s: (0, 0, 0, 0)
    else:
        st = lambda s: (seq(s), 0, 0)
        st4 = lambda s: (seq(s), 0, 0, 0)
    (gmix, w_in, wlr, wg2, bg, convw, convb, wg, ba, bx, lam, gn, wout) = mix
    wmain_spec = pl.BlockSpec((None, D_MODEL, D_MAIN), lambda s: (l, 0, 0), pipeline_mode=pl.Buffered(1))
    args = [x, frame0, h0, s0, gmix, w_in, wlr, wg2, bg, convw, convb, wg, ba, bx, lam, gn, wout]
    in_specs = [
        xrow,
        pl.BlockSpec((None, SUBLANES, D_LRU), st),
        pl.BlockSpec((None, 1, D_LRU), st),
        pl.BlockSpec((None, GLA_HEADS, GLA_DK, GLA_DV), st4),
        _layer_spec(gmix, l), wmain_spec, _layer_spec(wlr, l), _layer_spec(wg2, l), _layer_spec(bg, l),
        _layer_spec(convw, l), _layer_spec(convb, l), _layer_spec(wg, l), _layer_spec(ba, l),
        _layer_spec(bx, l), _layer_spec(lam, l), _layer_spec(gn, l), _layer_spec(wout, l),
    ]
    out_shape = [
        jax.ShapeDtypeStruct((nb * sl, D_MODEL), F32),
        jax.ShapeDtypeStruct((nb, 1, D_LRU), F32),
        jax.ShapeDtypeStruct((nb, GLA_HEADS, GLA_DK, GLA_DV), F32),
        jax.ShapeDtypeStruct((nb, SUBLANES, D_LRU), F32),
    ]
    out_specs = [
        pl.BlockSpec((tc, D_MODEL), lambda s: (s, 0)),
        pl.BlockSpec((None, 1, D_LRU), lambda s: (seq(s), 0, 0)),
        pl.BlockSpec((None, GLA_HEADS, GLA_DK, GLA_DV), lambda s: (seq(s), 0, 0, 0)),
        pl.BlockSpec((None, SUBLANES, D_LRU), lambda s: (seq(s), 0, 0)),
    ]
    scratch = [
        pltpu.VMEM((tc, D_U), F32),
        pltpu.VMEM((tc, D_MODEL), F32),
        pltpu.VMEM((SUBLANES, D_LRU), F32),
        pltpu.VMEM((1, D_LRU), F32),
        pltpu.VMEM((HK, GLA_DV), F32),
    ]
    kern = functools.partial(_mixer_seq_kernel, tc=tc, cg=cg, nj=nj, reset_first=reset_first)
    return pl.pallas_call(
        kern, grid=(nb * nj,), in_specs=in_specs, out_specs=out_specs, out_shape=out_shape,
        scratch_shapes=scratch,
        compiler_params=pltpu.CompilerParams(dimension_semantics=("arbitrary",),
                                             vmem_limit_bytes=VMEM_LIMIT),
        name=name,
    )(*args)


def _mixer_dec_kernel(*refs, nb, n_prev):
    (x_ref, frame_ref, h0_ref, s0_ref, gmix_ref, wmain_ref, wlr_ref, wg2_ref, bg_ref,
     convw_ref, convb_ref, wg_ref, ba_ref, bx_ref, lam_ref, gn_ref, wout_ref) = refs[:17]
    prev_refs = refs[17:17 + n_prev]
    (xo_ref, h_ref, xl_ref, so_ref,
     u_sc, z_sc, qs_sc, ks_sc, ke_sc, el_sc, o_sc) = refs[17 + n_prev:]
    for j, prev in enumerate(prev_refs):
        so_ref[j] = prev[...]
    so_new = so_ref.at[n_prev] if n_prev else so_ref
    ls = SUBLANES
    rows = nb * ls
    x = x_ref[...]
    _in_projection(x, gmix_ref, wmain_ref, wlr_ref, wg2_ref, bg_ref, u_sc)

    t = lax.broadcasted_iota(jnp.int32, (rows, D_LRU), 0) % ls
    xl = u_sc[:, 0:D_LRU]
    xl_ref[...] = xl
    frame = frame_ref[...]
    xc = convb_ref[...]
    for s in range(CONV_W - 1, -1, -1):
        if s == 0:
            sh = xl
        else:
            sh = jnp.where(t >= s, pltpu.roll(xl, s, 0), pltpu.roll(frame, (rows - ls + s) % rows, 0))
        xc = xc + convw_ref[CONV_W - 1 - s:CONV_W - s, :] * sh
    half = D_LRU // 2
    parts = []
    for hf in range(2):
        cs = slice(hf * half, (hf + 1) * half)
        parts.append(_lru_gates(xc[:, cs], _dot(xc[:, cs], wg_ref[hf]), ba_ref[:, cs], bx_ref[:, cs],
                                lam_ref[:, cs]))
    a, mult, ix = (jnp.concatenate([p[n] for p in parts], axis=1) for n in range(3))
    a3, b3 = _scan_groups(a, mult * ix)
    h = (a3 * h0_ref[...].reshape(a3.shape) + b3).reshape(rows, D_LRU)
    h_ref[...] = h
    z_sc[:, 0:D_LRU] = h * jax.nn.gelu(u_sc[:, O_GL:O_GL + D_LRU])

    tq = lax.broadcasted_iota(jnp.int32, (rows, HK), 0) % ls
    b = _cumsum_rows(u_sc[:, D_MAIN:D_U], tq, ls)
    bl = jnp.where(tq == ls - 1, b, 0.0)
    s = 1
    while s < ls:
        bl = bl + jnp.where(tq + s < ls, pltpu.roll(bl, rows - s, 0), 0.0)
        s *= 2
    k = u_sc[:, O_K:O_K + HK]
    qs_sc[...] = (u_sc[:, O_Q:O_Q + HK] * (GLA_DK ** -0.5)) * jnp.exp(b)
    ks_sc[...] = k * jnp.exp(-b)
    ke_sc[...] = k * jnp.exp(bl - b)
    el_sc[...] = jnp.exp(bl)

    lane_head = lax.broadcasted_iota(jnp.int32, (ls, HK), 1) // GLA_DK
    causal = ((lax.broadcasted_iota(jnp.int32, (GLA_HEADS * ls, ls), 0) % ls)
              >= lax.broadcasted_iota(jnp.int32, (GLA_HEADS * ls, ls), 1))

    def body(bi, carry):
        rs = pl.ds(pl.multiple_of(bi * ls, ls), ls)
        s_all = s0_ref[bi].reshape(HK, GLA_DV)
        o, s_new = _gla_chunk(qs_sc[rs, :], ks_sc[rs, :], ke_sc[rs, :], el_sc[rs, :][0:1],
                              u_sc[rs, O_V:O_V + HV], s_all, causal, lane_head, ls)
        for hd in range(GLA_HEADS):
            so_new[bi, hd] = s_new[hd]
            o_sc[rs, hd * GLA_DV:(hd + 1) * GLA_DV] = o[hd]
        return carry

    lax.fori_loop(0, nb, body, 0, unroll=4)

    gn = gn_ref[...]
    for hd in range(GLA_HEADS):
        vs = slice(hd * GLA_DV, (hd + 1) * GLA_DV)
        z_sc[:, D_LRU + hd * GLA_DV:D_LRU + (hd + 1) * GLA_DV] = _gla_output(
            o_sc[:, vs], u_sc[:, O_GO + hd * GLA_DV:O_GO + (hd + 1) * GLA_DV], gn)
    xo_ref[...] = x + _dot(z_sc[...], wout_ref[...])


def _mixer_dec_call(x, frame, h0, s0, s_prev, mix, l, *, nseq, nb, name):
    ls = SUBLANES
    assert nseq % nb == 0
    rows = nb * ls
    (gmix, w_in, wlr, wg2, bg, convw, convb, wg, ba, bx, lam, gn, wout) = mix
    rowspec = lambda w: pl.BlockSpec((rows, w), lambda i: (i, 0))
    sspec = pl.BlockSpec((None, nb, GLA_HEADS, GLA_DK, GLA_DV), lambda i: (l, i, 0, 0, 0))
    wmain_spec = pl.BlockSpec((None, D_MODEL, D_MAIN), lambda i: (l, 0, 0), pipeline_mode=pl.Buffered(1))
    in_specs = [
        rowspec(D_MODEL), rowspec(D_LRU), rowspec(D_LRU), sspec,
        _layer_spec(gmix, l), wmain_spec, _layer_spec(wlr, l), _layer_spec(wg2, l), _layer_spec(bg, l),
        _layer_spec(convw, l), _layer_spec(convb, l), _layer_spec(wg, l), _layer_spec(ba, l),
        _layer_spec(bx, l), _layer_spec(lam, l), _layer_spec(gn, l), _layer_spec(wout, l),
    ]
    out_shape = [
        jax.ShapeDtypeStruct((nseq * ls, D_MODEL), F32),
        jax.ShapeDtypeStruct((nseq * ls, D_LRU), F32),
        jax.ShapeDtypeStruct((nseq * ls, D_LRU), F32),
    ]
    state = (nseq, GLA_HEADS, GLA_DK, GLA_DV)
    sblock = pl.BlockSpec((nb,) + state[1:], lambda i: (i, 0, 0, 0))
    n_prev = len(s_prev)
    if n_prev:
        out_shape.append(jax.ShapeDtypeStruct((n_prev + 1,) + state, F32))
        s_out_spec = pl.BlockSpec((n_prev + 1, nb) + state[1:], lambda i: (0, i, 0, 0, 0))
    else:
        out_shape.append(jax.ShapeDtypeStruct(state, F32))
        s_out_spec = sblock
    out_specs = [rowspec(D_MODEL), rowspec(D_LRU), rowspec(D_LRU), s_out_spec]
    args = [x, frame, h0, s0, gmix, w_in, wlr, wg2, bg, convw, convb, wg, ba, bx, lam, gn, wout]
    args += list(s_prev)
    in_specs += [sblock] * n_prev
    scratch = [
        pltpu.VMEM((rows, D_U), F32), pltpu.VMEM((rows, D_MODEL), F32),
        pltpu.VMEM((rows, HK), F32), pltpu.VMEM((rows, HK), F32), pltpu.VMEM((rows, HK), F32),
        pltpu.VMEM((rows, HK), F32), pltpu.VMEM((rows, HV), F32),
    ]
    return pl.pallas_call(
        functools.partial(_mixer_dec_kernel, nb=nb, n_prev=n_prev),
        grid=(nseq // nb,), in_specs=in_specs, out_specs=out_specs, out_shape=out_shape,
        scratch_shapes=scratch,
        compiler_params=pltpu.CompilerParams(dimension_semantics=("arbitrary",),
                                             vmem_limit_bytes=VMEM_LIMIT),
        name=name,
    )(*args)


def _block_diag_gates(wa, wx):
    per = LRU_BLOCKS // 2
    bw = wa.shape[-1]
    eye = jnp.eye(per, dtype=wa.dtype)

    def bd(w):
        return jnp.einsum("lncd,nm->lncmd", w, eye).reshape(w.shape[0], per * bw, per * bw)

    halves = [jnp.concatenate([bd(wa[:, hf * per:(hf + 1) * per]), bd(wx[:, hf * per:(hf + 1) * per])],
                              axis=-1) for hf in range(2)]
    return jnp.stack(halves, axis=1)


def kernel(x_prompt, x_sample, state_lru_h, state_lru_conv, state_gla_S, meta, norm_ffn1, w_ffn1_gu,
           w_ffn1_down, norm_mix, w_in, lru_conv_w, lru_conv_b, lru_wa, lru_ba, lru_wx, lru_bx,
           lru_lambda, gla_w_gate2, gla_b_gate, gla_norm, w_out, norm_ffn2, w_ffn2_gu, w_ffn2_down,
           norm_final):
    bp, lp, d = x_prompt.shape
    bs, ls, _ = x_sample.shape
    n_meta = meta.shape[0]
    depth = w_in.shape[0]
    assert d == D_MODEL and ls == SUBLANES and n_meta % SUBLANES == 0
    tp, ts = bp * lp, bs * ls
    tsm = ts + n_meta
    assert ts % n_meta == 0
    tm_p = 512 if tp % 512 == 0 else lp
    tm_s = tsm // 2 if (tsm // 2) % SUBLANES == 0 else tsm
    tc_p = 256 if lp % 256 == 0 else GLA_CHUNK
    nb_s = 32 if bs % 32 == 0 else bs
    tail = CONV_W - 1

    vec = lambda a: a.reshape(a.shape[0], 1, a.shape[-1])
    mix = (vec(norm_mix), w_in.astype(BF16),
           jnp.pad(w_in[:, :, D_MAIN:], ((0, 0), (0, 0), (0, LANES - GLA_RANK))).astype(BF16),
           jnp.pad(gla_w_gate2, ((0, 0), (0, LANES - GLA_RANK), (0, 0))).astype(BF16),
           vec(gla_b_gate), lru_conv_w, vec(lru_conv_b), _block_diag_gates(lru_wa, lru_wx).astype(BF16),
           vec(lru_ba), vec(lru_bx), vec(lru_lambda), vec(gla_norm), w_out.astype(BF16))
    n1, n2, nfin = vec(norm_ffn1), vec(norm_ffn2), norm_final.reshape(1, d)
    frames_s = jnp.pad(state_lru_conv, ((0, 0), (0, 0), (SUBLANES - tail, 0), (0, 0)))
    frames_s = frames_s.reshape(depth, ts, D_LRU)
    h0_s = jnp.broadcast_to(state_lru_h[:, :, None, :], (depth, bs, ls, D_LRU)).reshape(depth, ts, D_LRU)
    zero_frame = jnp.zeros((1, SUBLANES, D_LRU), F32)
    zero_h = jnp.zeros((1, 1, D_LRU), F32)
    zero_s = jnp.zeros((1, GLA_HEADS, GLA_DK, GLA_DV), F32)

    xp = x_prompt.reshape(tp, d)
    xs = jnp.concatenate([x_sample.reshape(ts, d), meta.astype(F32)], axis=0)

    hs_p, convs_p, ss_p, hs_s, convs_s, ss_s = [], [], [], [], [], []
    for l in range(depth):
        xp1, xs1 = _ffn_call(xp, xs, n1, w_ffn1_gu, w_ffn1_down, l, tm_a=tm_p, tm_b=tm_s, tf=256,
                             name=f"ffn1_{l}")

        last = l == depth - 1
        xs2, h_s, xl_s, s_s = _mixer_dec_call(xs1, frames_s[l], h0_s[l], state_gla_S,
                                              ss_s if last else (), mix, l, nseq=bs, nb=nb_s,
                                              name=f"mix_s{l}")
        ss_s.append(s_s)
        hs_s.append(h_s.reshape(bs, ls, D_LRU)[:, ls - 1])
        convs_s.append(xl_s.reshape(bs, ls, D_LRU)[:, ls - tail:])

        xm2, h_m, s_m, tail_m = _mixer_seq_call(
            xs1, zero_frame, zero_h, zero_s, mix, l, nb=1, sl=n_meta, tc=n_meta, cg=n_meta,
            row_block0=ts // n_meta, reset_first=True, shared_state=False, name=f"mix_m{l}")
        xs2 = jnp.concatenate([xs2, xm2], axis=0)

        xp2, h_p, s_p, tail_p = _mixer_seq_call(
            xp1, tail_m, h_m, s_m, mix, l, nb=bp, sl=lp, tc=tc_p, cg=GLA_CHUNK, row_block0=0,
            reset_first=False, shared_state=True, name=f"mix_p{l}")
        hs_p.append(h_p[:, 0])
        convs_p.append(tail_p[:, SUBLANES - tail:])
        ss_p.append(s_p)

        fin = nfin if last else None
        xp, xs = _ffn_call(xp2, xs2, n2, w_ffn2_gu, w_ffn2_down, l, final=fin, tm_a=tm_p, tm_b=tm_s,
                           tf=256, name=f"ffn2_{l}")

    s_stack = ss_s[-1] if depth > 1 else ss_s[0][None]
    return (xp.reshape(bp, lp, d), xs[:ts].reshape(bs, ls, d),
            jnp.stack(hs_p), jnp.stack(convs_p), jnp.stack(ss_p),
            jnp.stack(hs_s), jnp.stack(convs_s), s_stack)
```

```python
import functools

import jax
import jax.numpy as jnp
from jax import lax
from jax.experimental import pallas as pl
from jax.experimental.pallas import tpu as pltpu

F32 = jnp.float32
BF16 = jnp.bfloat16

D_MODEL = 1024
D_FF = 2816
D_LRU = 512
LRU_BLOCKS = 8
CONV_W = 4
LRU_C = 8.0
GLA_HEADS = 4
GLA_DV = 128
GLA_DK = 64
GLA_RANK = 16
GLA_GATE_NORM = 16.0
GLA_CHUNK = 64
EPS = 1e-6
HK = GLA_HEADS * GLA_DK
HV = GLA_HEADS * GLA_DV
O_GL, O_Q, O_K, O_V, O_GO = D_LRU, 2 * D_LRU, 2 * D_LRU + HK, 2 * D_LRU + 2 * HK, 2 * D_LRU + 2 * HK + HV
D_MAIN = O_GO + HV
D_U = D_MAIN + HK
SUBLANES = 8
LANES = 128
VMEM_LIMIT = 58 * 1024 * 1024


def _dot(a, b):
    if a.dtype != b.dtype:
        a = a.astype(b.dtype)
    return jnp.dot(a, b, preferred_element_type=F32)


def _dot_nt(a, b):
    return lax.dot_general(a, b, (((1,), (1,)), ((), ())), preferred_element_type=F32)


def _dot_tn(a, b):
    return lax.dot_general(a, b, (((0,), (0,)), ((), ())), preferred_element_type=F32)


def _rms(x, g):
    ms = jnp.mean(x * x, axis=-1, keepdims=True)
    return (x * lax.rsqrt(ms + EPS)) * g


def _sigmoid(x):
    return 0.5 * jnp.tanh(0.5 * x) + 0.5


def _silu(x):
    return x * _sigmoid(x)


def _log_sigmoid(x):
    return jnp.minimum(x, 0.0) - jnp.log(1.0 + jnp.exp(-jnp.abs(x)))


def _layer_spec(a, l):
    nd = a.ndim
    return pl.BlockSpec((None,) + a.shape[1:], lambda *_: (l,) + (0,) * (nd - 1),
                        pipeline_mode=pl.Buffered(1))


def _whole_spec(a):
    nd = a.ndim
    return pl.BlockSpec(a.shape, lambda *_: (0,) * nd, pipeline_mode=pl.Buffered(1))


def _ffn_kernel(*refs, tf, has_final, n_a):
    if has_final:
        xa_ref, xb_ref, g_ref, wgu_ref, wd_ref, gfin_ref, oa_ref, ob_ref = refs
    else:
        xa_ref, xb_ref, g_ref, wgu_ref, wd_ref, oa_ref, ob_ref = refs

    def tile(x_ref, xo_ref):
        x = x_ref[...]
        xn = _rms(x, g_ref[...])
        acc = None
        for c in range(D_FF // tf):
            gate = _dot(xn, wgu_ref[:, c * tf:(c + 1) * tf])
            up = _dot(xn, wgu_ref[:, D_FF + c * tf:D_FF + (c + 1) * tf])
            d = _dot(_silu(gate) * up, wd_ref[c * tf:(c + 1) * tf, :])
            acc = d if acc is None else acc + d
        x = x + 0.5 * acc
        xo_ref[...] = _rms(x, gfin_ref[...]) if has_final else x

    i = pl.program_id(0)

    @pl.when(i < n_a)
    def _():
        tile(xa_ref, oa_ref)

    @pl.when(i >= n_a)
    def _():
        tile(xb_ref, ob_ref)


def _ffn_call(xa, xb, norm, wgu, wd, l, *, final=None, tm_a, tm_b, tf, name):
    (ta, d), (tb, _) = xa.shape, xb.shape
    assert ta % tm_a == 0 and tb % tm_b == 0 and D_FF % tf == 0
    n_a, n_b = ta // tm_a, tb // tm_b
    rows_a = pl.BlockSpec((tm_a, d), lambda i: (jnp.minimum(i, n_a - 1), 0))
    rows_b = pl.BlockSpec((tm_b, d), lambda i: (jnp.maximum(i - n_a, 0), 0))
    args = [xa, xb, norm, wgu, wd]
    specs = [rows_a, rows_b, _layer_spec(norm, l), _layer_spec(wgu, l), _layer_spec(wd, l)]
    if final is not None:
        args.append(final)
        specs.append(_whole_spec(final))
    return pl.pallas_call(
        functools.partial(_ffn_kernel, tf=tf, has_final=final is not None, n_a=n_a),
        grid=(n_a + n_b,), in_specs=specs, out_specs=[rows_a, rows_b],
        out_shape=[jax.ShapeDtypeStruct((ta, d), F32), jax.ShapeDtypeStruct((tb, d), F32)],
        compiler_params=pltpu.CompilerParams(dimension_semantics=("arbitrary",),
                                             vmem_limit_bytes=VMEM_LIMIT),
        name=name,
    )(*args)


def _in_projection(x, gmix_ref, wmain_ref, wlr_ref, wg2_ref, bg_ref, u_sc):
    hn = _rms(x, gmix_ref[...]).astype(wmain_ref.dtype)
    u_sc[:, 0:D_MAIN] = _dot(hn, wmain_ref[...])
    lr = _dot(hn, wlr_ref[...])
    zg = _dot(lr, wg2_ref[...]) + bg_ref[...]
    u_sc[:, D_MAIN:D_U] = _log_sigmoid(zg) / GLA_GATE_NORM


def _lru_gates(xc, p, ba, bx, lam):
    half = xc.shape[1]
    r = _sigmoid(p[:, 0:half] + ba)
    i = _sigmoid(p[:, half:] + bx)
    log_a = -LRU_C * r * jax.nn.softplus(-lam)
    a = jnp.exp(log_a)
    y = -jnp.tanh(log_a) * (a * a + 1.0)
    mult = jnp.where(y > 0.0, y * lax.rsqrt(y), 0.0)
    return a, mult, i * xc


def _scan_groups(a, b):
    rows, cols = a.shape
    a3 = a.reshape(rows // SUBLANES, SUBLANES, cols)
    b3 = b.reshape(rows // SUBLANES, SUBLANES, cols)
    t = lax.broadcasted_iota(jnp.int32, a3.shape, 1)
    s = 1
    while s < SUBLANES:
        keep = t >= s
        b3 = jnp.where(keep, a3 * pltpu.roll(b3, s, 1) + b3, b3)
        a3 = jnp.where(keep, a3 * pltpu.roll(a3, s, 1), a3)
        s *= 2
    return a3, b3


def _scan_rows(a, b, h0):
    a3, b3 = _scan_groups(a, b)
    carry, out = h0, []
    for g in range(a3.shape[0]):
        hg = a3[g] * carry + b3[g]
        out.append(hg)
        carry = hg[SUBLANES - 1:SUBLANES]
    return out[0] if len(out) == 1 else jnp.concatenate(out, axis=0)


def _cumsum_rows(x, t, n):
    s = 1
    while s < n:
        x = x + jnp.where(t >= s, pltpu.roll(x, s, 0), 0.0)
        s *= 2
    return x


def _head_stack(q_s, lane_head):
    return jnp.concatenate([jnp.where(lane_head == hd, q_s, 0.0) for hd in range(GLA_HEADS)], axis=0)


def _gla_kv(k_end, v, per_head):
    if per_head:
        return [_dot_tn(k_end[:, hd * GLA_DK:(hd + 1) * GLA_DK], v[:, hd * GLA_DV:(hd + 1) * GLA_DV])
                for hd in range(GLA_HEADS)]
    kv = _dot_tn(k_end, v)
    return [kv[hd * GLA_DK:(hd + 1) * GLA_DK, hd * GLA_DV:(hd + 1) * GLA_DV] for hd in range(GLA_HEADS)]


def _state_decay(el):
    return jnp.broadcast_to(el, (LANES, HK)).T


def _gla_next_state(el_t, s_all, kv):
    return [el_t[hd * GLA_DK:(hd + 1) * GLA_DK] * s_all[hd * GLA_DK:(hd + 1) * GLA_DK] + kv[hd]
            for hd in range(GLA_HEADS)]


def _gla_intra(att_raw, causal, o_inter, v, n):
    att = jnp.where(causal, att_raw, 0.0)
    return [o_inter[hd * n:(hd + 1) * n] + _dot(att[hd * n:(hd + 1) * n], v[:, hd * GLA_DV:(hd + 1) * GLA_DV])
            for hd in range(GLA_HEADS)]


def _gla_chunk(q_s, k_s, k_end, el, v, s_all, causal, lane_head, n):
    stack = _head_stack(q_s, lane_head)
    o_inter = _dot(stack, s_all)
    att_raw = _dot_nt(stack, k_s)
    kv = _gla_kv(k_end, v, per_head=False)
    el_t = _state_decay(el)
    return _gla_intra(att_raw, causal, o_inter, v, n), _gla_next_state(el_t, s_all, kv)


def _gla_output(o, go, gn):
    return _rms(o, gn) * _silu(go)


def _mixer_seq_kernel(*refs, tc, cg, nj, reset_first):
    (xin_ref, frame0_ref, h0_ref, s0_ref, gmix_ref, wmain_ref, wlr_ref, wg2_ref, bg_ref,
     convw_ref, convb_ref, wg_ref, ba_ref, bx_ref, lam_ref, gn_ref, wout_ref) = refs[:17]
    xo_ref, hl_ref, so_ref, tailo_ref, u, z, tail_sc, h_sc, s_sc = refs[17:]
    c = pl.program_id(0)

    @pl.when(c % nj == 0)
    def _():
        tail_sc[...] = frame0_ref[...]
        h_sc[...] = h0_ref[...]
        s_sc[...] = s0_ref[...].reshape(HK, GLA_DV)

    x = xin_ref[...]
    hn = _rms(x, gmix_ref[...]).astype(wmain_ref.dtype)
    tile_w = 2 * LANES
    carry = {}
    half = D_LRU // 2

    def in_tile(t):
        cols = slice(t * tile_w, (t + 1) * tile_w)
        u[:, cols] = _dot(hn, wmain_ref[:, cols])

    def gate_tile():
        lr = _dot(hn, wlr_ref[...])
        zg = _dot(lr, wg2_ref[...]) + bg_ref[...]
        u[:, D_MAIN:D_U] = _log_sigmoid(zg) / GLA_GATE_NORM

    def out_lru():
        xo_ref[...] = x + _dot(z[:, 0:D_LRU], wout_ref[0:D_LRU, :])

    def out_gla():
        xo_ref[...] += _dot(z[:, D_LRU:D_MODEL], wout_ref[D_LRU:D_MODEL, :])

    def lru_front(hf):
        cs = slice(hf * half, (hf + 1) * half)
        xl = u[:, cs]
        row8 = lax.broadcasted_iota(jnp.int32, (SUBLANES, half), 0)
        xc = convb_ref[:, cs]
        for sft in range(CONV_W - 1, -1, -1):
            if sft == 0:
                sh = xl
            else:
                rolled = pltpu.roll(xl, sft, 0)
                top = jnp.where(row8 < sft, pltpu.roll(tail_sc[:, cs], sft, 0), rolled[0:SUBLANES])
                sh = top if tc == SUBLANES else jnp.concatenate([top, rolled[SUBLANES:]], axis=0)
            xc = xc + convw_ref[CONV_W - 1 - sft:CONV_W - sft, cs] * sh
        carry["tail", hf] = xl[tc - SUBLANES:tc]
        tail_sc[:, cs] = carry["tail", hf]
        carry["xc", hf] = xc

    def lru_gate_mm(hf):
        carry["p", hf] = _dot(carry["xc", hf], wg_ref[hf])

    def lru_mid(hf):
        cs = slice(hf * half, (hf + 1) * half)
        a, mult, ix = _lru_gates(carry["xc", hf], carry["p", hf], ba_ref[:, cs], bx_ref[:, cs],
                                 lam_ref[:, cs])
        if reset_first:
            row = lax.broadcasted_iota(jnp.int32, (tc, half), 0)
            first = jnp.logical_and(row == 0, c % nj == 0)
            mult = jnp.where(first, 1.0, mult)
            a = jnp.where(first, 0.0, a)
        carry["a", hf], carry["b", hf] = a, mult * ix

    def lru_back(hf):
        cs = slice(hf * half, (hf + 1) * half)
        h = _scan_rows(carry["a", hf], carry["b", hf], h_sc[:, cs])
        carry["h", hf] = h[tc - 1:tc]
        h_sc[:, cs] = carry["h", hf]
        z[:, cs] = h * jax.nn.gelu(u[:, O_GL + hf * half:O_GL + (hf + 1) * half])

    lane_head = lax.broadcasted_iota(jnp.int32, (cg, HK), 1) // GLA_DK
    n_sub = tc // cg
    sub = lambda ci: slice(ci * cg, (ci + 1) * cg)

    def gla_prep(ci):
        tq = lax.broadcasted_iota(jnp.int32, (cg, HK), 0)
        b = _cumsum_rows(u[sub(ci), D_MAIN:D_U], tq, cg)
        bl = b[cg - 1:cg]
        k = u[sub(ci), O_K:O_K + HK]
        carry["stack", ci] = _head_stack(u[sub(ci), O_Q:O_Q + HK] * (GLA_DK ** -0.5) * jnp.exp(b),
                                         lane_head)
        carry["ks", ci], carry["ke", ci], carry["el", ci] = k * jnp.exp(-b), k * jnp.exp(bl - b), jnp.exp(bl)

    def gla_free_mm(ci):
        carry["att", ci] = _dot_nt(carry["stack", ci], carry["ks", ci])
        carry["kv", ci] = _gla_kv(carry["ke", ci], u[sub(ci), O_V:O_V + HV], per_head=True)
        carry["el_t", ci] = _state_decay(carry["el", ci])

    def gla_state(ci):
        s_all = carry["S"]
        carry["oi", ci] = _dot(carry["stack", ci], s_all)
        carry["S"] = jnp.concatenate(_gla_next_state(carry["el_t", ci], s_all, carry["kv", ci]), axis=0)

    def gla_out(ci):
        causal = ((lax.broadcasted_iota(jnp.int32, (GLA_HEADS * cg, cg), 0) % cg)
                  >= lax.broadcasted_iota(jnp.int32, (GLA_HEADS * cg, cg), 1))
        o = _gla_intra(carry["att", ci], causal, carry["oi", ci], u[sub(ci), O_V:O_V + HV], cg)
        for hd in range(GLA_HEADS):
            z[sub(ci), D_LRU + hd * GLA_DV:D_LRU + (hd + 1) * GLA_DV] = _gla_output(
                o[hd], u[sub(ci), O_GO + hd * GLA_DV:O_GO + (hd + 1) * GLA_DV], gn_ref[...])

    def gla_begin():
        carry["S"] = s_sc[...]

    def gla_end():
        s_sc[...] = carry["S"]

    lru = {n: functools.partial(f, hf) for hf in range(2)
           for n, f in ((f"front{hf}", lru_front), (f"gmm{hf}", lru_gate_mm), (f"mid{hf}", lru_mid),
                        (f"back{hf}", lru_back))}
    tiles = [functools.partial(in_tile, t) for t in range(D_MAIN // tile_w)]
    assert len(tiles) == 10
    prep = [functools.partial(gla_prep, ci) for ci in range(n_sub)]
    free = [functools.partial(gla_free_mm, ci) for ci in range(n_sub)]
    head = (tiles[0:3] + [lru["front0"], tiles[3], lru["front1"], lru["gmm0"], lru["gmm1"],
                          tiles[4], lru["mid0"], tiles[5], gate_tile, lru["back0"], tiles[6], lru["mid1"],
                          tiles[7], lru["back1"], tiles[8]])
    mid = [prep[0], tiles[9]] + ([prep[1], out_lru] if n_sub > 1 else [out_lru])
    for ci in range(2, n_sub):
        mid += [prep[ci], free[ci - 2]]
    mid += free[max(n_sub - 2, 0):]
    order = (head + mid + [gla_begin] + [functools.partial(gla_state, ci) for ci in range(n_sub)]
             + [gla_end] + [functools.partial(gla_out, ci) for ci in range(n_sub)] + [out_gla])
    for piece in order:
        piece()

    for hf in range(2):
        cs = slice(hf * half, (hf + 1) * half)
        hl_ref[:, cs] = carry["h", hf]
        tailo_ref[:, cs] = carry["tail", hf]
    so_ref[...] = s_sc[...].reshape(GLA_HEADS, GLA_DK, GLA_DV)


def _mixer_seq_call(x, frame0, h0, s0, mix, l, *, nb, sl, tc, cg, row_block0, reset_first,
                    shared_state, name):
    assert sl % tc == 0 and tc % cg == 0 and tc % SUBLANES == 0
    nj = sl // tc
    xrow = pl.BlockSpec((tc, D_MODEL), lambda s: (row_block0 + s, 0))
    seq = lambda s: s // nj
    if shared_state:
        st = lambda s: (0, 0, 0)
        st4 = lambda s: (0, 0, 0, 0)
    else:
        st = lambda s: (seq(s), 0, 0)
        st4 = lambda s: (seq(s), 0, 0, 0)
    (gmix, w_in, wlr, wg2, bg, convw, convb, wg, ba, bx, lam, gn, wout) = mix
    wmain_spec = pl.BlockSpec((None, D_MODEL, D_MAIN), lambda s: (l, 0, 0), pipeline_mode=pl.Buffered(1))
    args = [x, frame0, h0, s0, gmix, w_in, wlr, wg2, bg, convw, convb, wg, ba, bx, lam, gn, wout]
    in_specs = [
        xrow,
        pl.BlockSpec((None, SUBLANES, D_LRU), st),
        pl.BlockSpec((None, 1, D_LRU), st),
        pl.BlockSpec((None, GLA_HEADS, GLA_DK, GLA_DV), st4),
        _layer_spec(gmix, l), wmain_spec, _layer_spec(wlr, l), _layer_spec(wg2, l), _layer_spec(bg, l),
        _layer_spec(convw, l), _layer_spec(convb, l), _layer_spec(wg, l), _layer_spec(ba, l),
        _layer_spec(bx, l), _layer_spec(lam, l), _layer_spec(gn, l), _layer_spec(wout, l),
    ]
    out_shape = [
        jax.ShapeDtypeStruct((nb * sl, D_MODEL), F32),
        jax.ShapeDtypeStruct((nb, 1, D_LRU), F32),
        jax.ShapeDtypeStruct((nb, GLA_HEADS, GLA_DK, GLA_DV), F32),
        jax.ShapeDtypeStruct((nb, SUBLANES, D_LRU), F32),
    ]
    out_specs = [
        pl.BlockSpec((tc, D_MODEL), lambda s: (s, 0)),
        pl.BlockSpec((None, 1, D_LRU), lambda s: (seq(s), 0, 0)),
        pl.BlockSpec((None, GLA_HEADS, GLA_DK, GLA_DV), lambda s: (seq(s), 0, 0, 0)),
        pl.BlockSpec((None, SUBLANES, D_LRU), lambda s: (seq(s), 0, 0)),
    ]
    scratch = [
        pltpu.VMEM((tc, D_U), F32),
        pltpu.VMEM((tc, D_MODEL), F32),
        pltpu.VMEM((SUBLANES, D_LRU), F32),
        pltpu.VMEM((1, D_LRU), F32),
        pltpu.VMEM((HK, GLA_DV), F32),
    ]
    kern = functools.partial(_mixer_seq_kernel, tc=tc, cg=cg, nj=nj, reset_first=reset_first)
    return pl.pallas_call(
        kern, grid=(nb * nj,), in_specs=in_specs, out_specs=out_specs, out_shape=out_shape,
        scratch_shapes=scratch,
        compiler_params=pltpu.CompilerParams(dimension_semantics=("arbitrary",),
                                             vmem_limit_bytes=VMEM_LIMIT),
        name=name,
    )(*args)


def _mixer_dec_kernel(*refs, nb, n_prev):
    (x_ref, frame_ref, h0_ref, s0_ref, gmix_ref, wmain_ref, wlr_ref, wg2_ref, bg_ref,
     convw_ref, convb_ref, wg_ref, ba_ref, bx_ref, lam_ref, gn_ref, wout_ref) = refs[:17]
    prev_refs = refs[17:17 + n_prev]
    (xo_ref, h_ref, xl_ref, so_ref,
     u_sc, z_sc, qs_sc, ks_sc, ke_sc, el_sc, o_sc) = refs[17 + n_prev:]
    for j, prev in enumerate(prev_refs):
        so_ref[j] = prev[...]
    so_new = so_ref.at[n_prev] if n_prev else so_ref
    ls = SUBLANES
    rows = nb * ls
    x = x_ref[...]
    _in_projection(x, gmix_ref, wmain_ref, wlr_ref, wg2_ref, bg_ref, u_sc)

    t = lax.broadcasted_iota(jnp.int32, (rows, D_LRU), 0) % ls
    xl = u_sc[:, 0:D_LRU]
    xl_ref[...] = xl
    frame = frame_ref[...]
    xc = convb_ref[...]
    for s in range(CONV_W - 1, -1, -1):
        if s == 0:
            sh = xl
        else:
            sh = jnp.where(t >= s, pltpu.roll(xl, s, 0), pltpu.roll(frame, (rows - ls + s) % rows, 0))
        xc = xc + convw_ref[CONV_W - 1 - s:CONV_W - s, :] * sh
    half = D_LRU // 2
    parts = []
    for hf in range(2):
        cs = slice(hf * half, (hf + 1) * half)
        parts.append(_lru_gates(xc[:, cs], _dot(xc[:, cs], wg_ref[hf]), ba_ref[:, cs], bx_ref[:, cs],
                                lam_ref[:, cs]))
    a, mult, ix = (jnp.concatenate([p[n] for p in parts], axis=1) for n in range(3))
    a3, b3 = _scan_groups(a, mult * ix)
    h = (a3 * h0_ref[...].reshape(a3.shape) + b3).reshape(rows, D_LRU)
    h_ref[...] = h
    z_sc[:, 0:D_LRU] = h * jax.nn.gelu(u_sc[:, O_GL:O_GL + D_LRU])

    tq = lax.broadcasted_iota(jnp.int32, (rows, HK), 0) % ls
    b = _cumsum_rows(u_sc[:, D_MAIN:D_U], tq, ls)
    bl = jnp.where(tq == ls - 1, b, 0.0)
    s = 1
    while s < ls:
        bl = bl + jnp.where(tq + s < ls, pltpu.roll(bl, rows - s, 0), 0.0)
        s *= 2
    k = u_sc[:, O_K:O_K + HK]
    qs_sc[...] = (u_sc[:, O_Q:O_Q + HK] * (GLA_DK ** -0.5)) * jnp.exp(b)
    ks_sc[...] = k * jnp.exp(-b)
    ke_sc[...] = k * jnp.exp(bl - b)
    el_sc[...] = jnp.exp(bl)

    lane_head = lax.broadcasted_iota(jnp.int32, (ls, HK), 1) // GLA_DK
    causal = ((lax.broadcasted_iota(jnp.int32, (GLA_HEADS * ls, ls), 0) % ls)
              >= lax.broadcasted_iota(jnp.int32, (GLA_HEADS * ls, ls), 1))

    def body(bi, carry):
        rs = pl.ds(pl.multiple_of(bi * ls, ls), ls)
        s_all = s0_ref[bi].reshape(HK, GLA_DV)
        o, s_new = _gla_chunk(qs_sc[rs, :], ks_sc[rs, :], ke_sc[rs, :], el_sc[rs, :][0:1],
                              u_sc[rs, O_V:O_V + HV], s_all, causal, lane_head, ls)
        for hd in range(GLA_HEADS):
            so_new[bi, hd] = s_new[hd]
            o_sc[rs, hd * GLA_DV:(hd + 1) * GLA_DV] = o[hd]
        return carry

    lax.fori_loop(0, nb, body, 0, unroll=4)

    gn = gn_ref[...]
    for hd in range(GLA_HEADS):
        vs = slice(hd * GLA_DV, (hd + 1) * GLA_DV)
        z_sc[:, D_LRU + hd * GLA_DV:D_LRU + (hd + 1) * GLA_DV] = _gla_output(
            o_sc[:, vs], u_sc[:, O_GO + hd * GLA_DV:O_GO + (hd + 1) * GLA_DV], gn)
    xo_ref[...] = x + _dot(z_sc[...], wout_ref[...])


def _mixer_dec_call(x, frame, h0, s0, s_prev, mix, l, *, nseq, nb, name):
    ls = SUBLANES
    assert nseq % nb == 0
    rows = nb * ls
    (gmix, w_in, wlr, wg2, bg, convw, convb, wg, ba, bx, lam, gn, wout) = mix
    rowspec = lambda w: pl.BlockSpec((rows, w), lambda i: (i, 0))
    sspec = pl.BlockSpec((None, nb, GLA_HEADS, GLA_DK, GLA_DV), lambda i: (l, i, 0, 0, 0))
    wmain_spec = pl.BlockSpec((None, D_MODEL, D_MAIN), lambda i: (l, 0, 0), pipeline_mode=pl.Buffered(1))
    in_specs = [
        rowspec(D_MODEL), rowspec(D_LRU), rowspec(D_LRU), sspec,
        _layer_spec(gmix, l), wmain_spec, _layer_spec(wlr, l), _layer_spec(wg2, l), _layer_spec(bg, l),
        _layer_spec(convw, l), _layer_spec(convb, l), _layer_spec(wg, l), _layer_spec(ba, l),
        _layer_spec(bx, l), _layer_spec(lam, l), _layer_spec(gn, l), _layer_spec(wout, l),
    ]
    out_shape = [
        jax.ShapeDtypeStruct((nseq * ls, D_MODEL), F32),
        jax.ShapeDtypeStruct((nseq * ls, D_LRU), F32),
        jax.ShapeDtypeStruct((nseq * ls, D_LRU), F32),
    ]
    state = (nseq, GLA_HEADS, GLA_DK, GLA_DV)
    sblock = pl.BlockSpec((nb,) + state[1:], lambda i: (i, 0, 0, 0))
    n_prev = len(s_prev)
    if n_prev:
        out_shape.append(jax.ShapeDtypeStruct((n_prev + 1,) + state, F32))
        s_out_spec = pl.BlockSpec((n_prev + 1, nb) + state[1:], lambda i: (0, i, 0, 0, 0))
    else:
        out_shape.append(jax.ShapeDtypeStruct(state, F32))
        s_out_spec = sblock
    out_specs = [rowspec(D_MODEL), rowspec(D_LRU), rowspec(D_LRU), s_out_spec]
    args = [x, frame, h0, s0, gmix, w_in, wlr, wg2, bg, convw, convb, wg, ba, bx, lam, gn, wout]
    args += list(s_prev)
    in_specs += [sblock] * n_prev
    scratch = [
        pltpu.VMEM((rows, D_U), F32), pltpu.VMEM((rows, D_MODEL), F32),
        pltpu.VMEM((rows, HK), F32), pltpu.VMEM((rows, HK), F32), pltpu.VMEM((rows, HK), F32),
        pltpu.VMEM((rows, HK), F32), pltpu.VMEM((rows, HV), F32),
    ]
    return pl.pallas_call(
        functools.partial(_mixer_dec_kernel, nb=nb, n_prev=n_prev),
        grid=(nseq // nb,), in_specs=in_specs, out_specs=out_specs, out_shape=out_shape,
        scratch_shapes=scratch,
        compiler_params=pltpu.CompilerParams(dimension_semantics=("arbitrary",),
                                             vmem_limit_bytes=VMEM_LIMIT),
        name=name,
    )(*args)


def _block_diag_gates(wa, wx):
    per = LRU_BLOCKS // 2
    bw = wa.shape[-1]
    eye = jnp.eye(per, dtype=wa.dtype)

    def bd(w):
        return jnp.einsum("lncd,nm->lncmd", w, eye).reshape(w.shape[0], per * bw, per * bw)

    halves = [jnp.concatenate([bd(wa[:, hf * per:(hf + 1) * per]), bd(wx[:, hf * per:(hf + 1) * per])],
                              axis=-1) for hf in range(2)]
    return jnp.stack(halves, axis=1)


def kernel(x_prompt, x_sample, state_lru_h, state_lru_conv, state_gla_S, meta, norm_ffn1, w_ffn1_gu,
           w_ffn1_down, norm_mix, w_in, lru_conv_w, lru_conv_b, lru_wa, lru_ba, lru_wx, lru_bx,
           lru_lambda, gla_w_gate2, gla_b_gate, gla_norm, w_out, norm_ffn2, w_ffn2_gu, w_ffn2_down,
           norm_final):
    bp, lp, d = x_prompt.shape
    bs, ls, _ = x_sample.shape
    n_meta = meta.shape[0]
    depth = w_in.shape[0]
    assert d == D_MODEL and ls == SUBLANES and n_meta % SUBLANES == 0
    tp, ts = bp * lp, bs * ls
    tsm = ts + n_meta
    assert ts % n_meta == 0
    tm_p = 512 if tp % 512 == 0 else lp
    tm_s = tsm // 2 if (tsm // 2) % SUBLANES == 0 else tsm
    tc_p = 512 if lp % 512 == 0 else GLA_CHUNK
    nb_s = 32 if bs % 32 == 0 else bs
    tail = CONV_W - 1

    vec = lambda a: a.reshape(a.shape[0], 1, a.shape[-1])
    mix = (vec(norm_mix), w_in.astype(BF16),
           jnp.pad(w_in[:, :, D_MAIN:], ((0, 0), (0, 0), (0, LANES - GLA_RANK))).astype(BF16),
           jnp.pad(gla_w_gate2, ((0, 0), (0, LANES - GLA_RANK), (0, 0))).astype(BF16),
           vec(gla_b_gate), lru_conv_w, vec(lru_conv_b), _block_diag_gates(lru_wa, lru_wx).astype(BF16),
           vec(lru_ba), vec(lru_bx), vec(lru_lambda), vec(gla_norm), w_out.astype(BF16))
    n1, n2, nfin = vec(norm_ffn1), vec(norm_ffn2), norm_final.reshape(1, d)
    frames_s = jnp.pad(state_lru_conv, ((0, 0), (0, 0), (SUBLANES - tail, 0), (0, 0)))
    frames_s = frames_s.reshape(depth, ts, D_LRU)
    h0_s = jnp.broadcast_to(state_lru_h[:, :, None, :], (depth, bs, ls, D_LRU)).reshape(depth, ts, D_LRU)
    zero_frame = jnp.zeros((1, SUBLANES, D_LRU), F32)
    zero_h = jnp.zeros((1, 1, D_LRU), F32)
    zero_s = jnp.zeros((1, GLA_HEADS, GLA_DK, GLA_DV), F32)

    xp = x_prompt.reshape(tp, d)
    xs = jnp.concatenate([x_sample.reshape(ts, d), meta.astype(F32)], axis=0)

    hs_p, convs_p, ss_p, hs_s, convs_s, ss_s = [], [], [], [], [], []
    for l in range(depth):
        xp1, xs1 = _ffn_call(xp, xs, n1, w_ffn1_gu, w_ffn1_down, l, tm_a=tm_p, tm_b=tm_s, tf=256,
                             name=f"ffn1_{l}")

        last = l == depth - 1
        xs2, h_s, xl_s, s_s = _mixer_dec_call(xs1, frames_s[l], h0_s[l], state_gla_S,
                                              ss_s if last else (), mix, l, nseq=bs, nb=nb_s,
                                              name=f"mix_s{l}")
        ss_s.append(s_s)
        hs_s.append(h_s.reshape(bs, ls, D_LRU)[:, ls - 1])
        convs_s.append(xl_s.reshape(bs, ls, D_LRU)[:, ls - tail:])

        xm2, h_m, s_m, tail_m = _mixer_seq_call(
            xs1, zero_frame, zero_h, zero_s, mix, l, nb=1, sl=n_meta, tc=n_meta, cg=n_meta,
            row_block0=ts // n_meta, reset_first=True, shared_state=False, name=f"mix_m{l}")
        xs2 = jnp.concatenate([xs2, xm2], axis=0)

        xp2, h_p, s_p, tail_p = _mixer_seq_call(
            xp1, tail_m, h_m, s_m, mix, l, nb=bp, sl=lp, tc=tc_p, cg=GLA_CHUNK, row_block0=0,
            reset_first=False, shared_state=True, name=f"mix_p{l}")
        hs_p.append(h_p[:, 0])
        convs_p.append(tail_p[:, SUBLANES - tail:])
        ss_p.append(s_p)

        fin = nfin if last else None
        xp, xs = _ffn_call(xp2, xs2, n2, w_ffn2_gu, w_ffn2_down, l, final=fin, tm_a=tm_p, tm_b=tm_s,
                           tf=256, name=f"ffn2_{l}")

    s_stack = ss_s[-1] if depth > 1 else ss_s[0][None]
    return (xp.reshape(bp, lp, d), xs[:ts].reshape(bs, ls, d),
            jnp.stack(hs_p), jnp.stack(convs_p), jnp.stack(ss_p),
            jnp.stack(hs_s), jnp.stack(convs_s), s_stack)
```

```python
import functools

import jax
import jax.numpy as jnp
from jax import lax
from jax.experimental import pallas as pl
from jax.experimental.pallas import tpu as pltpu

F32 = jnp.float32
BF16 = jnp.bfloat16

D_MODEL = 1024
D_FF = 2816
D_LRU = 512
LRU_BLOCKS = 8
CONV_W = 4
LRU_C = 8.0
GLA_HEADS = 4
GLA_DV = 128
GLA_DK = 64
GLA_RANK = 16
GLA_GATE_NORM = 16.0
GLA_CHUNK = 64
EPS = 1e-6
HK = GLA_HEADS * GLA_DK
HV = GLA_HEADS * GLA_DV
O_GL, O_Q, O_K, O_V, O_GO = D_LRU, 2 * D_LRU, 2 * D_LRU + HK, 2 * D_LRU + 2 * HK, 2 * D_LRU + 2 * HK + HV
D_MAIN = O_GO + HV
D_U = D_MAIN + HK
SUBLANES = 8
LANES = 128
VMEM_LIMIT = 58 * 1024 * 1024
ROW_MIX, ROW_GATE_B, ROW_LAM_CONVB, ROW_GLA, ROW_FFN1, ROW_FFN2, ROW_FINAL = range(7)


def _dot(a, b):
    if a.dtype != b.dtype:
        a = a.astype(b.dtype)
    return jnp.dot(a, b, preferred_element_type=F32)


def _dot_nt(a, b):
    return lax.dot_general(a, b, (((1,), (1,)), ((), ())), preferred_element_type=F32)


def _dot_tn(a, b):
    return lax.dot_general(a, b, (((0,), (0,)), ((), ())), preferred_element_type=F32)


def _rms(x, g):
    ms = jnp.mean(x * x, axis=-1, keepdims=True)
    return (x * lax.rsqrt(ms + EPS)) * g


def _sigmoid(x):
    return 0.5 * jnp.tanh(0.5 * x) + 0.5


def _silu(x):
    return x * _sigmoid(x)


def _log_sigmoid(x):
    return jnp.minimum(x, 0.0) - jnp.log(1.0 + jnp.exp(-jnp.abs(x)))


def _pack_vectors(norm_mix, ba, bx, lam, conv_b, b_gate, gla_norm, norm_ffn1, norm_ffn2, norm_final):
    depth, d = norm_mix.shape
    rows = [norm_mix, jnp.concatenate([ba, bx], axis=1), jnp.concatenate([lam, conv_b], axis=1),
            jnp.pad(jnp.concatenate([b_gate, gla_norm], axis=1), ((0, 0), (0, d - HK - GLA_DV))),
            norm_ffn1, norm_ffn2, jnp.broadcast_to(norm_final[None, :], (depth, d)),
            jnp.zeros((depth, d), F32)]
    return jnp.stack(rows, axis=1)


def _mixer_vectors(vecs_ref):
    row = lambda r, a, b: vecs_ref.at[r:r + 1, a:b]
    return (row(ROW_MIX, 0, D_MODEL), row(ROW_GLA, 0, HK), row(ROW_LAM_CONVB, D_LRU, 2 * D_LRU),
            row(ROW_GATE_B, 0, D_LRU), row(ROW_GATE_B, D_LRU, 2 * D_LRU), row(ROW_LAM_CONVB, 0, D_LRU),
            row(ROW_GLA, HK, HK + GLA_DV))


def _layer_spec(a, l):
    nd = a.ndim
    return pl.BlockSpec((None,) + a.shape[1:], lambda *_: (l,) + (0,) * (nd - 1),
                        pipeline_mode=pl.Buffered(1))


def _ffn_kernel(xa_ref, xb_ref, vecs_ref, wgu_ref, wd_ref, oa_ref, ob_ref, *, tf, norm_row, has_final, n_a):
    g_ref = vecs_ref.at[norm_row:norm_row + 1, :]
    gfin_ref = vecs_ref.at[ROW_FINAL:ROW_FINAL + 1, :]

    def tile(x_ref, xo_ref):
        x = x_ref[...]
        xn = _rms(x, g_ref[...])
        acc = None
        for c in range(D_FF // tf):
            gate = _dot(xn, wgu_ref[:, c * tf:(c + 1) * tf])
            up = _dot(xn, wgu_ref[:, D_FF + c * tf:D_FF + (c + 1) * tf])
            d = _dot(_silu(gate) * up, wd_ref[c * tf:(c + 1) * tf, :])
            acc = d if acc is None else acc + d
        x = x + 0.5 * acc
        xo_ref[...] = _rms(x, gfin_ref[...]) if has_final else x

    i = pl.program_id(0)

    @pl.when(i < n_a)
    def _():
        tile(xa_ref, oa_ref)

    @pl.when(i >= n_a)
    def _():
        tile(xb_ref, ob_ref)


def _ffn_call(xa, xb, vecs, norm_row, wgu, wd, l, *, final, tm_a, tm_b, tf, name):
    (ta, d), (tb, _) = xa.shape, xb.shape
    assert ta % tm_a == 0 and tb % tm_b == 0 and D_FF % tf == 0
    n_a, n_b = ta // tm_a, tb // tm_b
    rows_a = pl.BlockSpec((tm_a, d), lambda i: (jnp.minimum(i, n_a - 1), 0))
    rows_b = pl.BlockSpec((tm_b, d), lambda i: (jnp.maximum(i - n_a, 0), 0))
    args = [xa, xb, vecs, wgu, wd]
    specs = [rows_a, rows_b, _layer_spec(vecs, l), _layer_spec(wgu, l), _layer_spec(wd, l)]
    return pl.pallas_call(
        functools.partial(_ffn_kernel, tf=tf, norm_row=norm_row, has_final=final, n_a=n_a),
        grid=(n_a + n_b,), in_specs=specs, out_specs=[rows_a, rows_b],
        out_shape=[jax.ShapeDtypeStruct((ta, d), F32), jax.ShapeDtypeStruct((tb, d), F32)],
        compiler_params=pltpu.CompilerParams(dimension_semantics=("arbitrary",),
                                             vmem_limit_bytes=VMEM_LIMIT),
        name=name,
    )(*args)


def _in_projection(x, gmix_ref, wmain_ref, wlr_ref, wg2_ref, bg_ref, u_sc):
    hn = _rms(x, gmix_ref[...]).astype(wmain_ref.dtype)
    u_sc[:, 0:D_MAIN] = _dot(hn, wmain_ref[...])
    lr = _dot(hn, wlr_ref[...])
    zg = _dot(lr, wg2_ref[...]) + bg_ref[...]
    u_sc[:, D_MAIN:D_U] = _log_sigmoid(zg) / GLA_GATE_NORM


def _lru_gates(xc, p, ba, bx, lam):
    half = xc.shape[1]
    r = _sigmoid(p[:, 0:half] + ba)
    i = _sigmoid(p[:, half:] + bx)
    log_a = -LRU_C * r * jax.nn.softplus(-lam)
    a = jnp.exp(log_a)
    y = -jnp.tanh(log_a) * (a * a + 1.0)
    mult = jnp.where(y > 0.0, y * lax.rsqrt(y), 0.0)
    return a, mult, i * xc


def _scan_groups(a, b):
    rows, cols = a.shape
    a3 = a.reshape(rows // SUBLANES, SUBLANES, cols)
    b3 = b.reshape(rows // SUBLANES, SUBLANES, cols)
    t = lax.broadcasted_iota(jnp.int32, a3.shape, 1)
    s = 1
    while s < SUBLANES:
        keep = t >= s
        b3 = jnp.where(keep, a3 * pltpu.roll(b3, s, 1) + b3, b3)
        a3 = jnp.where(keep, a3 * pltpu.roll(a3, s, 1), a3)
        s *= 2
    return a3, b3


def _scan_rows(a, b, h0):
    a3, b3 = _scan_groups(a, b)
    carry, out = h0, []
    for g in range(a3.shape[0]):
        hg = a3[g] * carry + b3[g]
        out.append(hg)
        carry = hg[SUBLANES - 1:SUBLANES]
    return out[0] if len(out) == 1 else jnp.concatenate(out, axis=0)


def _cumsum_rows(x, t, n):
    s = 1
    while s < n:
        x = x + jnp.where(t >= s, pltpu.roll(x, s, 0), 0.0)
        s *= 2
    return x


def _head_stack(q_s, lane_head):
    return jnp.concatenate([jnp.where(lane_head == hd, q_s, 0.0) for hd in range(GLA_HEADS)], axis=0)


def _gla_kv(k_end, v, per_head):
    if per_head:
        return [_dot_tn(k_end[:, hd * GLA_DK:(hd + 1) * GLA_DK], v[:, hd * GLA_DV:(hd + 1) * GLA_DV])
                for hd in range(GLA_HEADS)]
    kv = _dot_tn(k_end, v)
    return [kv[hd * GLA_DK:(hd + 1) * GLA_DK, hd * GLA_DV:(hd + 1) * GLA_DV] for hd in range(GLA_HEADS)]


def _state_decay(el):
    return jnp.broadcast_to(el, (LANES, HK)).T


def _gla_next_state(el_t, s_all, kv):
    return [el_t[hd * GLA_DK:(hd + 1) * GLA_DK] * s_all[hd * GLA_DK:(hd + 1) * GLA_DK] + kv[hd]
            for hd in range(GLA_HEADS)]


def _gla_intra(att_raw, causal, o_inter, v, n):
    att = jnp.where(causal, att_raw, 0.0)
    return [o_inter[hd * n:(hd + 1) * n] + _dot(att[hd * n:(hd + 1) * n], v[:, hd * GLA_DV:(hd + 1) * GLA_DV])
            for hd in range(GLA_HEADS)]


def _gla_chunk(q_s, k_s, k_end, el, v, s_all, causal, lane_head, n):
    stack = _head_stack(q_s, lane_head)
    o_inter = _dot(stack, s_all)
    att_raw = _dot_nt(stack, k_s)
    kv = _gla_kv(k_end, v, per_head=False)
    el_t = _state_decay(el)
    return _gla_intra(att_raw, causal, o_inter, v, n), _gla_next_state(el_t, s_all, kv)


def _gla_output(o, go, gn):
    return _rms(o, gn) * _silu(go)


def _mixer_seq_kernel(*refs, tc, cg, nj, reset_first):
    (xin_ref, frame0_ref, h0_ref, s0_ref, vecs_ref, wmain_ref, wlr_ref, wg2_ref, convw_ref, wg_ref,
     wout_ref) = refs[:11]
    gmix_ref, bg_ref, convb_ref, ba_ref, bx_ref, lam_ref, gn_ref = _mixer_vectors(vecs_ref)
    xo_ref, hl_ref, so_ref, tailo_ref, u, z, tail_sc, h_sc, s_sc = refs[11:]
    c = pl.program_id(0)

    @pl.when(c % nj == 0)
    def _():
        tail_sc[...] = frame0_ref[...]
        h_sc[...] = h0_ref[...]
        s_sc[...] = s0_ref[...].reshape(HK, GLA_DV)

    x = xin_ref[...]
    hn = _rms(x, gmix_ref[...]).astype(wmain_ref.dtype)
    tile_w = 2 * LANES
    carry = {}
    half = D_LRU // 2

    def in_tile(t):
        cols = slice(t * tile_w, (t + 1) * tile_w)
        u[:, cols] = _dot(hn, wmain_ref[:, cols])

    def gate_tile():
        lr = _dot(hn, wlr_ref[...])
        zg = _dot(lr, wg2_ref[...]) + bg_ref[...]
        u[:, D_MAIN:D_U] = _log_sigmoid(zg) / GLA_GATE_NORM

    def out_lru():
        xo_ref[...] = x + _dot(z[:, 0:D_LRU], wout_ref[0:D_LRU, :])

    def out_gla():
        xo_ref[...] += _dot(z[:, D_LRU:D_MODEL], wout_ref[D_LRU:D_MODEL, :])

    def lru_front(hf):
        cs = slice(hf * half, (hf + 1) * half)
        xl = u[:, cs]
        row8 = lax.broadcasted_iota(jnp.int32, (SUBLANES, half), 0)
        xc = convb_ref[:, cs]
        for sft in range(CONV_W - 1, -1, -1):
            if sft == 0:
                sh = xl
            else:
                rolled = pltpu.roll(xl, sft, 0)
                top = jnp.where(row8 < sft, pltpu.roll(tail_sc[:, cs], sft, 0), rolled[0:SUBLANES])
                sh = top if tc == SUBLANES else jnp.concatenate([top, rolled[SUBLANES:]], axis=0)
            xc = xc + convw_ref[CONV_W - 1 - sft:CONV_W - sft, cs] * sh
        carry["tail", hf] = xl[tc - SUBLANES:tc]
        tail_sc[:, cs] = carry["tail", hf]
        carry["xc", hf] = xc

    def lru_gate_mm(hf):
        carry["p", hf] = _dot(carry["xc", hf], wg_ref[hf])

    def lru_mid(hf):
        cs = slice(hf * half, (hf + 1) * half)
        a, mult, ix = _lru_gates(carry["xc", hf], carry["p", hf], ba_ref[:, cs], bx_ref[:, cs],
                                 lam_ref[:, cs])
        if reset_first:
            row = lax.broadcasted_iota(jnp.int32, (tc, half), 0)
            first = jnp.logical_and(row == 0, c % nj == 0)
            mult = jnp.where(first, 1.0, mult)
            a = jnp.where(first, 0.0, a)
        carry["a", hf], carry["b", hf] = a, mult * ix

    def lru_back(hf):
        cs = slice(hf * half, (hf + 1) * half)
        h = _scan_rows(carry["a", hf], carry["b", hf], h_sc[:, cs])
        carry["h", hf] = h[tc - 1:tc]
        h_sc[:, cs] = carry["h", hf]
        z[:, cs] = h * jax.nn.gelu(u[:, O_GL + hf * half:O_GL + (hf + 1) * half])

    lane_head = lax.broadcasted_iota(jnp.int32, (cg, HK), 1) // GLA_DK
    n_sub = tc // cg
    sub = lambda ci: slice(ci * cg, (ci + 1) * cg)

    def gla_prep(ci):
        tq = lax.broadcasted_iota(jnp.int32, (cg, HK), 0)
        b = _cumsum_rows(u[sub(ci), D_MAIN:D_U], tq, cg)
        bl = b[cg - 1:cg]
        k = u[sub(ci), O_K:O_K + HK]
        carry["stack", ci] = _head_stack(u[sub(ci), O_Q:O_Q + HK] * (GLA_DK ** -0.5) * jnp.exp(b),
                                         lane_head)
        carry["ks", ci], carry["ke", ci], carry["el", ci] = k * jnp.exp(-b), k * jnp.exp(bl - b), jnp.exp(bl)

    def gla_free_mm(ci):
        carry["att", ci] = _dot_nt(carry["stack", ci], carry["ks", ci])
        carry["kv", ci] = _gla_kv(carry["ke", ci], u[sub(ci), O_V:O_V + HV], per_head=True)
        carry["el_t", ci] = _state_decay(carry["el", ci])

    def gla_state(ci):
        s_all = carry["S"]
        carry["oi", ci] = _dot(carry["stack", ci], s_all)
        carry["S"] = jnp.concatenate(_gla_next_state(carry["el_t", ci], s_all, carry["kv", ci]), axis=0)

    def gla_out(ci):
        causal = ((lax.broadcasted_iota(jnp.int32, (GLA_HEADS * cg, cg), 0) % cg)
                  >= lax.broadcasted_iota(jnp.int32, (GLA_HEADS * cg, cg), 1))
        o = _gla_intra(carry["att", ci], causal, carry["oi", ci], u[sub(ci), O_V:O_V + HV], cg)
        for hd in range(GLA_HEADS):
            z[sub(ci), D_LRU + hd * GLA_DV:D_LRU + (hd + 1) * GLA_DV] = _gla_output(
                o[hd], u[sub(ci), O_GO + hd * GLA_DV:O_GO + (hd + 1) * GLA_DV], gn_ref[...])

    def gla_begin():
        carry["S"] = s_sc[...]

    def gla_end():
        s_sc[...] = carry["S"]

    lru = {n: functools.partial(f, hf) for hf in range(2)
           for n, f in ((f"front{hf}", lru_front), (f"gmm{hf}", lru_gate_mm), (f"mid{hf}", lru_mid),
                        (f"back{hf}", lru_back))}
    tiles = [functools.partial(in_tile, t) for t in range(D_MAIN // tile_w)]
    assert len(tiles) == 10
    prep = [functools.partial(gla_prep, ci) for ci in range(n_sub)]
    free = [functools.partial(gla_free_mm, ci) for ci in range(n_sub)]
    head = (tiles[0:3] + [lru["front0"], tiles[3], lru["front1"], lru["gmm0"], lru["gmm1"],
                          tiles[4], lru["mid0"], tiles[5], gate_tile, lru["back0"], tiles[6], lru["mid1"],
                          tiles[7], lru["back1"], tiles[8]])
    mid = [prep[0], tiles[9]] + ([prep[1], out_lru] if n_sub > 1 else [out_lru])
    for ci in range(2, n_sub):
        mid += [prep[ci], free[ci - 2]]
    mid += free[max(n_sub - 2, 0):]
    order = (head + mid + [gla_begin] + [functools.partial(gla_state, ci) for ci in range(n_sub)]
             + [gla_end] + [functools.partial(gla_out, ci) for ci in range(n_sub)] + [out_gla])
    for piece in order:
        piece()

    for hf in range(2):
        cs = slice(hf * half, (hf + 1) * half)
        hl_ref[:, cs] = carry["h", hf]
        tailo_ref[:, cs] = carry["tail", hf]
    so_ref[...] = s_sc[...].reshape(GLA_HEADS, GLA_DK, GLA_DV)


def _mixer_seq_call(x, frame0, h0, s0, mix, l, *, nb, sl, tc, cg, row_block0, reset_first,
                    shared_state, name):
    assert sl % tc == 0 and tc % cg == 0 and tc % SUBLANES == 0
    nj = sl // tc
    xrow = pl.BlockSpec((tc, D_MODEL), lambda s: (row_block0 + s, 0))
    seq = lambda s: s // nj
    if shared_state:
        st = lambda s: (0, 0, 0)
        st4 = lambda s: (0, 0, 0, 0)
    else:
        st = lambda s: (seq(s), 0, 0)
        st4 = lambda s: (seq(s), 0, 0, 0)
    (vecs, w_in, wlr, wg2, convw, wg, wout) = mix
    wmain_spec = pl.BlockSpec((None, D_MODEL, D_MAIN), lambda s: (l, 0, 0), pipeline_mode=pl.Buffered(1))
    args = [x, frame0, h0, s0, vecs, w_in, wlr, wg2, convw, wg, wout]
    in_specs = [
        xrow,
        pl.BlockSpec((None, SUBLANES, D_LRU), st),
        pl.BlockSpec((None, 1, D_LRU), st),
        pl.BlockSpec((None, GLA_HEADS, GLA_DK, GLA_DV), st4),
        _layer_spec(vecs, l), wmain_spec, _layer_spec(wlr, l), _layer_spec(wg2, l), _layer_spec(convw, l),
        _layer_spec(wg, l), _layer_spec(wout, l),
    ]
    out_shape = [
        jax.ShapeDtypeStruct((nb * sl, D_MODEL), F32),
        jax.ShapeDtypeStruct((nb, 1, D_LRU), F32),
        jax.ShapeDtypeStruct((nb, GLA_HEADS, GLA_DK, GLA_DV), F32),
        jax.ShapeDtypeStruct((nb, SUBLANES, D_LRU), F32),
    ]
    out_specs = [
        pl.BlockSpec((tc, D_MODEL), lambda s: (s, 0)),
        pl.BlockSpec((None, 1, D_LRU), lambda s: (seq(s), 0, 0)),
        pl.BlockSpec((None, GLA_HEADS, GLA_DK, GLA_DV), lambda s: (seq(s), 0, 0, 0)),
        pl.BlockSpec((None, SUBLANES, D_LRU), lambda s: (seq(s), 0, 0)),
    ]
    scratch = [
        pltpu.VMEM((tc, D_U), F32),
        pltpu.VMEM((tc, D_MODEL), F32),
        pltpu.VMEM((SUBLANES, D_LRU), F32),
        pltpu.VMEM((1, D_LRU), F32),
        pltpu.VMEM((HK, GLA_DV), F32),
    ]
    kern = functools.partial(_mixer_seq_kernel, tc=tc, cg=cg, nj=nj, reset_first=reset_first)
    return pl.pallas_call(
        kern, grid=(nb * nj,), in_specs=in_specs, out_specs=out_specs, out_shape=out_shape,
        scratch_shapes=scratch,
        compiler_params=pltpu.CompilerParams(dimension_semantics=("arbitrary",),
                                             vmem_limit_bytes=VMEM_LIMIT),
        name=name,
    )(*args)


def _mixer_dec_kernel(*refs, nb, n_prev):
    (x_ref, frame_ref, h0_ref, s0_ref, vecs_ref, wmain_ref, wlr_ref, wg2_ref, convw_ref, wg_ref,
     wout_ref) = refs[:11]
    gmix_ref, bg_ref, convb_ref, ba_ref, bx_ref, lam_ref, gn_ref = _mixer_vectors(vecs_ref)
    prev_refs = refs[11:11 + n_prev]
    (xo_ref, h_ref, xl_ref, so_ref,
     u_sc, z_sc, qs_sc, ks_sc, ke_sc, el_sc, o_sc) = refs[11 + n_prev:]
    for j, prev in enumerate(prev_refs):
        so_ref[j] = prev[...]
    so_new = so_ref.at[n_prev] if n_prev else so_ref
    ls = SUBLANES
    rows = nb * ls
    x = x_ref[...]
    _in_projection(x, gmix_ref, wmain_ref, wlr_ref, wg2_ref, bg_ref, u_sc)

    t = lax.broadcasted_iota(jnp.int32, (rows, D_LRU), 0) % ls
    xl = u_sc[:, 0:D_LRU]
    xl_ref[...] = xl
    frame = frame_ref[...]
    xc = convb_ref[...]
    for s in range(CONV_W - 1, -1, -1):
        if s == 0:
            sh = xl
        else:
            sh = jnp.where(t >= s, pltpu.roll(xl, s, 0), pltpu.roll(frame, (rows - ls + s) % rows, 0))
        xc = xc + convw_ref[CONV_W - 1 - s:CONV_W - s, :] * sh
    half = D_LRU // 2
    parts = []
    for hf in range(2):
        cs = slice(hf * half, (hf + 1) * half)
        parts.append(_lru_gates(xc[:, cs], _dot(xc[:, cs], wg_ref[hf]), ba_ref[:, cs], bx_ref[:, cs],
                                lam_ref[:, cs]))
    a, mult, ix = (jnp.concatenate([p[n] for p in parts], axis=1) for n in range(3))
    a3, b3 = _scan_groups(a, mult * ix)
    h = (a3 * h0_ref[...].reshape(a3.shape) + b3).reshape(rows, D_LRU)
    h_ref[...] = h
    z_sc[:, 0:D_LRU] = h * jax.nn.gelu(u_sc[:, O_GL:O_GL + D_LRU])

    tq = lax.broadcasted_iota(jnp.int32, (rows, HK), 0) % ls
    b = _cumsum_rows(u_sc[:, D_MAIN:D_U], tq, ls)
    bl = jnp.where(tq == ls - 1, b, 0.0)
    s = 1
    while s < ls:
        bl = bl + jnp.where(tq + s < ls, pltpu.roll(bl, rows - s, 0), 0.0)
        s *= 2
    k = u_sc[:, O_K:O_K + HK]
    qs_sc[...] = (u_sc[:, O_Q:O_Q + HK] * (GLA_DK ** -0.5)) * jnp.exp(b)
    ks_sc[...] = k * jnp.exp(-b)
    ke_sc[...] = k * jnp.exp(bl - b)
    el_sc[...] = jnp.exp(bl)

    lane_head = lax.broadcasted_iota(jnp.int32, (ls, HK), 1) // GLA_DK
    causal = ((lax.broadcasted_iota(jnp.int32, (GLA_HEADS * ls, ls), 0) % ls)
              >= lax.broadcasted_iota(jnp.int32, (GLA_HEADS * ls, ls), 1))

    def body(bi, carry):
        rs = pl.ds(pl.multiple_of(bi * ls, ls), ls)
        s_all = s0_ref[bi].reshape(HK, GLA_DV)
        o, s_new = _gla_chunk(qs_sc[rs, :], ks_sc[rs, :], ke_sc[rs, :], el_sc[rs, :][0:1],
                              u_sc[rs, O_V:O_V + HV], s_all, causal, lane_head, ls)
        for hd in range(GLA_HEADS):
            so_new[bi, hd] = s_new[hd]
            o_sc[rs, hd * GLA_DV:(hd + 1) * GLA_DV] = o[hd]
        return carry

    lax.fori_loop(0, nb, body, 0, unroll=4)

    gn = gn_ref[...]
    for hd in range(GLA_HEADS):
        vs = slice(hd * GLA_DV, (hd + 1) * GLA_DV)
        z_sc[:, D_LRU + hd * GLA_DV:D_LRU + (hd + 1) * GLA_DV] = _gla_output(
            o_sc[:, vs], u_sc[:, O_GO + hd * GLA_DV:O_GO + (hd + 1) * GLA_DV], gn)
    xo_ref[...] = x + _dot(z_sc[...], wout_ref[...])


def _mixer_dec_call(x, frame, h0, s0, s_prev, mix, l, *, nseq, nb, name):
    ls = SUBLANES
    assert nseq % nb == 0
    rows = nb * ls
    (vecs, w_in, wlr, wg2, convw, wg, wout) = mix
    rowspec = lambda w: pl.BlockSpec((rows, w), lambda i: (i, 0))
    sspec = pl.BlockSpec((None, nb, GLA_HEADS, GLA_DK, GLA_DV), lambda i: (l, i, 0, 0, 0))
    wmain_spec = pl.BlockSpec((None, D_MODEL, D_MAIN), lambda i: (l, 0, 0), pipeline_mode=pl.Buffered(1))
    in_specs = [
        rowspec(D_MODEL), rowspec(D_LRU), rowspec(D_LRU), sspec,
        _layer_spec(vecs, l), wmain_spec, _layer_spec(wlr, l), _layer_spec(wg2, l), _layer_spec(convw, l),
        _layer_spec(wg, l), _layer_spec(wout, l),
    ]
    out_shape = [
        jax.ShapeDtypeStruct((nseq * ls, D_MODEL), F32),
        jax.ShapeDtypeStruct((nseq * ls, D_LRU), F32),
        jax.ShapeDtypeStruct((nseq * ls, D_LRU), F32),
    ]
    state = (nseq, GLA_HEADS, GLA_DK, GLA_DV)
    sblock = pl.BlockSpec((nb,) + state[1:], lambda i: (i, 0, 0, 0))
    n_prev = len(s_prev)
    if n_prev:
        out_shape.append(jax.ShapeDtypeStruct((n_prev + 1,) + state, F32))
        s_out_spec = pl.BlockSpec((n_prev + 1, nb) + state[1:], lambda i: (0, i, 0, 0, 0))
    else:
        out_shape.append(jax.ShapeDtypeStruct(state, F32))
        s_out_spec = sblock
    out_specs = [rowspec(D_MODEL), rowspec(D_LRU), rowspec(D_LRU), s_out_spec]
    args = [x, frame, h0, s0, vecs, w_in, wlr, wg2, convw, wg, wout]
    args += list(s_prev)
    in_specs += [sblock] * n_prev
    scratch = [
        pltpu.VMEM((rows, D_U), F32), pltpu.VMEM((rows, D_MODEL), F32),
        pltpu.VMEM((rows, HK), F32), pltpu.VMEM((rows, HK), F32), pltpu.VMEM((rows, HK), F32),
        pltpu.VMEM((rows, HK), F32), pltpu.VMEM((rows, HV), F32),
    ]
    return pl.pallas_call(
        functools.partial(_mixer_dec_kernel, nb=nb, n_prev=n_prev),
        grid=(nseq // nb,), in_specs=in_specs, out_specs=out_specs, out_shape=out_shape,
        scratch_shapes=scratch,
        compiler_params=pltpu.CompilerParams(dimension_semantics=("arbitrary",),
                                             vmem_limit_bytes=VMEM_LIMIT),
        name=name,
    )(*args)


def _block_diag_gates(wa, wx):
    per = LRU_BLOCKS // 2
    bw = wa.shape[-1]
    eye = jnp.eye(per, dtype=wa.dtype)

    def bd(w):
        return jnp.einsum("lncd,nm->lncmd", w, eye).reshape(w.shape[0], per * bw, per * bw)

    halves = [jnp.concatenate([bd(wa[:, hf * per:(hf + 1) * per]), bd(wx[:, hf * per:(hf + 1) * per])],
                              axis=-1) for hf in range(2)]
    return jnp.stack(halves, axis=1)


def kernel(x_prompt, x_sample, state_lru_h, state_lru_conv, state_gla_S, meta, norm_ffn1, w_ffn1_gu,
           w_ffn1_down, norm_mix, w_in, lru_conv_w, lru_conv_b, lru_wa, lru_ba, lru_wx, lru_bx,
           lru_lambda, gla_w_gate2, gla_b_gate, gla_norm, w_out, norm_ffn2, w_ffn2_gu, w_ffn2_down,
           norm_final):
    bp, lp, d = x_prompt.shape
    bs, ls, _ = x_sample.shape
    n_meta = meta.shape[0]
    depth = w_in.shape[0]
    assert d == D_MODEL and ls == SUBLANES and n_meta % SUBLANES == 0
    tp, ts = bp * lp, bs * ls
    tsm = ts + n_meta
    assert ts % n_meta == 0
    tm_p = 512 if tp % 512 == 0 else lp
    tm_s = tsm // 2 if (tsm // 2) % SUBLANES == 0 else tsm
    tc_p = 512 if lp % 512 == 0 else GLA_CHUNK
    nb_s = 32 if bs % 32 == 0 else bs
    tail = CONV_W - 1

    vecs = _pack_vectors(norm_mix, lru_ba, lru_bx, lru_lambda, lru_conv_b, gla_b_gate, gla_norm,
                         norm_ffn1, norm_ffn2, norm_final)
    mix = (vecs, w_in.astype(BF16),
           jnp.pad(w_in[:, :, D_MAIN:], ((0, 0), (0, 0), (0, LANES - GLA_RANK))).astype(BF16),
           jnp.pad(gla_w_gate2, ((0, 0), (0, LANES - GLA_RANK), (0, 0))).astype(BF16),
           lru_conv_w, _block_diag_gates(lru_wa, lru_wx).astype(BF16), w_out.astype(BF16))
    frames_s = jnp.pad(state_lru_conv, ((0, 0), (0, 0), (SUBLANES - tail, 0), (0, 0)))
    frames_s = frames_s.reshape(depth, ts, D_LRU)
    h0_s = jnp.broadcast_to(state_lru_h[:, :, None, :], (depth, bs, ls, D_LRU)).reshape(depth, ts, D_LRU)
    zero_frame = jnp.zeros((1, SUBLANES, D_LRU), F32)
    zero_h = jnp.zeros((1, 1, D_LRU), F32)
    zero_s = jnp.zeros((1, GLA_HEADS, GLA_DK, GLA_DV), F32)

    xp = x_prompt.reshape(tp, d)
    xs = jnp.concatenate([x_sample.reshape(ts, d), meta.astype(F32)], axis=0)

    hs_p, convs_p, ss_p, hs_s, convs_s, ss_s = [], [], [], [], [], []
    for l in range(depth):
        xp1, xs1 = _ffn_call(xp, xs, vecs, ROW_FFN1, w_ffn1_gu, w_ffn1_down, l, final=False, tm_a=tm_p,
                             tm_b=tm_s, tf=256, name=f"ffn1_{l}")

        last = l == depth - 1
        xs2, h_s, xl_s, s_s = _mixer_dec_call(xs1, frames_s[l], h0_s[l], state_gla_S,
                                              ss_s if last else (), mix, l, nseq=bs, nb=nb_s,
                                              name=f"mix_s{l}")
        ss_s.append(s_s)
        hs_s.append(h_s.reshape(bs, ls, D_LRU)[:, ls - 1])
        convs_s.append(xl_s.reshape(bs, ls, D_LRU)[:, ls - tail:])

        xm2, h_m, s_m, tail_m = _mixer_seq_call(
            xs1, zero_frame, zero_h, zero_s, mix, l, nb=1, sl=n_meta, tc=n_meta, cg=n_meta,
            row_block0=ts // n_meta, reset_first=True, shared_state=False, name=f"mix_m{l}")
        xs2 = jnp.concatenate([xs2, xm2], axis=0)

        xp2, h_p, s_p, tail_p = _mixer_seq_call(
            xp1, tail_m, h_m, s_m, mix, l, nb=bp, sl=lp, tc=tc_p, cg=GLA_CHUNK, row_block0=0,
            reset_first=False, shared_state=True, name=f"mix_p{l}")
        hs_p.append(h_p[:, 0])
        convs_p.append(tail_p[:, SUBLANES - tail:])
        ss_p.append(s_p)

        xp, xs = _ffn_call(xp2, xs2, vecs, ROW_FFN2, w_ffn2_gu, w_ffn2_down, l, final=last, tm_a=tm_p,
                           tm_b=tm_s, tf=256, name=f"ffn2_{l}")

    s_stack = ss_s[-1] if depth > 1 else ss_s[0][None]
    return (xp.reshape(bp, lp, d), xs[:ts].reshape(bs, ls, d),
            jnp.stack(hs_p), jnp.stack(convs_p), jnp.stack(ss_p),
            jnp.stack(hs_s), jnp.stack(convs_s), s_stack)
```

```python
import functools

import jax
import jax.numpy as jnp
from jax import lax
from jax.experimental import pallas as pl
from jax.experimental.pallas import tpu as pltpu

F32 = jnp.float32
BF16 = jnp.bfloat16

D_MODEL = 1024
D_FF = 2816
D_LRU = 512
LRU_BLOCKS = 8
CONV_W = 4
LRU_C = 8.0
GLA_HEADS = 4
GLA_DV = 128
GLA_DK = 64
GLA_RANK = 16
GLA_GATE_NORM = 16.0
GLA_CHUNK = 64
EPS = 1e-6
HK = GLA_HEADS * GLA_DK
HV = GLA_HEADS * GLA_DV
O_GL, O_Q, O_K, O_V, O_GO = D_LRU, 2 * D_LRU, 2 * D_LRU + HK, 2 * D_LRU + 2 * HK, 2 * D_LRU + 2 * HK + HV
D_MAIN = O_GO + HV
D_U = D_MAIN + HK
SUBLANES = 8
LANES = 128
VMEM_LIMIT = 58 * 1024 * 1024
ROW_MIX, ROW_GATE_B, ROW_LAM_CONVB, ROW_GLA, ROW_FFN1, ROW_FFN2, ROW_FINAL = range(7)


def _dot(a, b):
    if a.dtype != b.dtype:
        a = a.astype(b.dtype)
    return jnp.dot(a, b, preferred_element_type=F32)


def _dot_nt(a, b):
    return lax.dot_general(a, b, (((1,), (1,)), ((), ())), preferred_element_type=F32)


def _dot_tn(a, b):
    return lax.dot_general(a, b, (((0,), (0,)), ((), ())), preferred_element_type=F32)


def _rms(x, g):
    ms = jnp.mean(x * x, axis=-1, keepdims=True)
    return (x * lax.rsqrt(ms + EPS)) * g


def _sigmoid(x):
    return 0.5 * jnp.tanh(0.5 * x) + 0.5


def _silu(x):
    return x * _sigmoid(x)


def _log_sigmoid(x):
    return jnp.minimum(x, 0.0) - jnp.log(1.0 + jnp.exp(-jnp.abs(x)))


def _pack_vectors(norm_mix, ba, bx, lam, conv_b, b_gate, gla_norm, norm_ffn1, norm_ffn2, norm_final):
    depth, d = norm_mix.shape
    rows = [norm_mix, jnp.concatenate([ba, bx], axis=1), jnp.concatenate([lam, conv_b], axis=1),
            jnp.pad(jnp.concatenate([b_gate, gla_norm], axis=1), ((0, 0), (0, d - HK - GLA_DV))),
            norm_ffn1, norm_ffn2, jnp.broadcast_to(norm_final[None, :], (depth, d)),
            jnp.zeros((depth, d), F32)]
    return jnp.stack(rows, axis=1)


def _mixer_vectors(vecs_ref):
    row = lambda r, a, b: vecs_ref.at[r:r + 1, a:b]
    return (row(ROW_MIX, 0, D_MODEL), row(ROW_GLA, 0, HK), row(ROW_LAM_CONVB, D_LRU, 2 * D_LRU),
            row(ROW_GATE_B, 0, D_LRU), row(ROW_GATE_B, D_LRU, 2 * D_LRU), row(ROW_LAM_CONVB, 0, D_LRU),
            row(ROW_GLA, HK, HK + GLA_DV))


def _layer_spec(a, l):
    nd = a.ndim
    return pl.BlockSpec((None,) + a.shape[1:], lambda *_: (l,) + (0,) * (nd - 1),
                        pipeline_mode=pl.Buffered(1))


def _ffn_kernel(xa_ref, xb_ref, vecs_ref, wgu_ref, wd_ref, oa_ref, ob_ref, *, tf, norm_row, has_final, n_a):
    g_ref = vecs_ref.at[norm_row:norm_row + 1, :]
    gfin_ref = vecs_ref.at[ROW_FINAL:ROW_FINAL + 1, :]

    def tile(x_ref, xo_ref):
        x = x_ref[...]
        xn = _rms(x, g_ref[...])
        acc = None
        for c in range(D_FF // tf):
            gate = _dot(xn, wgu_ref[:, c * tf:(c + 1) * tf])
            up = _dot(xn, wgu_ref[:, D_FF + c * tf:D_FF + (c + 1) * tf])
            d = _dot(_silu(gate) * up, wd_ref[c * tf:(c + 1) * tf, :])
            acc = d if acc is None else acc + d
        x = x + 0.5 * acc
        xo_ref[...] = _rms(x, gfin_ref[...]) if has_final else x

    i = pl.program_id(0)

    @pl.when(i < n_a)
    def _():
        tile(xa_ref, oa_ref)

    @pl.when(i >= n_a)
    def _():
        tile(xb_ref, ob_ref)


def _ffn_call(xa, xb, vecs, norm_row, wgu, wd, l, *, final, tm_a, tm_b, tf, name):
    (ta, d), (tb, _) = xa.shape, xb.shape
    assert ta % tm_a == 0 and tb % tm_b == 0 and D_FF % tf == 0
    n_a, n_b = ta // tm_a, tb // tm_b
    rows_a = pl.BlockSpec((tm_a, d), lambda i: (jnp.minimum(i, n_a - 1), 0))
    rows_b = pl.BlockSpec((tm_b, d), lambda i: (jnp.maximum(i - n_a, 0), 0))
    args = [xa, xb, vecs, wgu, wd]
    specs = [rows_a, rows_b, _layer_spec(vecs, l), _layer_spec(wgu, l), _layer_spec(wd, l)]
    return pl.pallas_call(
        functools.partial(_ffn_kernel, tf=tf, norm_row=norm_row, has_final=final, n_a=n_a),
        grid=(n_a + n_b,), in_specs=specs, out_specs=[rows_a, rows_b],
        out_shape=[jax.ShapeDtypeStruct((ta, d), F32), jax.ShapeDtypeStruct((tb, d), F32)],
        compiler_params=pltpu.CompilerParams(dimension_semantics=("arbitrary",),
                                             vmem_limit_bytes=VMEM_LIMIT),
        name=name,
    )(*args)


def _in_projection(x, gmix_ref, wmain_ref, wlr_ref, wg2_ref, bg_ref, u_sc):
    hn = _rms(x, gmix_ref[...]).astype(wmain_ref.dtype)
    u_sc[:, 0:D_MAIN] = _dot(hn, wmain_ref[...])
    lr = _dot(hn, wlr_ref[...])
    zg = _dot(lr, wg2_ref[...]) + bg_ref[...]
    u_sc[:, D_MAIN:D_U] = _log_sigmoid(zg) / GLA_GATE_NORM


def _lru_gates(xc, p, ba, bx, lam):
    half = xc.shape[1]
    r = _sigmoid(p[:, 0:half] + ba)
    i = _sigmoid(p[:, half:] + bx)
    log_a = -LRU_C * r * jax.nn.softplus(-lam)
    a = jnp.exp(log_a)
    y = -jnp.tanh(log_a) * (a * a + 1.0)
    mult = jnp.where(y > 0.0, y * lax.rsqrt(y), 0.0)
    return a, mult, i * xc


def _scan_groups(a, b):
    rows, cols = a.shape
    a3 = a.reshape(rows // SUBLANES, SUBLANES, cols)
    b3 = b.reshape(rows // SUBLANES, SUBLANES, cols)
    t = lax.broadcasted_iota(jnp.int32, a3.shape, 1)
    s = 1
    while s < SUBLANES:
        keep = t >= s
        b3 = jnp.where(keep, a3 * pltpu.roll(b3, s, 1) + b3, b3)
        a3 = jnp.where(keep, a3 * pltpu.roll(a3, s, 1), a3)
        s *= 2
    return a3, b3


def _scan_rows(a, b, h0):
    a3, b3 = _scan_groups(a, b)
    carry, out = h0, []
    for g in range(a3.shape[0]):
        hg = a3[g] * carry + b3[g]
        out.append(hg)
        carry = hg[SUBLANES - 1:SUBLANES]
    return out[0] if len(out) == 1 else jnp.concatenate(out, axis=0)


def _cumsum_rows(x, t, n):
    s = 1
    while s < n:
        x = x + jnp.where(t >= s, pltpu.roll(x, s, 0), 0.0)
        s *= 2
    return x


def _head_stack(q_s, lane_head):
    return jnp.concatenate([jnp.where(lane_head == hd, q_s, 0.0) for hd in range(GLA_HEADS)], axis=0)


def _gla_kv(k_end, v, per_head):
    if per_head:
        return [_dot_tn(k_end[:, hd * GLA_DK:(hd + 1) * GLA_DK], v[:, hd * GLA_DV:(hd + 1) * GLA_DV])
                for hd in range(GLA_HEADS)]
    kv = _dot_tn(k_end, v)
    return [kv[hd * GLA_DK:(hd + 1) * GLA_DK, hd * GLA_DV:(hd + 1) * GLA_DV] for hd in range(GLA_HEADS)]


def _state_decay(el):
    return jnp.broadcast_to(el, (LANES, HK)).T


def _gla_next_state(el_t, s_all, kv):
    return [el_t[hd * GLA_DK:(hd + 1) * GLA_DK] * s_all[hd * GLA_DK:(hd + 1) * GLA_DK] + kv[hd]
            for hd in range(GLA_HEADS)]


def _gla_intra(att_raw, causal, o_inter, v, n):
    att = jnp.where(causal, att_raw, 0.0)
    return [o_inter[hd * n:(hd + 1) * n] + _dot(att[hd * n:(hd + 1) * n], v[:, hd * GLA_DV:(hd + 1) * GLA_DV])
            for hd in range(GLA_HEADS)]


def _gla_chunk(q_s, k_s, k_end, el, v, s_all, causal, lane_head, n):
    stack = _head_stack(q_s, lane_head)
    o_inter = _dot(stack, s_all)
    att_raw = _dot_nt(stack, k_s)
    kv = _gla_kv(k_end, v, per_head=False)
    el_t = _state_decay(el)
    return _gla_intra(att_raw, causal, o_inter, v, n), _gla_next_state(el_t, s_all, kv)


def _gla_output(o, go, gn):
    return _rms(o, gn) * _silu(go)


def _mixer_seq_kernel(*refs, tc, cg, nj, reset_first):
    (xin_ref, frame0_ref, h0_ref, s0_ref, vecs_ref, wmain_ref, wlr_ref, wg2_ref, convw_ref, wg_ref,
     wout_ref) = refs[:11]
    gmix_ref, bg_ref, convb_ref, ba_ref, bx_ref, lam_ref, gn_ref = _mixer_vectors(vecs_ref)
    xo_ref, hl_ref, so_ref, tailo_ref, u, z, tail_sc, h_sc, s_sc = refs[11:]
    c = pl.program_id(0)

    @pl.when(c % nj == 0)
    def _():
        tail_sc[...] = frame0_ref[...]
        h_sc[...] = h0_ref[...]
        s_sc[...] = s0_ref[...].reshape(HK, GLA_DV)

    x = xin_ref[...]
    hn = _rms(x, gmix_ref[...]).astype(wmain_ref.dtype)
    tile_w = 2 * LANES
    carry = {}
    half = D_LRU // 2

    def in_tile(t):
        cols = slice(t * tile_w, (t + 1) * tile_w)
        u[:, cols] = _dot(hn, wmain_ref[:, cols])

    def gate_tile():
        lr = _dot(hn, wlr_ref[...])
        zg = _dot(lr, wg2_ref[...]) + bg_ref[...]
        u[:, D_MAIN:D_U] = _log_sigmoid(zg) / GLA_GATE_NORM

    def out_lru():
        xo_ref[...] = x + _dot(z[:, 0:D_LRU], wout_ref[0:D_LRU, :])

    def out_gla():
        xo_ref[...] += _dot(z[:, D_LRU:D_MODEL], wout_ref[D_LRU:D_MODEL, :])

    def lru_front(hf):
        cs = slice(hf * half, (hf + 1) * half)
        xl = u[:, cs]
        row8 = lax.broadcasted_iota(jnp.int32, (SUBLANES, half), 0)
        xc = convb_ref[:, cs]
        for sft in range(CONV_W - 1, -1, -1):
            if sft == 0:
                sh = xl
            else:
                rolled = pltpu.roll(xl, sft, 0)
                top = jnp.where(row8 < sft, pltpu.roll(tail_sc[:, cs], sft, 0), rolled[0:SUBLANES])
                sh = top if tc == SUBLANES else jnp.concatenate([top, rolled[SUBLANES:]], axis=0)
            xc = xc + convw_ref[CONV_W - 1 - sft:CONV_W - sft, cs] * sh
        carry["tail", hf] = xl[tc - SUBLANES:tc]
        tail_sc[:, cs] = carry["tail", hf]
        carry["xc", hf] = xc

    def lru_gate_mm(hf):
        carry["p", hf] = _dot(carry["xc", hf], wg_ref[hf])

    def lru_mid(hf):
        cs = slice(hf * half, (hf + 1) * half)
        a, mult, ix = _lru_gates(carry["xc", hf], carry["p", hf], ba_ref[:, cs], bx_ref[:, cs],
                                 lam_ref[:, cs])
        if reset_first:
            row = lax.broadcasted_iota(jnp.int32, (tc, half), 0)
            first = jnp.logical_and(row == 0, c % nj == 0)
            mult = jnp.where(first, 1.0, mult)
            a = jnp.where(first, 0.0, a)
        carry["a", hf], carry["b", hf] = a, mult * ix

    def lru_back(hf):
        cs = slice(hf * half, (hf + 1) * half)
        h = _scan_rows(carry["a", hf], carry["b", hf], h_sc[:, cs])
        carry["h", hf] = h[tc - 1:tc]
        h_sc[:, cs] = carry["h", hf]
        z[:, cs] = h * jax.nn.gelu(u[:, O_GL + hf * half:O_GL + (hf + 1) * half])

    lane_head = lax.broadcasted_iota(jnp.int32, (cg, HK), 1) // GLA_DK
    n_sub = tc // cg
    sub = lambda ci: slice(ci * cg, (ci + 1) * cg)

    def gla_prep(ci):
        tq = lax.broadcasted_iota(jnp.int32, (cg, HK), 0)
        b = _cumsum_rows(u[sub(ci), D_MAIN:D_U], tq, cg)
        bl = b[cg - 1:cg]
        k = u[sub(ci), O_K:O_K + HK]
        carry["stack", ci] = _head_stack(u[sub(ci), O_Q:O_Q + HK] * (GLA_DK ** -0.5) * jnp.exp(b),
                                         lane_head)
        carry["ks", ci], carry["ke", ci], carry["el", ci] = k * jnp.exp(-b), k * jnp.exp(bl - b), jnp.exp(bl)

    def gla_free_mm(ci):
        carry["att", ci] = _dot_nt(carry["stack", ci], carry["ks", ci])
        carry["kv", ci] = _gla_kv(carry["ke", ci], u[sub(ci), O_V:O_V + HV], per_head=True)
        carry["el_t", ci] = _state_decay(carry["el", ci])

    def gla_state(ci):
        s_all = carry["S"]
        carry["oi", ci] = _dot(carry["stack", ci], s_all)
        carry["S"] = jnp.concatenate(_gla_next_state(carry["el_t", ci], s_all, carry["kv", ci]), axis=0)

    def gla_out(ci):
        causal = ((lax.broadcasted_iota(jnp.int32, (GLA_HEADS * cg, cg), 0) % cg)
                  >= lax.broadcasted_iota(jnp.int32, (GLA_HEADS * cg, cg), 1))
        o = _gla_intra(carry["att", ci], causal, carry["oi", ci], u[sub(ci), O_V:O_V + HV], cg)
        for hd in range(GLA_HEADS):
            z[sub(ci), D_LRU + hd * GLA_DV:D_LRU + (hd + 1) * GLA_DV] = _gla_output(
                o[hd], u[sub(ci), O_GO + hd * GLA_DV:O_GO + (hd + 1) * GLA_DV], gn_ref[...])

    def gla_begin():
        carry["S"] = s_sc[...]

    def gla_end():
        s_sc[...] = carry["S"]

    lru = {n: functools.partial(f, hf) for hf in range(2)
           for n, f in ((f"front{hf}", lru_front), (f"gmm{hf}", lru_gate_mm), (f"mid{hf}", lru_mid),
                        (f"back{hf}", lru_back))}
    tiles = [functools.partial(in_tile, t) for t in range(D_MAIN // tile_w)]
    assert len(tiles) == 10
    prep = [functools.partial(gla_prep, ci) for ci in range(n_sub)]
    free = [functools.partial(gla_free_mm, ci) for ci in range(n_sub)]
    head = (tiles[0:3] + [lru["front0"], tiles[3], lru["front1"], lru["gmm0"], lru["gmm1"],
                          tiles[4], lru["mid0"], tiles[5], gate_tile, lru["back0"], tiles[6], lru["mid1"],
                          tiles[7], lru["back1"], tiles[8]])
    mid = [prep[0], tiles[9]] + ([prep[1], out_lru] if n_sub > 1 else [out_lru])
    for ci in range(2, n_sub):
        mid += [prep[ci], free[ci - 2]]
    mid += free[max(n_sub - 2, 0):]
    order = (head + mid + [gla_begin] + [functools.partial(gla_state, ci) for ci in range(n_sub)]
             + [gla_end] + [functools.partial(gla_out, ci) for ci in range(n_sub)] + [out_gla])
    for piece in order:
        piece()

    for hf in range(2):
        cs = slice(hf * half, (hf + 1) * half)
        hl_ref[:, cs] = carry["h", hf]
        tailo_ref[:, cs] = carry["tail", hf]
    so_ref[...] = s_sc[...].reshape(GLA_HEADS, GLA_DK, GLA_DV)


def _mixer_seq_call(x, frame0, h0, s0, mix, l, *, nb, sl, tc, cg, row_block0, reset_first,
                    shared_state, name):
    assert sl % tc == 0 and tc % cg == 0 and tc % SUBLANES == 0
    nj = sl // tc
    xrow = pl.BlockSpec((tc, D_MODEL), lambda s: (row_block0 + s, 0))
    seq = lambda s: s // nj
    if shared_state:
        st = lambda s: (0, 0, 0)
        st4 = lambda s: (0, 0, 0, 0)
    else:
        st = lambda s: (seq(s), 0, 0)
        st4 = lambda s: (seq(s), 0, 0, 0)
    (vecs, w_in, wlr, wg2, convw, wg, wout) = mix
    wmain_spec = pl.BlockSpec((None, D_MODEL, D_MAIN), lambda s: (l, 0, 0), pipeline_mode=pl.Buffered(1))
    args = [x, frame0, h0, s0, vecs, w_in, wlr, wg2, convw, wg, wout]
    in_specs = [
        xrow,
        pl.BlockSpec((None, SUBLANES, D_LRU), st),
        pl.BlockSpec((None, 1, D_LRU), st),
        pl.BlockSpec((None, GLA_HEADS, GLA_DK, GLA_DV), st4),
        _layer_spec(vecs, l), wmain_spec, _layer_spec(wlr, l), _layer_spec(wg2, l), _layer_spec(convw, l),
        _layer_spec(wg, l), _layer_spec(wout, l),
    ]
    out_shape = [
        jax.ShapeDtypeStruct((nb * sl, D_MODEL), F32),
        jax.ShapeDtypeStruct((nb, 1, D_LRU), F32),
        jax.ShapeDtypeStruct((nb, GLA_HEADS, GLA_DK, GLA_DV), F32),
        jax.ShapeDtypeStruct((nb, SUBLANES, D_LRU), F32),
    ]
    out_specs = [
        pl.BlockSpec((tc, D_MODEL), lambda s: (s, 0)),
        pl.BlockSpec((None, 1, D_LRU), lambda s: (seq(s), 0, 0)),
        pl.BlockSpec((None, GLA_HEADS, GLA_DK, GLA_DV), lambda s: (seq(s), 0, 0, 0)),
        pl.BlockSpec((None, SUBLANES, D_LRU), lambda s: (seq(s), 0, 0)),
    ]
    scratch = [
        pltpu.VMEM((tc, D_U), F32),
        pltpu.VMEM((tc, D_MODEL), F32),
        pltpu.VMEM((SUBLANES, D_LRU), F32),
        pltpu.VMEM((1, D_LRU), F32),
        pltpu.VMEM((HK, GLA_DV), F32),
    ]
    kern = functools.partial(_mixer_seq_kernel, tc=tc, cg=cg, nj=nj, reset_first=reset_first)
    return pl.pallas_call(
        kern, grid=(nb * nj,), in_specs=in_specs, out_specs=out_specs, out_shape=out_shape,
        scratch_shapes=scratch,
        compiler_params=pltpu.CompilerParams(dimension_semantics=("arbitrary",),
                                             vmem_limit_bytes=VMEM_LIMIT),
        name=name,
    )(*args)


def _mixer_dec_kernel(*refs, nb, n_prev):
    (x_ref, frame_ref, h0_ref, s0_ref, vecs_ref, wmain_ref, wlr_ref, wg2_ref, convw_ref, wg_ref,
     wout_ref) = refs[:11]
    gmix_ref, bg_ref, convb_ref, ba_ref, bx_ref, lam_ref, gn_ref = _mixer_vectors(vecs_ref)
    prev_refs = refs[11:11 + n_prev]
    (xo_ref, h_ref, xl_ref, so_ref,
     u_sc, z_sc, qs_sc, ks_sc, ke_sc, el_sc, o_sc) = refs[11 + n_prev:]
    for j, prev in enumerate(prev_refs):
        so_ref[j] = prev[...]
    so_new = so_ref.at[n_prev] if n_prev else so_ref
    ls = SUBLANES
    rows = nb * ls
    x = x_ref[...]
    _in_projection(x, gmix_ref, wmain_ref, wlr_ref, wg2_ref, bg_ref, u_sc)

    t = lax.broadcasted_iota(jnp.int32, (rows, D_LRU), 0) % ls
    xl = u_sc[:, 0:D_LRU]
    xl_ref[...] = xl
    frame = frame_ref[...]
    xc = convb_ref[...]
    for s in range(CONV_W - 1, -1, -1):
        if s == 0:
            sh = xl
        else:
            sh = jnp.where(t >= s, pltpu.roll(xl, s, 0), pltpu.roll(frame, (rows - ls + s) % rows, 0))
        xc = xc + convw_ref[CONV_W - 1 - s:CONV_W - s, :] * sh
    half = D_LRU // 2
    parts = []
    for hf in range(2):
        cs = slice(hf * half, (hf + 1) * half)
        parts.append(_lru_gates(xc[:, cs], _dot(xc[:, cs], wg_ref[hf]), ba_ref[:, cs], bx_ref[:, cs],
                                lam_ref[:, cs]))
    a, mult, ix = (jnp.concatenate([p[n] for p in parts], axis=1) for n in range(3))
    a3, b3 = _scan_groups(a, mult * ix)
    h = (a3 * h0_ref[...].reshape(a3.shape) + b3).reshape(rows, D_LRU)
    h_ref[...] = h
    z_sc[:, 0:D_LRU] = h * jax.nn.gelu(u_sc[:, O_GL:O_GL + D_LRU])

    tq = lax.broadcasted_iota(jnp.int32, (rows, HK), 0) % ls
    b = _cumsum_rows(u_sc[:, D_MAIN:D_U], tq, ls)
    bl = jnp.where(tq == ls - 1, b, 0.0)
    s = 1
    while s < ls:
        bl = bl + jnp.where(tq + s < ls, pltpu.roll(bl, rows - s, 0), 0.0)
        s *= 2
    k = u_sc[:, O_K:O_K + HK]
    qs_sc[...] = (u_sc[:, O_Q:O_Q + HK] * (GLA_DK ** -0.5)) * jnp.exp(b)
    ks_sc[...] = k * jnp.exp(-b)
    ke_sc[...] = k * jnp.exp(bl - b)
    el_sc[...] = jnp.exp(bl)

    lane_head = lax.broadcasted_iota(jnp.int32, (ls, HK), 1) // GLA_DK
    causal = ((lax.broadcasted_iota(jnp.int32, (GLA_HEADS * ls, ls), 0) % ls)
              >= lax.broadcasted_iota(jnp.int32, (GLA_HEADS * ls, ls), 1))

    def body(bi, carry):
        rs = pl.ds(pl.multiple_of(bi * ls, ls), ls)
        s_all = s0_ref[bi].reshape(HK, GLA_DV)
        o, s_new = _gla_chunk(qs_sc[rs, :], ks_sc[rs, :], ke_sc[rs, :], el_sc[rs, :][0:1],
                              u_sc[rs, O_V:O_V + HV], s_all, causal, lane_head, ls)
        for hd in range(GLA_HEADS):
            so_new[bi, hd] = s_new[hd]
            o_sc[rs, hd * GLA_DV:(hd + 1) * GLA_DV] = o[hd]
        return carry

    lax.fori_loop(0, nb, body, 0, unroll=4)

    gn = gn_ref[...]
    for hd in range(GLA_HEADS):
        vs = slice(hd * GLA_DV, (hd + 1) * GLA_DV)
        z_sc[:, D_LRU + hd * GLA_DV:D_LRU + (hd + 1) * GLA_DV] = _gla_output(
            o_sc[:, vs], u_sc[:, O_GO + hd * GLA_DV:O_GO + (hd + 1) * GLA_DV], gn)
    xo_ref[...] = x + _dot(z_sc[...], wout_ref[...])


def _mixer_dec_call(x, frame, h0, s0, s_prev, mix, l, *, nseq, nb, name):
    ls = SUBLANES
    assert nseq % nb == 0
    rows = nb * ls
    (vecs, w_in, wlr, wg2, convw, wg, wout) = mix
    rowspec = lambda w: pl.BlockSpec((rows, w), lambda i: (i, 0))
    sspec = pl.BlockSpec((None, nb, GLA_HEADS, GLA_DK, GLA_DV), lambda i: (l, i, 0, 0, 0))
    wmain_spec = pl.BlockSpec((None, D_MODEL, D_MAIN), lambda i: (l, 0, 0), pipeline_mode=pl.Buffered(1))
    in_specs = [
        rowspec(D_MODEL), rowspec(D_LRU), rowspec(D_LRU), sspec,
        _layer_spec(vecs, l), wmain_spec, _layer_spec(wlr, l), _layer_spec(wg2, l), _layer_spec(convw, l),
        _layer_spec(wg, l), _layer_spec(wout, l),
    ]
    out_shape = [
        jax.ShapeDtypeStruct((nseq * ls, D_MODEL), F32),
        jax.ShapeDtypeStruct((nseq * ls, D_LRU), F32),
        jax.ShapeDtypeStruct((nseq * ls, D_LRU), F32),
    ]
    state = (nseq, GLA_HEADS, GLA_DK, GLA_DV)
    sblock = pl.BlockSpec((nb,) + state[1:], lambda i: (i, 0, 0, 0))
    n_prev = len(s_prev)
    if n_prev:
        out_shape.append(jax.ShapeDtypeStruct((n_prev + 1,) + state, F32))
        s_out_spec = pl.BlockSpec((n_prev + 1, nb) + state[1:], lambda i: (0, i, 0, 0, 0))
    else:
        out_shape.append(jax.ShapeDtypeStruct(state, F32))
        s_out_spec = sblock
    out_specs = [rowspec(D_MODEL), rowspec(D_LRU), rowspec(D_LRU), s_out_spec]
    args = [x, frame, h0, s0, vecs, w_in, wlr, wg2, convw, wg, wout]
    args += list(s_prev)
    in_specs += [sblock] * n_prev
    scratch = [
        pltpu.VMEM((rows, D_U), F32), pltpu.VMEM((rows, D_MODEL), F32),
        pltpu.VMEM((rows, HK), F32), pltpu.VMEM((rows, HK), F32), pltpu.VMEM((rows, HK), F32),
        pltpu.VMEM((rows, HK), F32), pltpu.VMEM((rows, HV), F32),
    ]
    return pl.pallas_call(
        functools.partial(_mixer_dec_kernel, nb=nb, n_prev=n_prev),
        grid=(nseq // nb,), in_specs=in_specs, out_specs=out_specs, out_shape=out_shape,
        scratch_shapes=scratch,
        compiler_params=pltpu.CompilerParams(dimension_semantics=("arbitrary",),
                                             vmem_limit_bytes=VMEM_LIMIT),
        name=name,
    )(*args)


def _block_diag_gates(wa, wx):
    per = LRU_BLOCKS // 2
    bw = wa.shape[-1]
    eye = jnp.eye(per, dtype=wa.dtype)

    def bd(w):
        return jnp.einsum("lncd,nm->lncmd", w, eye).reshape(w.shape[0], per * bw, per * bw)

    halves = [jnp.concatenate([bd(wa[:, hf * per:(hf + 1) * per]), bd(wx[:, hf * per:(hf + 1) * per])],
                              axis=-1) for hf in range(2)]
    return jnp.stack(halves, axis=1)


def kernel(x_prompt, x_sample, state_lru_h, state_lru_conv, state_gla_S, meta, norm_ffn1, w_ffn1_gu,
           w_ffn1_down, norm_mix, w_in, lru_conv_w, lru_conv_b, lru_wa, lru_ba, lru_wx, lru_bx,
           lru_lambda, gla_w_gate2, gla_b_gate, gla_norm, w_out, norm_ffn2, w_ffn2_gu, w_ffn2_down,
           norm_final):
    bp, lp, d = x_prompt.shape
    bs, ls, _ = x_sample.shape
    n_meta = meta.shape[0]
    depth = w_in.shape[0]
    assert d == D_MODEL and ls == SUBLANES and n_meta % SUBLANES == 0
    tp, ts = bp * lp, bs * ls
    tsm = ts + n_meta
    assert ts % n_meta == 0
    tm_p = 512 if tp % 512 == 0 else lp
    tm_s = tsm // 2 if (tsm // 2) % SUBLANES == 0 else tsm
    tc_p = 1024 if lp % 1024 == 0 else GLA_CHUNK
    nb_s = 32 if bs % 32 == 0 else bs
    tail = CONV_W - 1

    vecs = _pack_vectors(norm_mix, lru_ba, lru_bx, lru_lambda, lru_conv_b, gla_b_gate, gla_norm,
                         norm_ffn1, norm_ffn2, norm_final)
    mix = (vecs, w_in.astype(BF16),
           jnp.pad(w_in[:, :, D_MAIN:], ((0, 0), (0, 0), (0, LANES - GLA_RANK))).astype(BF16),
           jnp.pad(gla_w_gate2, ((0, 0), (0, LANES - GLA_RANK), (0, 0))).astype(BF16),
           lru_conv_w, _block_diag_gates(lru_wa, lru_wx).astype(BF16), w_out.astype(BF16))
    frames_s = jnp.pad(state_lru_conv, ((0, 0), (0, 0), (SUBLANES - tail, 0), (0, 0)))
    frames_s = frames_s.reshape(depth, ts, D_LRU)
    h0_s = jnp.broadcast_to(state_lru_h[:, :, None, :], (depth, bs, ls, D_LRU)).reshape(depth, ts, D_LRU)
    zero_frame = jnp.zeros((1, SUBLANES, D_LRU), F32)
    zero_h = jnp.zeros((1, 1, D_LRU), F32)
    zero_s = jnp.zeros((1, GLA_HEADS, GLA_DK, GLA_DV), F32)

    xp = x_prompt.reshape(tp, d)
    xs = jnp.concatenate([x_sample.reshape(ts, d), meta.astype(F32)], axis=0)

    hs_p, convs_p, ss_p, hs_s, convs_s, ss_s = [], [], [], [], [], []
    for l in range(depth):
        xp1, xs1 = _ffn_call(xp, xs, vecs, ROW_FFN1, w_ffn1_gu, w_ffn1_down, l, final=False, tm_a=tm_p,
                             tm_b=tm_s, tf=256, name=f"ffn1_{l}")

        last = l == depth - 1
        xs2, h_s, xl_s, s_s = _mixer_dec_call(xs1, frames_s[l], h0_s[l], state_gla_S,
                                              ss_s if last else (), mix, l, nseq=bs, nb=nb_s,
                                              name=f"mix_s{l}")
        ss_s.append(s_s)
        hs_s.append(h_s.reshape(bs, ls, D_LRU)[:, ls - 1])
        convs_s.append(xl_s.reshape(bs, ls, D_LRU)[:, ls - tail:])

        xm2, h_m, s_m, tail_m = _mixer_seq_call(
            xs1, zero_frame, zero_h, zero_s, mix, l, nb=1, sl=n_meta, tc=n_meta, cg=n_meta,
            row_block0=ts // n_meta, reset_first=True, shared_state=False, name=f"mix_m{l}")
        xs2 = jnp.concatenate([xs2, xm2], axis=0)

        xp2, h_p, s_p, tail_p = _mixer_seq_call(
            xp1, tail_m, h_m, s_m, mix, l, nb=bp, sl=lp, tc=tc_p, cg=GLA_CHUNK, row_block0=0,
            reset_first=False, shared_state=True, name=f"mix_p{l}")
        hs_p.append(h_p[:, 0])
        convs_p.append(tail_p[:, SUBLANES - tail:])
        ss_p.append(s_p)

        xp, xs = _ffn_call(xp2, xs2, vecs, ROW_FFN2, w_ffn2_gu, w_ffn2_down, l, final=last, tm_a=tm_p,
                           tm_b=tm_s, tf=256, name=f"ffn2_{l}")

    s_stack = ss_s[-1] if depth > 1 else ss_s[0][None]
    return (xp.reshape(bp, lp, d), xs[:ts].reshape(bs, ls, d),
            jnp.stack(hs_p), jnp.stack(convs_p), jnp.stack(ss_p),
            jnp.stack(hs_s), jnp.stack(convs_s), s_stack)
```

```python
import functools

import jax
import jax.numpy as jnp
from jax import lax
from jax.experimental import pallas as pl
from jax.experimental.pallas import tpu as pltpu

F32 = jnp.float32
BF16 = jnp.bfloat16

D_MODEL = 1024
D_FF = 2816
D_LRU = 512
LRU_BLOCKS = 8
CONV_W = 4
LRU_C = 8.0
GLA_HEADS = 4
GLA_DV = 128
GLA_DK = 64
GLA_RANK = 16
GLA_GATE_NORM = 16.0
GLA_CHUNK = 64
EPS = 1e-6
HK = GLA_HEADS * GLA_DK
HV = GLA_HEADS * GLA_DV
O_GL, O_Q, O_K, O_V, O_GO = D_LRU, 2 * D_LRU, 2 * D_LRU + HK, 2 * D_LRU + 2 * HK, 2 * D_LRU + 2 * HK + HV
D_MAIN = O_GO + HV
D_U = D_MAIN + HK
SUBLANES = 8
LANES = 128
VMEM_LIMIT = 58 * 1024 * 1024
ROW_MIX, ROW_GATE_B, ROW_LAM_CONVB, ROW_GLA, ROW_FFN1, ROW_FFN2, ROW_FINAL = range(7)


def _dot(a, b):
    if a.dtype != b.dtype:
        a = a.astype(b.dtype)
    return jnp.dot(a, b, preferred_element_type=F32)


def _dot_nt(a, b):
    return lax.dot_general(a, b, (((1,), (1,)), ((), ())), preferred_element_type=F32)


def _dot_tn(a, b):
    return lax.dot_general(a, b, (((0,), (0,)), ((), ())), preferred_element_type=F32)


def _rms(x, g):
    ms = jnp.mean(x * x, axis=-1, keepdims=True)
    return (x * lax.rsqrt(ms + EPS)) * g


def _sigmoid(x):
    return 0.5 * jnp.tanh(0.5 * x) + 0.5


def _silu(x):
    return x * _sigmoid(x)


def _log_sigmoid(x):
    return jnp.minimum(x, 0.0) - jnp.log(1.0 + jnp.exp(-jnp.abs(x)))


def _pack_vectors(norm_mix, ba, bx, lam, conv_b, b_gate, gla_norm, norm_ffn1, norm_ffn2, norm_final):
    depth, d = norm_mix.shape
    rows = [norm_mix, jnp.concatenate([ba, bx], axis=1), jnp.concatenate([lam, conv_b], axis=1),
            jnp.pad(jnp.concatenate([b_gate, gla_norm], axis=1), ((0, 0), (0, d - HK - GLA_DV))),
            norm_ffn1, norm_ffn2, jnp.broadcast_to(norm_final[None, :], (depth, d)),
            jnp.zeros((depth, d), F32)]
    return jnp.stack(rows, axis=1)


def _mixer_vectors(vecs_ref):
    row = lambda r, a, b: vecs_ref.at[r:r + 1, a:b]
    return (row(ROW_MIX, 0, D_MODEL), row(ROW_GLA, 0, HK), row(ROW_LAM_CONVB, D_LRU, 2 * D_LRU),
            row(ROW_GATE_B, 0, D_LRU), row(ROW_GATE_B, D_LRU, 2 * D_LRU), row(ROW_LAM_CONVB, 0, D_LRU),
            row(ROW_GLA, HK, HK + GLA_DV))


def _layer_spec(a, l):
    nd = a.ndim
    return pl.BlockSpec((None,) + a.shape[1:], lambda *_: (l,) + (0,) * (nd - 1),
                        pipeline_mode=pl.Buffered(1))


def _ffn_kernel(xa_ref, xb_ref, vecs_ref, wgu_hbm, wd_hbm, oa_ref, ob_ref, wgu_ref, wd_ref, stage_gu, stage_d,
                sem, *, tf, norm_row, has_final, n_a, l):
    g_ref = vecs_ref.at[norm_row:norm_row + 1, :]
    gfin_ref = vecs_ref.at[ROW_FINAL:ROW_FINAL + 1, :]
    n_c = D_FF // tf
    cols = lambda c: (slice(c * tf, (c + 1) * tf), slice(D_FF + c * tf, D_FF + (c + 1) * tf))

    def chunk_copies(c):
        gate, up = cols(c)
        slot = c % 2
        return (pltpu.make_async_copy(wgu_hbm.at[l, :, gate], stage_gu.at[slot, 0], sem.at[0, slot]),
                pltpu.make_async_copy(wgu_hbm.at[l, :, up], stage_gu.at[slot, 1], sem.at[1, slot]),
                pltpu.make_async_copy(wd_hbm.at[l, gate, :], stage_d.at[slot], sem.at[2, slot]))

    def start_chunk(c):
        for cp in chunk_copies(c):
            cp.start()

    def land_chunk(c):
        if c + 1 < n_c:
            start_chunk(c + 1)
        for cp in chunk_copies(c):
            cp.wait()
        gate, up = cols(c)
        slot = c % 2
        wgu_ref[:, gate] = stage_gu[slot, 0].astype(wgu_ref.dtype)
        wgu_ref[:, up] = stage_gu[slot, 1].astype(wgu_ref.dtype)
        wd_ref[gate, :] = stage_d[slot].astype(wd_ref.dtype)

    def tile(x_ref, xo_ref, load_weights):
        if load_weights:
            start_chunk(0)
        x = x_ref[...]
        xn = _rms(x, g_ref[...]).astype(wgu_ref.dtype)
        acc = None
        for c in range(n_c):
            if load_weights:
                land_chunk(c)
            gate, up = cols(c)
            d = _dot(_silu(_dot(xn, wgu_ref[:, gate])) * _dot(xn, wgu_ref[:, up]), wd_ref[gate, :])
            acc = d if acc is None else acc + d
        x = x + 0.5 * acc
        xo_ref[...] = _rms(x, gfin_ref[...]) if has_final else x

    i = pl.program_id(0)

    @pl.when(i == 0)
    def _():
        tile(xa_ref, oa_ref, True)

    @pl.when(jnp.logical_and(i > 0, i < n_a))
    def _():
        tile(xa_ref, oa_ref, False)

    @pl.when(i >= n_a)
    def _():
        tile(xb_ref, ob_ref, False)


def _ffn_call(xa, xb, vecs, norm_row, wgu, wd, l, *, final, tm_a, tm_b, tf, name):
    (ta, d), (tb, _) = xa.shape, xb.shape
    assert ta % tm_a == 0 and tb % tm_b == 0 and D_FF % tf == 0
    n_a, n_b = ta // tm_a, tb // tm_b
    rows_a = pl.BlockSpec((tm_a, d), lambda i: (jnp.minimum(i, n_a - 1), 0))
    rows_b = pl.BlockSpec((tm_b, d), lambda i: (jnp.maximum(i - n_a, 0), 0))
    hbm = pl.BlockSpec(memory_space=pl.ANY)
    scratch = [pltpu.VMEM(wgu.shape[1:], BF16), pltpu.VMEM(wd.shape[1:], BF16),
               pltpu.VMEM((2, 2, d, tf), wgu.dtype), pltpu.VMEM((2, tf, d), wd.dtype),
               pltpu.SemaphoreType.DMA((3, 2))]
    return pl.pallas_call(
        functools.partial(_ffn_kernel, tf=tf, norm_row=norm_row, has_final=final, n_a=n_a, l=l),
        grid=(n_a + n_b,), in_specs=[rows_a, rows_b, _layer_spec(vecs, l), hbm, hbm],
        out_specs=[rows_a, rows_b],
        out_shape=[jax.ShapeDtypeStruct((ta, d), F32), jax.ShapeDtypeStruct((tb, d), F32)],
        scratch_shapes=scratch,
        compiler_params=pltpu.CompilerParams(dimension_semantics=("arbitrary",),
                                             vmem_limit_bytes=VMEM_LIMIT),
        name=name,
    )(xa, xb, vecs, wgu, wd)


def _in_projection(x, gmix_ref, wmain_ref, wlr_ref, wg2_ref, bg_ref, u_sc):
    hn = _rms(x, gmix_ref[...]).astype(wmain_ref.dtype)
    u_sc[:, 0:D_MAIN] = _dot(hn, wmain_ref[...])
    lr = _dot(hn, wlr_ref[...])
    zg = _dot(lr, wg2_ref[...]) + bg_ref[...]
    u_sc[:, D_MAIN:D_U] = _log_sigmoid(zg) / GLA_GATE_NORM


def _lru_gates(xc, p, ba, bx, lam):
    half = xc.shape[1]
    r = _sigmoid(p[:, 0:half] + ba)
    i = _sigmoid(p[:, half:] + bx)
    log_a = -LRU_C * r * jax.nn.softplus(-lam)
    a = jnp.exp(log_a)
    y = -jnp.tanh(log_a) * (a * a + 1.0)
    mult = jnp.where(y > 0.0, y * lax.rsqrt(y), 0.0)
    return a, mult, i * xc


def _scan_groups(a, b):
    rows, cols = a.shape
    a3 = a.reshape(rows // SUBLANES, SUBLANES, cols)
    b3 = b.reshape(rows // SUBLANES, SUBLANES, cols)
    t = lax.broadcasted_iota(jnp.int32, a3.shape, 1)
    s = 1
    while s < SUBLANES:
        keep = t >= s
        b3 = jnp.where(keep, a3 * pltpu.roll(b3, s, 1) + b3, b3)
        a3 = jnp.where(keep, a3 * pltpu.roll(a3, s, 1), a3)
        s *= 2
    return a3, b3


def _scan_rows(a, b, h0):
    a3, b3 = _scan_groups(a, b)
    carry, out = h0, []
    for g in range(a3.shape[0]):
        hg = a3[g] * carry + b3[g]
        out.append(hg)
        carry = hg[SUBLANES - 1:SUBLANES]
    return out[0] if len(out) == 1 else jnp.concatenate(out, axis=0)


def _cumsum_rows(x, t, n):
    s = 1
    while s < n:
        x = x + jnp.where(t >= s, pltpu.roll(x, s, 0), 0.0)
        s *= 2
    return x


def _head_stack(q_s, lane_head):
    return jnp.concatenate([jnp.where(lane_head == hd, q_s, 0.0) for hd in range(GLA_HEADS)], axis=0)


def _gla_kv(k_end, v, per_head):
    if per_head:
        return [_dot_tn(k_end[:, hd * GLA_DK:(hd + 1) * GLA_DK], v[:, hd * GLA_DV:(hd + 1) * GLA_DV])
                for hd in range(GLA_HEADS)]
    kv = _dot_tn(k_end, v)
    return [kv[hd * GLA_DK:(hd + 1) * GLA_DK, hd * GLA_DV:(hd + 1) * GLA_DV] for hd in range(GLA_HEADS)]


def _state_decay(el):
    return jnp.broadcast_to(el, (LANES, HK)).T


def _gla_next_state(el_t, s_all, kv):
    return [el_t[hd * GLA_DK:(hd + 1) * GLA_DK] * s_all[hd * GLA_DK:(hd + 1) * GLA_DK] + kv[hd]
            for hd in range(GLA_HEADS)]


def _gla_intra(att_raw, causal, o_inter, v, n):
    att = jnp.where(causal, att_raw, 0.0)
    return [o_inter[hd * n:(hd + 1) * n] + _dot(att[hd * n:(hd + 1) * n], v[:, hd * GLA_DV:(hd + 1) * GLA_DV])
            for hd in range(GLA_HEADS)]


def _gla_chunk(q_s, k_s, k_end, el, v, s_all, causal, lane_head, n):
    stack = _head_stack(q_s, lane_head)
    o_inter = _dot(stack, s_all)
    att_raw = _dot_nt(stack, k_s)
    kv = _gla_kv(k_end, v, per_head=False)
    el_t = _state_decay(el)
    return _gla_intra(att_raw, causal, o_inter, v, n), _gla_next_state(el_t, s_all, kv)


def _gla_output(o, go, gn):
    return _rms(o, gn) * _silu(go)


def _mixer_seq_kernel(*refs, tc, cg, nj, reset_first):
    (xin_ref, frame0_ref, h0_ref, s0_ref, vecs_ref, wmain_ref, wlr_ref, wg2_ref, convw_ref, wg_ref,
     wout_ref) = refs[:11]
    gmix_ref, bg_ref, convb_ref, ba_ref, bx_ref, lam_ref, gn_ref = _mixer_vectors(vecs_ref)
    xo_ref, hl_ref, so_ref, tailo_ref, u, z, tail_sc, h_sc, s_sc = refs[11:]
    c = pl.program_id(0)

    @pl.when(c % nj == 0)
    def _():
        tail_sc[...] = frame0_ref[...]
        h_sc[...] = h0_ref[...]
        s_sc[...] = s0_ref[...].reshape(HK, GLA_DV)

    x = xin_ref[...]
    hn = _rms(x, gmix_ref[...]).astype(wmain_ref.dtype)
    tile_w = 2 * LANES
    carry = {}
    half = D_LRU // 2

    def in_tile(t):
        cols = slice(t * tile_w, (t + 1) * tile_w)
        u[:, cols] = _dot(hn, wmain_ref[:, cols])

    def gate_tile():
        lr = _dot(hn, wlr_ref[...])
        zg = _dot(lr, wg2_ref[...]) + bg_ref[...]
        u[:, D_MAIN:D_U] = _log_sigmoid(zg) / GLA_GATE_NORM

    def out_lru():
        xo_ref[...] = x + _dot(z[:, 0:D_LRU], wout_ref[0:D_LRU, :])

    def out_gla():
        xo_ref[...] += _dot(z[:, D_LRU:D_MODEL], wout_ref[D_LRU:D_MODEL, :])

    def lru_front(hf):
        cs = slice(hf * half, (hf + 1) * half)
        xl = u[:, cs]
        row8 = lax.broadcasted_iota(jnp.int32, (SUBLANES, half), 0)
        xc = convb_ref[:, cs]
        for sft in range(CONV_W - 1, -1, -1):
            if sft == 0:
                sh = xl
            else:
                rolled = pltpu.roll(xl, sft, 0)
                top = jnp.where(row8 < sft, pltpu.roll(tail_sc[:, cs], sft, 0), rolled[0:SUBLANES])
                sh = top if tc == SUBLANES else jnp.concatenate([top, rolled[SUBLANES:]], axis=0)
            xc = xc + convw_ref[CONV_W - 1 - sft:CONV_W - sft, cs] * sh
        carry["tail", hf] = xl[tc - SUBLANES:tc]
        tail_sc[:, cs] = carry["tail", hf]
        carry["xc", hf] = xc

    def lru_gate_mm(hf):
        carry["p", hf] = _dot(carry["xc", hf], wg_ref[hf])

    def lru_mid(hf):
        cs = slice(hf * half, (hf + 1) * half)
        a, mult, ix = _lru_gates(carry["xc", hf], carry["p", hf], ba_ref[:, cs], bx_ref[:, cs],
                                 lam_ref[:, cs])
        if reset_first:
            row = lax.broadcasted_iota(jnp.int32, (tc, half), 0)
            first = jnp.logical_and(row == 0, c % nj == 0)
            mult = jnp.where(first, 1.0, mult)
            a = jnp.where(first, 0.0, a)
        carry["a", hf], carry["b", hf] = a, mult * ix

    def lru_back(hf):
        cs = slice(hf * half, (hf + 1) * half)
        h = _scan_rows(carry["a", hf], carry["b", hf], h_sc[:, cs])
        carry["h", hf] = h[tc - 1:tc]
        h_sc[:, cs] = carry["h", hf]
        z[:, cs] = h * jax.nn.gelu(u[:, O_GL + hf * half:O_GL + (hf + 1) * half])

    lane_head = lax.broadcasted_iota(jnp.int32, (cg, HK), 1) // GLA_DK
    n_sub = tc // cg
    sub = lambda ci: slice(ci * cg, (ci + 1) * cg)

    def gla_prep(ci):
        tq = lax.broadcasted_iota(jnp.int32, (cg, HK), 0)
        b = _cumsum_rows(u[sub(ci), D_MAIN:D_U], tq, cg)
        bl = b[cg - 1:cg]
        k = u[sub(ci), O_K:O_K + HK]
        carry["stack", ci] = _head_stack(u[sub(ci), O_Q:O_Q + HK] * (GLA_DK ** -0.5) * jnp.exp(b),
                                         lane_head)
        carry["ks", ci], carry["ke", ci], carry["el", ci] = k * jnp.exp(-b), k * jnp.exp(bl - b), jnp.exp(bl)

    def gla_free_mm(ci):
        carry["att", ci] = _dot_nt(carry["stack", ci], carry["ks", ci])
        carry["kv", ci] = _gla_kv(carry["ke", ci], u[sub(ci), O_V:O_V + HV], per_head=True)
        carry["el_t", ci] = _state_decay(carry["el", ci])

    def gla_state(ci):
        s_all = carry["S"]
        carry["oi", ci] = _dot(carry["stack", ci], s_all)
        carry["S"] = jnp.concatenate(_gla_next_state(carry["el_t", ci], s_all, carry["kv", ci]), axis=0)

    def gla_out(ci):
        causal = ((lax.broadcasted_iota(jnp.int32, (GLA_HEADS * cg, cg), 0) % cg)
                  >= lax.broadcasted_iota(jnp.int32, (GLA_HEADS * cg, cg), 1))
        o = _gla_intra(carry["att", ci], causal, carry["oi", ci], u[sub(ci), O_V:O_V + HV], cg)
        for hd in range(GLA_HEADS):
            z[sub(ci), D_LRU + hd * GLA_DV:D_LRU + (hd + 1) * GLA_DV] = _gla_output(
                o[hd], u[sub(ci), O_GO + hd * GLA_DV:O_GO + (hd + 1) * GLA_DV], gn_ref[...])

    def gla_begin():
        carry["S"] = s_sc[...]

    def gla_end():
        s_sc[...] = carry["S"]

    lru = {n: functools.partial(f, hf) for hf in range(2)
           for n, f in ((f"front{hf}", lru_front), (f"gmm{hf}", lru_gate_mm), (f"mid{hf}", lru_mid),
                        (f"back{hf}", lru_back))}
    tiles = [functools.partial(in_tile, t) for t in range(D_MAIN // tile_w)]
    assert len(tiles) == 10
    prep = [functools.partial(gla_prep, ci) for ci in range(n_sub)]
    free = [functools.partial(gla_free_mm, ci) for ci in range(n_sub)]
    head = (tiles[0:3] + [lru["front0"], tiles[3], lru["front1"], lru["gmm0"], lru["gmm1"],
                          tiles[4], lru["mid0"], tiles[5], gate_tile, lru["back0"], tiles[6], lru["mid1"],
                          tiles[7], lru["back1"], tiles[8]])
    mid = [prep[0], tiles[9]] + ([prep[1], out_lru] if n_sub > 1 else [out_lru])
    for ci in range(2, n_sub):
        mid += [prep[ci], free[ci - 2]]
    mid += free[max(n_sub - 2, 0):]
    order = (head + mid + [gla_begin] + [functools.partial(gla_state, ci) for ci in range(n_sub)]
             + [gla_end] + [functools.partial(gla_out, ci) for ci in range(n_sub)] + [out_gla])
    for piece in order:
        piece()

    for hf in range(2):
        cs = slice(hf * half, (hf + 1) * half)
        hl_ref[:, cs] = carry["h", hf]
        tailo_ref[:, cs] = carry["tail", hf]
    so_ref[...] = s_sc[...].reshape(GLA_HEADS, GLA_DK, GLA_DV)


def _mixer_seq_call(x, frame0, h0, s0, mix, l, *, nb, sl, tc, cg, row_block0, reset_first,
                    shared_state, name):
    assert sl % tc == 0 and tc % cg == 0 and tc % SUBLANES == 0
    nj = sl // tc
    xrow = pl.BlockSpec((tc, D_MODEL), lambda s: (row_block0 + s, 0))
    seq = lambda s: s // nj
    if shared_state:
        st = lambda s: (0, 0, 0)
        st4 = lambda s: (0, 0, 0, 0)
    else:
        st = lambda s: (seq(s), 0, 0)
        st4 = lambda s: (seq(s), 0, 0, 0)
    (vecs, w_in, wlr, wg2, convw, wg, wout) = mix
    wmain_spec = pl.BlockSpec((None, D_MODEL, D_MAIN), lambda s: (l, 0, 0), pipeline_mode=pl.Buffered(1))
    args = [x, frame0, h0, s0, vecs, w_in, wlr, wg2, convw, wg, wout]
    in_specs = [
        xrow,
        pl.BlockSpec((None, SUBLANES, D_LRU), st),
        pl.BlockSpec((None, 1, D_LRU), st),
        pl.BlockSpec((None, GLA_HEADS, GLA_DK, GLA_DV), st4),
        _layer_spec(vecs, l), wmain_spec, _layer_spec(wlr, l), _layer_spec(wg2, l), _layer_spec(convw, l),
        _layer_spec(wg, l), _layer_spec(wout, l),
    ]
    out_shape = [
        jax.ShapeDtypeStruct((nb * sl, D_MODEL), F32),
        jax.ShapeDtypeStruct((nb, 1, D_LRU), F32),
        jax.ShapeDtypeStruct((nb, GLA_HEADS, GLA_DK, GLA_DV), F32),
        jax.ShapeDtypeStruct((nb, SUBLANES, D_LRU), F32),
    ]
    out_specs = [
        pl.BlockSpec((tc, D_MODEL), lambda s: (s, 0)),
        pl.BlockSpec((None, 1, D_LRU), lambda s: (seq(s), 0, 0)),
        pl.BlockSpec((None, GLA_HEADS, GLA_DK, GLA_DV), lambda s: (seq(s), 0, 0, 0)),
        pl.BlockSpec((None, SUBLANES, D_LRU), lambda s: (seq(s), 0, 0)),
    ]
    scratch = [
        pltpu.VMEM((tc, D_U), F32),
        pltpu.VMEM((tc, D_MODEL), F32),
        pltpu.VMEM((SUBLANES, D_LRU), F32),
        pltpu.VMEM((1, D_LRU), F32),
        pltpu.VMEM((HK, GLA_DV), F32),
    ]
    kern = functools.partial(_mixer_seq_kernel, tc=tc, cg=cg, nj=nj, reset_first=reset_first)
    return pl.pallas_call(
        kern, grid=(nb * nj,), in_specs=in_specs, out_specs=out_specs, out_shape=out_shape,
        scratch_shapes=scratch,
        compiler_params=pltpu.CompilerParams(dimension_semantics=("arbitrary",),
                                             vmem_limit_bytes=VMEM_LIMIT),
        name=name,
    )(*args)


def _mixer_dec_kernel(*refs, nb, n_prev):
    (x_ref, frame_ref, h0_ref, s0_ref, vecs_ref, wmain_ref, wlr_ref, wg2_ref, convw_ref, wg_ref,
     wout_ref) = refs[:11]
    gmix_ref, bg_ref, convb_ref, ba_ref, bx_ref, lam_ref, gn_ref = _mixer_vectors(vecs_ref)
    prev_refs = refs[11:11 + n_prev]
    (xo_ref, h_ref, xl_ref, so_ref,
     u_sc, z_sc, qs_sc, ks_sc, ke_sc, el_sc, o_sc) = refs[11 + n_prev:]
    for j, prev in enumerate(prev_refs):
        so_ref[j] = prev[...]
    so_new = so_ref.at[n_prev] if n_prev else so_ref
    ls = SUBLANES
    rows = nb * ls
    x = x_ref[...]
    _in_projection(x, gmix_ref, wmain_ref, wlr_ref, wg2_ref, bg_ref, u_sc)

    t = lax.broadcasted_iota(jnp.int32, (rows, D_LRU), 0) % ls
    xl = u_sc[:, 0:D_LRU]
    xl_ref[...] = xl
    frame = frame_ref[...]
    xc = convb_ref[...]
    for s in range(CONV_W - 1, -1, -1):
        if s == 0:
            sh = xl
        else:
            sh = jnp.where(t >= s, pltpu.roll(xl, s, 0), pltpu.roll(frame, (rows - ls + s) % rows, 0))
        xc = xc + convw_ref[CONV_W - 1 - s:CONV_W - s, :] * sh
    half = D_LRU // 2
    parts = []
    for hf in range(2):
        cs = slice(hf * half, (hf + 1) * half)
        parts.append(_lru_gates(xc[:, cs], _dot(xc[:, cs], wg_ref[hf]), ba_ref[:, cs], bx_ref[:, cs],
                                lam_ref[:, cs]))
    a, mult, ix = (jnp.concatenate([p[n] for p in parts], axis=1) for n in range(3))
    a3, b3 = _scan_groups(a, mult * ix)
    h = (a3 * h0_ref[...].reshape(a3.shape) + b3).reshape(rows, D_LRU)
    h_ref[...] = h
    z_sc[:, 0:D_LRU] = h * jax.nn.gelu(u_sc[:, O_GL:O_GL + D_LRU])

    tq = lax.broadcasted_iota(jnp.int32, (rows, HK), 0) % ls
    b = _cumsum_rows(u_sc[:, D_MAIN:D_U], tq, ls)
    bl = jnp.where(tq == ls - 1, b, 0.0)
    s = 1
    while s < ls:
        bl = bl + jnp.where(tq + s < ls, pltpu.roll(bl, rows - s, 0), 0.0)
        s *= 2
    k = u_sc[:, O_K:O_K + HK]
    qs_sc[...] = (u_sc[:, O_Q:O_Q + HK] * (GLA_DK ** -0.5)) * jnp.exp(b)
    ks_sc[...] = k * jnp.exp(-b)
    ke_sc[...] = k * jnp.exp(bl - b)
    el_sc[...] = jnp.exp(bl)

    lane_head = lax.broadcasted_iota(jnp.int32, (ls, HK), 1) // GLA_DK
    causal = ((lax.broadcasted_iota(jnp.int32, (GLA_HEADS * ls, ls), 0) % ls)
              >= lax.broadcasted_iota(jnp.int32, (GLA_HEADS * ls, ls), 1))

    def body(bi, carry):
        rs = pl.ds(pl.multiple_of(bi * ls, ls), ls)
        s_all = s0_ref[bi].reshape(HK, GLA_DV)
        o, s_new = _gla_chunk(qs_sc[rs, :], ks_sc[rs, :], ke_sc[rs, :], el_sc[rs, :][0:1],
                              u_sc[rs, O_V:O_V + HV], s_all, causal, lane_head, ls)
        for hd in range(GLA_HEADS):
            so_new[bi, hd] = s_new[hd]
            o_sc[rs, hd * GLA_DV:(hd + 1) * GLA_DV] = o[hd]
        return carry

    lax.fori_loop(0, nb, body, 0, unroll=4)

    gn = gn_ref[...]
    for hd in range(GLA_HEADS):
        vs = slice(hd * GLA_DV, (hd + 1) * GLA_DV)
        z_sc[:, D_LRU + hd * GLA_DV:D_LRU + (hd + 1) * GLA_DV] = _gla_output(
            o_sc[:, vs], u_sc[:, O_GO + hd * GLA_DV:O_GO + (hd + 1) * GLA_DV], gn)
    xo_ref[...] = x + _dot(z_sc[...], wout_ref[...])


def _mixer_dec_call(x, frame, h0, s0, s_prev, mix, l, *, nseq, nb, name):
    ls = SUBLANES
    assert nseq % nb == 0
    rows = nb * ls
    (vecs, w_in, wlr, wg2, convw, wg, wout) = mix
    rowspec = lambda w: pl.BlockSpec((rows, w), lambda i: (i, 0))
    sspec = pl.BlockSpec((None, nb, GLA_HEADS, GLA_DK, GLA_DV), lambda i: (l, i, 0, 0, 0))
    wmain_spec = pl.BlockSpec((None, D_MODEL, D_MAIN), lambda i: (l, 0, 0), pipeline_mode=pl.Buffered(1))
    in_specs = [
        rowspec(D_MODEL), rowspec(D_LRU), rowspec(D_LRU), sspec,
        _layer_spec(vecs, l), wmain_spec, _layer_spec(wlr, l), _layer_spec(wg2, l), _layer_spec(convw, l),
        _layer_spec(wg, l), _layer_spec(wout, l),
    ]
    out_shape = [
        jax.ShapeDtypeStruct((nseq * ls, D_MODEL), F32),
        jax.ShapeDtypeStruct((nseq * ls, D_LRU), F32),
        jax.ShapeDtypeStruct((nseq * ls, D_LRU), F32),
    ]
    state = (nseq, GLA_HEADS, GLA_DK, GLA_DV)
    sblock = pl.BlockSpec((nb,) + state[1:], lambda i: (i, 0, 0, 0))
    n_prev = len(s_prev)
    if n_prev:
        out_shape.append(jax.ShapeDtypeStruct((n_prev + 1,) + state, F32))
        s_out_spec = pl.BlockSpec((n_prev + 1, nb) + state[1:], lambda i: (0, i, 0, 0, 0))
    else:
        out_shape.append(jax.ShapeDtypeStruct(state, F32))
        s_out_spec = sblock
    out_specs = [rowspec(D_MODEL), rowspec(D_LRU), rowspec(D_LRU), s_out_spec]
    args = [x, frame, h0, s0, vecs, w_in, wlr, wg2, convw, wg, wout]
    args += list(s_prev)
    in_specs += [sblock] * n_prev
    scratch = [
        pltpu.VMEM((rows, D_U), F32), pltpu.VMEM((rows, D_MODEL), F32),
        pltpu.VMEM((rows, HK), F32), pltpu.VMEM((rows, HK), F32), pltpu.VMEM((rows, HK), F32),
        pltpu.VMEM((rows, HK), F32), pltpu.VMEM((rows, HV), F32),
    ]
    return pl.pallas_call(
        functools.partial(_mixer_dec_kernel, nb=nb, n_prev=n_prev),
        grid=(nseq // nb,), in_specs=in_specs, out_specs=out_specs, out_shape=out_shape,
        scratch_shapes=scratch,
        compiler_params=pltpu.CompilerParams(dimension_semantics=("arbitrary",),
                                             vmem_limit_bytes=VMEM_LIMIT),
        name=name,
    )(*args)


def _block_diag_gates(wa, wx):
    per = LRU_BLOCKS // 2
    bw = wa.shape[-1]
    eye = jnp.eye(per, dtype=wa.dtype)

    def bd(w):
        return jnp.einsum("lncd,nm->lncmd", w, eye).reshape(w.shape[0], per * bw, per * bw)

    halves = [jnp.concatenate([bd(wa[:, hf * per:(hf + 1) * per]), bd(wx[:, hf * per:(hf + 1) * per])],
                              axis=-1) for hf in range(2)]
    return jnp.stack(halves, axis=1)


def kernel(x_prompt, x_sample, state_lru_h, state_lru_conv, state_gla_S, meta, norm_ffn1, w_ffn1_gu,
           w_ffn1_down, norm_mix, w_in, lru_conv_w, lru_conv_b, lru_wa, lru_ba, lru_wx, lru_bx,
           lru_lambda, gla_w_gate2, gla_b_gate, gla_norm, w_out, norm_ffn2, w_ffn2_gu, w_ffn2_down,
           norm_final):
    bp, lp, d = x_prompt.shape
    bs, ls, _ = x_sample.shape
    n_meta = meta.shape[0]
    depth = w_in.shape[0]
    assert d == D_MODEL and ls == SUBLANES and n_meta % SUBLANES == 0
    tp, ts = bp * lp, bs * ls
    tsm = ts + n_meta
    assert ts % n_meta == 0
    tm_p = 512 if tp % 512 == 0 else lp
    tm_s = tsm // 2 if (tsm // 2) % SUBLANES == 0 else tsm
    tc_p = 1024 if lp % 1024 == 0 else GLA_CHUNK
    nb_s = 32 if bs % 32 == 0 else bs
    tail = CONV_W - 1

    vecs = _pack_vectors(norm_mix, lru_ba, lru_bx, lru_lambda, lru_conv_b, gla_b_gate, gla_norm,
                         norm_ffn1, norm_ffn2, norm_final)
    mix = (vecs, w_in.astype(BF16),
           jnp.pad(w_in[:, :, D_MAIN:], ((0, 0), (0, 0), (0, LANES - GLA_RANK))).astype(BF16),
           jnp.pad(gla_w_gate2, ((0, 0), (0, LANES - GLA_RANK), (0, 0))).astype(BF16),
           lru_conv_w, _block_diag_gates(lru_wa, lru_wx).astype(BF16), w_out.astype(BF16))
    frames_s = jnp.pad(state_lru_conv, ((0, 0), (0, 0), (SUBLANES - tail, 0), (0, 0)))
    frames_s = frames_s.reshape(depth, ts, D_LRU)
    h0_s = jnp.broadcast_to(state_lru_h[:, :, None, :], (depth, bs, ls, D_LRU)).reshape(depth, ts, D_LRU)
    zero_frame = jnp.zeros((1, SUBLANES, D_LRU), F32)
    zero_h = jnp.zeros((1, 1, D_LRU), F32)
    zero_s = jnp.zeros((1, GLA_HEADS, GLA_DK, GLA_DV), F32)

    xp = x_prompt.reshape(tp, d)
    xs = jnp.concatenate([x_sample.reshape(ts, d), meta.astype(F32)], axis=0)

    hs_p, convs_p, ss_p, hs_s, convs_s, ss_s = [], [], [], [], [], []
    for l in range(depth):
        xp1, xs1 = _ffn_call(xp, xs, vecs, ROW_FFN1, w_ffn1_gu, w_ffn1_down, l, final=False, tm_a=tm_p,
                             tm_b=tm_s, tf=256, name=f"ffn1_{l}")

        last = l == depth - 1
        xs2, h_s, xl_s, s_s = _mixer_dec_call(xs1, frames_s[l], h0_s[l], state_gla_S,
                                              ss_s if last else (), mix, l, nseq=bs, nb=nb_s,
                                              name=f"mix_s{l}")
        ss_s.append(s_s)
        hs_s.append(h_s.reshape(bs, ls, D_LRU)[:, ls - 1])
        convs_s.append(xl_s.reshape(bs, ls, D_LRU)[:, ls - tail:])

        xm2, h_m, s_m, tail_m = _mixer_seq_call(
            xs1, zero_frame, zero_h, zero_s, mix, l, nb=1, sl=n_meta, tc=n_meta, cg=n_meta,
            row_block0=ts // n_meta, reset_first=True, shared_state=False, name=f"mix_m{l}")
        xs2 = jnp.concatenate([xs2, xm2], axis=0)

        xp2, h_p, s_p, tail_p = _mixer_seq_call(
            xp1, tail_m, h_m, s_m, mix, l, nb=bp, sl=lp, tc=tc_p, cg=GLA_CHUNK, row_block0=0,
            reset_first=False, shared_state=True, name=f"mix_p{l}")
        hs_p.append(h_p[:, 0])
        convs_p.append(tail_p[:, SUBLANES - tail:])
        ss_p.append(s_p)

        xp, xs = _ffn_call(xp2, xs2, vecs, ROW_FFN2, w_ffn2_gu, w_ffn2_down, l, final=last, tm_a=tm_p,
                           tm_b=tm_s, tf=256, name=f"ffn2_{l}")

    s_stack = ss_s[-1] if depth > 1 else ss_s[0][None]
    return (xp.reshape(bp, lp, d), xs[:ts].reshape(bs, ls, d),
            jnp.stack(hs_p), jnp.stack(convs_p), jnp.stack(ss_p),
            jnp.stack(hs_s), jnp.stack(convs_s), s_stack)
```

```python
import functools

import jax
import jax.numpy as jnp
from jax import lax
from jax.experimental import pallas as pl
from jax.experimental.pallas import tpu as pltpu

F32 = jnp.float32
BF16 = jnp.bfloat16

D_MODEL = 1024
D_FF = 2816
D_LRU = 512
LRU_BLOCKS = 8
CONV_W = 4
LRU_C = 8.0
GLA_HEADS = 4
GLA_DV = 128
GLA_DK = 64
GLA_RANK = 16
GLA_GATE_NORM = 16.0
GLA_CHUNK = 64
EPS = 1e-6
HK = GLA_HEADS * GLA_DK
HV = GLA_HEADS * GLA_DV
O_GL, O_Q, O_K, O_V, O_GO = D_LRU, 2 * D_LRU, 2 * D_LRU + HK, 2 * D_LRU + 2 * HK, 2 * D_LRU + 2 * HK + HV
D_MAIN = O_GO + HV
D_U = D_MAIN + HK
SUBLANES = 8
LANES = 128
VMEM_LIMIT = 58 * 1024 * 1024
ROW_MIX, ROW_GATE_B, ROW_LAM_CONVB, ROW_GLA, ROW_FFN1, ROW_FFN2, ROW_FINAL = range(7)


def _dot(a, b):
    if a.dtype != b.dtype:
        a = a.astype(b.dtype)
    return jnp.dot(a, b, preferred_element_type=F32)


def _dot_nt(a, b):
    return lax.dot_general(a, b, (((1,), (1,)), ((), ())), preferred_element_type=F32)


def _dot_tn(a, b):
    return lax.dot_general(a, b, (((0,), (0,)), ((), ())), preferred_element_type=F32)


def _rms(x, g):
    ms = jnp.mean(x * x, axis=-1, keepdims=True)
    return (x * lax.rsqrt(ms + EPS)) * g


def _sigmoid(x):
    return 0.5 * jnp.tanh(0.5 * x) + 0.5


def _silu(x):
    return x * _sigmoid(x)


def _log_sigmoid(x):
    return jnp.minimum(x, 0.0) - jnp.log(1.0 + jnp.exp(-jnp.abs(x)))


def _pack_vectors(norm_mix, ba, bx, lam, conv_b, b_gate, gla_norm, norm_ffn1, norm_ffn2, norm_final):
    depth, d = norm_mix.shape
    rows = [norm_mix, jnp.concatenate([ba, bx], axis=1), jnp.concatenate([lam, conv_b], axis=1),
            jnp.pad(jnp.concatenate([b_gate, gla_norm], axis=1), ((0, 0), (0, d - HK - GLA_DV))),
            norm_ffn1, norm_ffn2, jnp.broadcast_to(norm_final[None, :], (depth, d)),
            jnp.zeros((depth, d), F32)]
    return jnp.stack(rows, axis=1)


def _mixer_vectors(vecs_ref):
    row = lambda r, a, b: vecs_ref.at[r:r + 1, a:b]
    return (row(ROW_MIX, 0, D_MODEL), row(ROW_GLA, 0, HK), row(ROW_LAM_CONVB, D_LRU, 2 * D_LRU),
            row(ROW_GATE_B, 0, D_LRU), row(ROW_GATE_B, D_LRU, 2 * D_LRU), row(ROW_LAM_CONVB, 0, D_LRU),
            row(ROW_GLA, HK, HK + GLA_DV))


def _layer_spec(a, l):
    nd = a.ndim
    return pl.BlockSpec((None,) + a.shape[1:], lambda *_: (l,) + (0,) * (nd - 1),
                        pipeline_mode=pl.Buffered(1))


def _whole_spec(a):
    nd = a.ndim
    return pl.BlockSpec(a.shape, lambda *_: (0,) * nd, pipeline_mode=pl.Buffered(1))


_MAIN_COLS_SPEC = pl.BlockSpec((D_MODEL, D_MAIN), lambda *_: (0, 0), pipeline_mode=pl.Buffered(1))


def _ffn_kernel(*refs, tf, norm_row, has_final, n_a, n_cast):
    xa_ref, xb_ref, vecs_ref, wgu_ref, wd_ref = refs[:5]
    cast_src = refs[5:5 + n_cast]
    oa_ref, ob_ref = refs[5 + n_cast:7 + n_cast]
    cast_dst = refs[7 + n_cast:]
    g_ref = vecs_ref.at[norm_row:norm_row + 1, :]
    gfin_ref = vecs_ref.at[ROW_FINAL:ROW_FINAL + 1, :]

    def tile(x_ref, xo_ref):
        x = x_ref[...]
        xn = _rms(x, g_ref[...])
        acc = None
        for c in range(D_FF // tf):
            gate = _dot(xn, wgu_ref[:, c * tf:(c + 1) * tf])
            up = _dot(xn, wgu_ref[:, D_FF + c * tf:D_FF + (c + 1) * tf])
            d = _dot(_silu(gate) * up, wd_ref[c * tf:(c + 1) * tf, :])
            acc = d if acc is None else acc + d
        x = x + 0.5 * acc
        xo_ref[...] = _rms(x, gfin_ref[...]) if has_final else x

    i = pl.program_id(0)

    @pl.when(i < n_a)
    def _():
        tile(xa_ref, oa_ref)
        for s_ref, d_ref in zip(cast_src, cast_dst):
            d_ref[...] = s_ref[...].astype(d_ref.dtype)

    @pl.when(i >= n_a)
    def _():
        tile(xb_ref, ob_ref)


def _ffn_call(xa, xb, vecs, norm_row, wgu, wd, l, *, final, tm_a, tm_b, tf, name, cast=()):
    (ta, d), (tb, _) = xa.shape, xb.shape
    assert ta % tm_a == 0 and tb % tm_b == 0 and D_FF % tf == 0
    n_a, n_b = ta // tm_a, tb // tm_b
    rows_a = pl.BlockSpec((tm_a, d), lambda i: (jnp.minimum(i, n_a - 1), 0))
    rows_b = pl.BlockSpec((tm_b, d), lambda i: (jnp.maximum(i - n_a, 0), 0))
    args = [xa, xb, vecs, wgu, wd] + list(cast)
    specs = [rows_a, rows_b, _layer_spec(vecs, l), _layer_spec(wgu, l), _layer_spec(wd, l)]
    out_specs = [rows_a, rows_b]
    out_shape = [jax.ShapeDtypeStruct((ta, d), F32), jax.ShapeDtypeStruct((tb, d), F32)]
    for w in cast:
        _, r, c = w.shape
        assert r % n_a == 0 and (r // n_a) % (2 * SUBLANES) == 0
        slab = lambda i: (jnp.minimum(i, n_a - 1), 0)
        specs.append(pl.BlockSpec((None, r // n_a, c), lambda i: (l, jnp.minimum(i, n_a - 1), 0)))
        out_specs.append(pl.BlockSpec((r // n_a, c), slab))
        out_shape.append(jax.ShapeDtypeStruct((r, c), BF16))
    return pl.pallas_call(
        functools.partial(_ffn_kernel, tf=tf, norm_row=norm_row, has_final=final, n_a=n_a,
                          n_cast=len(cast)),
        grid=(n_a + n_b,), in_specs=specs, out_specs=out_specs, out_shape=out_shape,
        compiler_params=pltpu.CompilerParams(dimension_semantics=("arbitrary",),
                                             vmem_limit_bytes=VMEM_LIMIT),
        name=name,
    )(*args)


def _in_projection(x, gmix_ref, wmain_ref, wlr_ref, wg2_ref, bg_ref, u_sc):
    hn = _rms(x, gmix_ref[...]).astype(wmain_ref.dtype)
    u_sc[:, 0:D_MAIN] = _dot(hn, wmain_ref[...])
    lr = _dot(hn, wlr_ref[...])
    zg = _dot(lr, wg2_ref[...]) + bg_ref[...]
    u_sc[:, D_MAIN:D_U] = _log_sigmoid(zg) / GLA_GATE_NORM


def _lru_gates(xc, p, ba, bx, lam):
    half = xc.shape[1]
    r = _sigmoid(p[:, 0:half] + ba)
    i = _sigmoid(p[:, half:] + bx)
    log_a = -LRU_C * r * jax.nn.softplus(-lam)
    a = jnp.exp(log_a)
    y = -jnp.tanh(log_a) * (a * a + 1.0)
    mult = jnp.where(y > 0.0, y * lax.rsqrt(y), 0.0)
    return a, mult, i * xc


def _scan_groups(a, b):
    rows, cols = a.shape
    a3 = a.reshape(rows // SUBLANES, SUBLANES, cols)
    b3 = b.reshape(rows // SUBLANES, SUBLANES, cols)
    t = lax.broadcasted_iota(jnp.int32, a3.shape, 1)
    s = 1
    while s < SUBLANES:
        keep = t >= s
        b3 = jnp.where(keep, a3 * pltpu.roll(b3, s, 1) + b3, b3)
        a3 = jnp.where(keep, a3 * pltpu.roll(a3, s, 1), a3)
        s *= 2
    return a3, b3


def _scan_rows(a, b, h0):
    a3, b3 = _scan_groups(a, b)
    carry, out = h0, []
    for g in range(a3.shape[0]):
        hg = a3[g] * carry + b3[g]
        out.append(hg)
        carry = hg[SUBLANES - 1:SUBLANES]
    return out[0] if len(out) == 1 else jnp.concatenate(out, axis=0)


def _cumsum_rows(x, t, n):
    s = 1
    while s < n:
        x = x + jnp.where(t >= s, pltpu.roll(x, s, 0), 0.0)
        s *= 2
    return x


def _head_stack(q_s, lane_head):
    return jnp.concatenate([jnp.where(lane_head == hd, q_s, 0.0) for hd in range(GLA_HEADS)], axis=0)


def _gla_kv(k_end, v, per_head):
    if per_head:
        return [_dot_tn(k_end[:, hd * GLA_DK:(hd + 1) * GLA_DK], v[:, hd * GLA_DV:(hd + 1) * GLA_DV])
                for hd in range(GLA_HEADS)]
    out = []
    for pr in range(GLA_HEADS // 2):
        kv = _dot_tn(k_end[:, 2 * pr * GLA_DK:2 * (pr + 1) * GLA_DK], v[:, 2 * pr * GLA_DV:2 * (pr + 1) * GLA_DV])
        out += [kv[0:GLA_DK, 0:GLA_DV], kv[GLA_DK:2 * GLA_DK, GLA_DV:2 * GLA_DV]]
    return out


def _state_decay(el):
    return jnp.broadcast_to(el, (LANES, HK)).T


def _gla_next_state(el_t, s_all, kv):
    return [el_t[hd * GLA_DK:(hd + 1) * GLA_DK] * s_all[hd * GLA_DK:(hd + 1) * GLA_DK] + kv[hd]
            for hd in range(GLA_HEADS)]


def _gla_intra(att_raw, causal, o_inter, v, n):
    att = jnp.where(causal, att_raw, 0.0)
    return [o_inter[hd * n:(hd + 1) * n] + _dot(att[hd * n:(hd + 1) * n], v[:, hd * GLA_DV:(hd + 1) * GLA_DV])
            for hd in range(GLA_HEADS)]


def _gla_chunk(q_s, k_s, k_end, el, v, s_all, causal, lane_head, n):
    stack = _head_stack(q_s, lane_head)
    o_inter = _dot(stack, s_all)
    att_raw = _dot_nt(stack, k_s)
    kv = _gla_kv(k_end, v, per_head=False)
    el_t = _state_decay(el)
    return _gla_intra(att_raw, causal, o_inter, v, n), _gla_next_state(el_t, s_all, kv)


def _gla_output(o, go, gn):
    return _rms(o, gn) * _silu(go)


def _mixer_seq_kernel(*refs, tc, cg, nj, reset_first):
    (xin_ref, frame0_ref, h0_ref, s0_ref, vecs_ref, wmain_ref, wlr_ref, wg2_ref, convw_ref, wg_ref,
     wout_ref) = refs[:11]
    gmix_ref, bg_ref, convb_ref, ba_ref, bx_ref, lam_ref, gn_ref = _mixer_vectors(vecs_ref)
    xo_ref, hl_ref, so_ref, tailo_ref, u, z, tail_sc, h_sc, s_sc = refs[11:]
    c = pl.program_id(0)

    @pl.when(c % nj == 0)
    def _():
        tail_sc[...] = frame0_ref[...]
        h_sc[...] = h0_ref[...]
        s_sc[...] = s0_ref[...].reshape(HK, GLA_DV)

    x = xin_ref[...]
    hn = _rms(x, gmix_ref[...]).astype(wmain_ref.dtype)
    tile_w = 2 * LANES
    carry = {}
    half = D_LRU // 2

    def in_tile(t):
        cols = slice(t * tile_w, (t + 1) * tile_w)
        u[:, cols] = _dot(hn, wmain_ref[:, cols])

    def gate_tile():
        lr = _dot(hn, wlr_ref[...])
        zg = _dot(lr, wg2_ref[...]) + bg_ref[...]
        u[:, D_MAIN:D_U] = _log_sigmoid(zg) / GLA_GATE_NORM

    def out_lru():
        xo_ref[...] = x + _dot(z[:, 0:D_LRU], wout_ref[0:D_LRU, :])

    def out_gla():
        xo_ref[...] += _dot(z[:, D_LRU:D_MODEL], wout_ref[D_LRU:D_MODEL, :])

    def lru_front(hf):
        cs = slice(hf * half, (hf + 1) * half)
        xl = u[:, cs]
        row8 = lax.broadcasted_iota(jnp.int32, (SUBLANES, half), 0)
        xc = convb_ref[:, cs]
        for sft in range(CONV_W - 1, -1, -1):
            if sft == 0:
                sh = xl
            else:
                rolled = pltpu.roll(xl, sft, 0)
                top = jnp.where(row8 < sft, pltpu.roll(tail_sc[:, cs], sft, 0), rolled[0:SUBLANES])
                sh = top if tc == SUBLANES else jnp.concatenate([top, rolled[SUBLANES:]], axis=0)
            xc = xc + convw_ref[CONV_W - 1 - sft:CONV_W - sft, cs] * sh
        carry["tail", hf] = xl[tc - SUBLANES:tc]
        tail_sc[:, cs] = carry["tail", hf]
        carry["xc", hf] = xc

    def lru_gate_mm(hf):
        carry["p", hf] = _dot(carry["xc", hf], wg_ref[hf])

    def lru_mid(hf):
        cs = slice(hf * half, (hf + 1) * half)
        a, mult, ix = _lru_gates(carry["xc", hf], carry["p", hf], ba_ref[:, cs], bx_ref[:, cs],
                                 lam_ref[:, cs])
        if reset_first:
            row = lax.broadcasted_iota(jnp.int32, (tc, half), 0)
            first = jnp.logical_and(row == 0, c % nj == 0)
            mult = jnp.where(first, 1.0, mult)
            a = jnp.where(first, 0.0, a)
        carry["a", hf], carry["b", hf] = a, mult * ix

    def lru_back(hf):
        cs = slice(hf * half, (hf + 1) * half)
        h = _scan_rows(carry["a", hf], carry["b", hf], h_sc[:, cs])
        carry["h", hf] = h[tc - 1:tc]
        h_sc[:, cs] = carry["h", hf]
        z[:, cs] = h * jax.nn.gelu(u[:, O_GL + hf * half:O_GL + (hf + 1) * half])

    lane_head = lax.broadcasted_iota(jnp.int32, (cg, HK), 1) // GLA_DK
    n_sub = tc // cg
    sub = lambda ci: slice(ci * cg, (ci + 1) * cg)

    def gla_prep(ci):
        tq = lax.broadcasted_iota(jnp.int32, (cg, HK), 0)
        b = _cumsum_rows(u[sub(ci), D_MAIN:D_U], tq, cg)
        bl = b[cg - 1:cg]
        k = u[sub(ci), O_K:O_K + HK]
        carry["stack", ci] = _head_stack(u[sub(ci), O_Q:O_Q + HK] * (GLA_DK ** -0.5) * jnp.exp(b),
                                         lane_head)
        carry["ks", ci], carry["ke", ci], carry["el", ci] = k * jnp.exp(-b), k * jnp.exp(bl - b), jnp.exp(bl)

    def gla_free_mm(ci):
        carry["att", ci] = _dot_nt(carry["stack", ci], carry["ks", ci])
        carry["kv", ci] = _gla_kv(carry["ke", ci], u[sub(ci), O_V:O_V + HV], per_head=True)
        carry["el_t", ci] = _state_decay(carry["el", ci])

    def gla_state(ci):
        s_all = carry["S"]
        carry["oi", ci] = _dot(carry["stack", ci], s_all)
        carry["S"] = jnp.concatenate(_gla_next_state(carry["el_t", ci], s_all, carry["kv", ci]), axis=0)

    def gla_out(ci):
        causal = ((lax.broadcasted_iota(jnp.int32, (GLA_HEADS * cg, cg), 0) % cg)
                  >= lax.broadcasted_iota(jnp.int32, (GLA_HEADS * cg, cg), 1))
        o = _gla_intra(carry["att", ci], causal, carry["oi", ci], u[sub(ci), O_V:O_V + HV], cg)
        for hd in range(GLA_HEADS):
            z[sub(ci), D_LRU + hd * GLA_DV:D_LRU + (hd + 1) * GLA_DV] = _gla_output(
                o[hd], u[sub(ci), O_GO + hd * GLA_DV:O_GO + (hd + 1) * GLA_DV], gn_ref[...])

    def gla_begin():
        carry["S"] = s_sc[...]

    def gla_end():
        s_sc[...] = carry["S"]

    lru = {n: functools.partial(f, hf) for hf in range(2)
           for n, f in ((f"front{hf}", lru_front), (f"gmm{hf}", lru_gate_mm), (f"mid{hf}", lru_mid),
                        (f"back{hf}", lru_back))}
    tiles = [functools.partial(in_tile, t) for t in range(D_MAIN // tile_w)]
    assert len(tiles) == 10
    prep = [functools.partial(gla_prep, ci) for ci in range(n_sub)]
    free = [functools.partial(gla_free_mm, ci) for ci in range(n_sub)]
    head = (tiles[0:3] + [lru["front0"], tiles[3], lru["front1"], lru["gmm0"], lru["gmm1"],
                          tiles[4], lru["mid0"], tiles[5], gate_tile, lru["back0"], tiles[6], lru["mid1"],
                          tiles[7], lru["back1"], tiles[8]])
    mid = [prep[0], tiles[9]] + ([prep[1], out_lru] if n_sub > 1 else [out_lru])
    for ci in range(2, n_sub):
        mid += [prep[ci], free[ci - 2]]
    mid += free[max(n_sub - 2, 0):]
    order = (head + mid + [gla_begin] + [functools.partial(gla_state, ci) for ci in range(n_sub)]
             + [gla_end] + [functools.partial(gla_out, ci) for ci in range(n_sub)] + [out_gla])
    for piece in order:
        piece()

    for hf in range(2):
        cs = slice(hf * half, (hf + 1) * half)
        hl_ref[:, cs] = carry["h", hf]
        tailo_ref[:, cs] = carry["tail", hf]
    so_ref[...] = s_sc[...].reshape(GLA_HEADS, GLA_DK, GLA_DV)


def _mixer_seq_call(x, frame0, h0, s0, mix, l, *, nb, sl, tc, cg, row_block0, reset_first,
                    shared_state, name):
    assert sl % tc == 0 and tc % cg == 0 and tc % SUBLANES == 0
    nj = sl // tc
    xrow = pl.BlockSpec((tc, D_MODEL), lambda s: (row_block0 + s, 0))
    seq = lambda s: s // nj
    if shared_state:
        st = lambda s: (0, 0, 0)
        st4 = lambda s: (0, 0, 0, 0)
    else:
        st = lambda s: (seq(s), 0, 0)
        st4 = lambda s: (seq(s), 0, 0, 0)
    (vecs, w_in, wlr, wg2, convw, wg, wout) = mix
    args = [x, frame0, h0, s0, vecs, w_in, wlr, wg2, convw, wg, wout]
    in_specs = [
        xrow,
        pl.BlockSpec((None, SUBLANES, D_LRU), st),
        pl.BlockSpec((None, 1, D_LRU), st),
        pl.BlockSpec((None, GLA_HEADS, GLA_DK, GLA_DV), st4),
        _layer_spec(vecs, l), _MAIN_COLS_SPEC, _layer_spec(wlr, l), _layer_spec(wg2, l),
        _layer_spec(convw, l), _layer_spec(wg, l), _whole_spec(wout),
    ]
    out_shape = [
        jax.ShapeDtypeStruct((nb * sl, D_MODEL), F32),
        jax.ShapeDtypeStruct((nb, 1, D_LRU), F32),
        jax.ShapeDtypeStruct((nb, GLA_HEADS, GLA_DK, GLA_DV), F32),
        jax.ShapeDtypeStruct((nb, SUBLANES, D_LRU), F32),
    ]
    out_specs = [
        pl.BlockSpec((tc, D_MODEL), lambda s: (s, 0)),
        pl.BlockSpec((None, 1, D_LRU), lambda s: (seq(s), 0, 0)),
        pl.BlockSpec((None, GLA_HEADS, GLA_DK, GLA_DV), lambda s: (seq(s), 0, 0, 0)),
        pl.BlockSpec((None, SUBLANES, D_LRU), lambda s: (seq(s), 0, 0)),
    ]
    scratch = [
        pltpu.VMEM((tc, D_U), F32),
        pltpu.VMEM((tc, D_MODEL), F32),
        pltpu.VMEM((SUBLANES, D_LRU), F32),
        pltpu.VMEM((1, D_LRU), F32),
        pltpu.VMEM((HK, GLA_DV), F32),
    ]
    kern = functools.partial(_mixer_seq_kernel, tc=tc, cg=cg, nj=nj, reset_first=reset_first)
    return pl.pallas_call(
        kern, grid=(nb * nj,), in_specs=in_specs, out_specs=out_specs, out_shape=out_shape,
        scratch_shapes=scratch,
        compiler_params=pltpu.CompilerParams(dimension_semantics=("arbitrary",),
                                             vmem_limit_bytes=VMEM_LIMIT),
        name=name,
    )(*args)


def _mixer_dec_kernel(*refs, nb, n_prev):
    (x_ref, frame_ref, h0_ref, s0_ref, vecs_ref, wmain_ref, wlr_ref, wg2_ref, convw_ref, wg_ref,
     wout_ref) = refs[:11]
    gmix_ref, bg_ref, convb_ref, ba_ref, bx_ref, lam_ref, gn_ref = _mixer_vectors(vecs_ref)
    prev_refs = refs[11:11 + n_prev]
    (xo_ref, h_ref, xl_ref, so_ref,
     u_sc, z_sc, qs_sc, ks_sc, ke_sc, el_sc, o_sc) = refs[11 + n_prev:]
    for j, prev in enumerate(prev_refs):
        so_ref[j] = prev[...]
    so_new = so_ref.at[n_prev] if n_prev else so_ref
    ls = SUBLANES
    rows = nb * ls
    x = x_ref[...]
    _in_projection(x, gmix_ref, wmain_ref, wlr_ref, wg2_ref, bg_ref, u_sc)

    t = lax.broadcasted_iota(jnp.int32, (rows, D_LRU), 0) % ls
    xl = u_sc[:, 0:D_LRU]
    xl_ref[...] = xl
    frame = frame_ref[...]
    xc = convb_ref[...]
    for s in range(CONV_W - 1, -1, -1):
        if s == 0:
            sh = xl
        else:
            sh = jnp.where(t >= s, pltpu.roll(xl, s, 0), pltpu.roll(frame, (rows - ls + s) % rows, 0))
        xc = xc + convw_ref[CONV_W - 1 - s:CONV_W - s, :] * sh
    half = D_LRU // 2
    parts = []
    for hf in range(2):
        cs = slice(hf * half, (hf + 1) * half)
        parts.append(_lru_gates(xc[:, cs], _dot(xc[:, cs], wg_ref[hf]), ba_ref[:, cs], bx_ref[:, cs],
                                lam_ref[:, cs]))
    a, mult, ix = (jnp.concatenate([p[n] for p in parts], axis=1) for n in range(3))
    a3, b3 = _scan_groups(a, mult * ix)
    h = (a3 * h0_ref[...].reshape(a3.shape) + b3).reshape(rows, D_LRU)
    h_ref[...] = h
    z_sc[:, 0:D_LRU] = h * jax.nn.gelu(u_sc[:, O_GL:O_GL + D_LRU])

    tq = lax.broadcasted_iota(jnp.int32, (rows, HK), 0) % ls
    b = _cumsum_rows(u_sc[:, D_MAIN:D_U], tq, ls)
    bl = jnp.where(tq == ls - 1, b, 0.0)
    s = 1
    while s < ls:
        bl = bl + jnp.where(tq + s < ls, pltpu.roll(bl, rows - s, 0), 0.0)
        s *= 2
    k = u_sc[:, O_K:O_K + HK]
    qs_sc[...] = (u_sc[:, O_Q:O_Q + HK] * (GLA_DK ** -0.5)) * jnp.exp(b)
    ks_sc[...] = k * jnp.exp(-b)
    ke_sc[...] = k * jnp.exp(bl - b)
    el_sc[...] = jnp.exp(bl)

    lane_head = lax.broadcasted_iota(jnp.int32, (ls, HK), 1) // GLA_DK
    causal = ((lax.broadcasted_iota(jnp.int32, (GLA_HEADS * ls, ls), 0) % ls)
              >= lax.broadcasted_iota(jnp.int32, (GLA_HEADS * ls, ls), 1))

    def body(bi, carry):
        rs = pl.ds(pl.multiple_of(bi * ls, ls), ls)
        s_all = s0_ref[bi].reshape(HK, GLA_DV)
        o, s_new = _gla_chunk(qs_sc[rs, :], ks_sc[rs, :], ke_sc[rs, :], el_sc[rs, :][0:1],
                              u_sc[rs, O_V:O_V + HV], s_all, causal, lane_head, ls)
        for hd in range(GLA_HEADS):
            so_new[bi, hd] = s_new[hd]
            o_sc[rs, hd * GLA_DV:(hd + 1) * GLA_DV] = o[hd]
        return carry

    lax.fori_loop(0, nb, body, 0, unroll=8)

    gn = gn_ref[...]
    for hd in range(GLA_HEADS):
        vs = slice(hd * GLA_DV, (hd + 1) * GLA_DV)
        z_sc[:, D_LRU + hd * GLA_DV:D_LRU + (hd + 1) * GLA_DV] = _gla_output(
            o_sc[:, vs], u_sc[:, O_GO + hd * GLA_DV:O_GO + (hd + 1) * GLA_DV], gn)
    xo_ref[...] = x + _dot(z_sc[...], wout_ref[...])


def _mixer_dec_call(x, frame, h0, s0, s_prev, mix, l, *, nseq, nb, name):
    ls = SUBLANES
    assert nseq % nb == 0
    rows = nb * ls
    (vecs, w_in, wlr, wg2, convw, wg, wout) = mix
    rowspec = lambda w: pl.BlockSpec((rows, w), lambda i: (i, 0))
    sspec = pl.BlockSpec((None, nb, GLA_HEADS, GLA_DK, GLA_DV), lambda i: (l, i, 0, 0, 0))
    in_specs = [
        rowspec(D_MODEL), rowspec(D_LRU), rowspec(D_LRU), sspec,
        _layer_spec(vecs, l), _MAIN_COLS_SPEC, _layer_spec(wlr, l), _layer_spec(wg2, l),
        _layer_spec(convw, l), _layer_spec(wg, l), _whole_spec(wout),
    ]
    out_shape = [
        jax.ShapeDtypeStruct((nseq * ls, D_MODEL), F32),
        jax.ShapeDtypeStruct((nseq * ls, D_LRU), F32),
        jax.ShapeDtypeStruct((nseq * ls, D_LRU), F32),
    ]
    state = (nseq, GLA_HEADS, GLA_DK, GLA_DV)
    sblock = pl.BlockSpec((nb,) + state[1:], lambda i: (i, 0, 0, 0))
    n_prev = len(s_prev)
    if n_prev:
        out_shape.append(jax.ShapeDtypeStruct((n_prev + 1,) + state, F32))
        s_out_spec = pl.BlockSpec((n_prev + 1, nb) + state[1:], lambda i: (0, i, 0, 0, 0))
    else:
        out_shape.append(jax.ShapeDtypeStruct(state, F32))
        s_out_spec = sblock
    out_specs = [rowspec(D_MODEL), rowspec(D_LRU), rowspec(D_LRU), s_out_spec]
    args = [x, frame, h0, s0, vecs, w_in, wlr, wg2, convw, wg, wout]
    args += list(s_prev)
    in_specs += [sblock] * n_prev
    scratch = [
        pltpu.VMEM((rows, D_U), F32), pltpu.VMEM((rows, D_MODEL), F32),
        pltpu.VMEM((rows, HK), F32), pltpu.VMEM((rows, HK), F32), pltpu.VMEM((rows, HK), F32),
        pltpu.VMEM((rows, HK), F32), pltpu.VMEM((rows, HV), F32),
    ]
    return pl.pallas_call(
        functools.partial(_mixer_dec_kernel, nb=nb, n_prev=n_prev),
        grid=(nseq // nb,), in_specs=in_specs, out_specs=out_specs, out_shape=out_shape,
        scratch_shapes=scratch,
        compiler_params=pltpu.CompilerParams(dimension_semantics=("arbitrary",),
                                             vmem_limit_bytes=VMEM_LIMIT),
        name=name,
    )(*args)


def _block_diag_gates(wa, wx):
    per = LRU_BLOCKS // 2
    bw = wa.shape[-1]
    eye = jnp.eye(per, dtype=wa.dtype)

    def bd(w):
        return jnp.einsum("lncd,nm->lncmd", w, eye).reshape(w.shape[0], per * bw, per * bw)

    halves = [jnp.concatenate([bd(wa[:, hf * per:(hf + 1) * per]), bd(wx[:, hf * per:(hf + 1) * per])],
                              axis=-1) for hf in range(2)]
    return jnp.stack(halves, axis=1)


def kernel(x_prompt, x_sample, state_lru_h, state_lru_conv, state_gla_S, meta, norm_ffn1, w_ffn1_gu,
           w_ffn1_down, norm_mix, w_in, lru_conv_w, lru_conv_b, lru_wa, lru_ba, lru_wx, lru_bx,
           lru_lambda, gla_w_gate2, gla_b_gate, gla_norm, w_out, norm_ffn2, w_ffn2_gu, w_ffn2_down,
           norm_final):
    bp, lp, d = x_prompt.shape
    bs, ls, _ = x_sample.shape
    n_meta = meta.shape[0]
    depth = w_in.shape[0]
    assert d == D_MODEL and ls == SUBLANES and n_meta % SUBLANES == 0
    tp, ts = bp * lp, bs * ls
    tsm = ts + n_meta
    assert ts % n_meta == 0
    tm_p = 512 if tp % 512 == 0 else lp
    tm_s = tsm // 2 if (tsm // 2) % SUBLANES == 0 else tsm
    tc_p = 1024 if lp % 1024 == 0 else GLA_CHUNK
    nb_s = 32 if bs % 32 == 0 else bs
    tail = CONV_W - 1

    vecs = _pack_vectors(norm_mix, lru_ba, lru_bx, lru_lambda, lru_conv_b, gla_b_gate, gla_norm,
                         norm_ffn1, norm_ffn2, norm_final)
    w_lr = jnp.pad(w_in[:, :, D_MAIN:], ((0, 0), (0, 0), (0, LANES - GLA_RANK))).astype(BF16)
    w_g2 = jnp.pad(gla_w_gate2, ((0, 0), (0, LANES - GLA_RANK), (0, 0))).astype(BF16)
    w_gates = _block_diag_gates(lru_wa, lru_wx).astype(BF16)
    frames_s = jnp.pad(state_lru_conv, ((0, 0), (0, 0), (SUBLANES - tail, 0), (0, 0)))
    frames_s = frames_s.reshape(depth, ts, D_LRU)
    h0_s = jnp.broadcast_to(state_lru_h[:, :, None, :], (depth, bs, ls, D_LRU)).reshape(depth, ts, D_LRU)
    zero_frame = jnp.zeros((1, SUBLANES, D_LRU), F32)
    zero_h = jnp.zeros((1, 1, D_LRU), F32)
    zero_s = jnp.zeros((1, GLA_HEADS, GLA_DK, GLA_DV), F32)

    xp = x_prompt.reshape(tp, d)
    xs = jnp.concatenate([x_sample.reshape(ts, d), meta.astype(F32)], axis=0)

    hs_p, convs_p, ss_p, hs_s, convs_s, ss_s = [], [], [], [], [], []
    for l in range(depth):
        xp1, xs1, w_in_l, w_out_l = _ffn_call(xp, xs, vecs, ROW_FFN1, w_ffn1_gu, w_ffn1_down, l, final=False,
                                              tm_a=tm_p, tm_b=tm_s, tf=256, name=f"ffn1_{l}",
                                              cast=(w_in, w_out))
        mix = (vecs, w_in_l, w_lr, w_g2, lru_conv_w, w_gates, w_out_l)

        last = l == depth - 1
        xs2, h_s, xl_s, s_s = _mixer_dec_call(xs1, frames_s[l], h0_s[l], state_gla_S,
                                              ss_s if last else (), mix, l, nseq=bs, nb=nb_s,
                                              name=f"mix_s{l}")
        ss_s.append(s_s)
        hs_s.append(h_s.reshape(bs, ls, D_LRU)[:, ls - 1])
        convs_s.append(xl_s.reshape(bs, ls, D_LRU)[:, ls - tail:])

        xm2, h_m, s_m, tail_m = _mixer_seq_call(
            xs1, zero_frame, zero_h, zero_s, mix, l, nb=1, sl=n_meta, tc=n_meta, cg=n_meta,
            row_block0=ts // n_meta, reset_first=True, shared_state=False, name=f"mix_m{l}")
        xs2 = jnp.concatenate([xs2, xm2], axis=0)

        xp2, h_p, s_p, tail_p = _mixer_seq_call(
            xp1, tail_m, h_m, s_m, mix, l, nb=bp, sl=lp, tc=tc_p, cg=GLA_CHUNK, row_block0=0,
            reset_first=False, shared_state=True, name=f"mix_p{l}")
        hs_p.append(h_p[:, 0])
        convs_p.append(tail_p[:, SUBLANES - tail:])
        ss_p.append(s_p)

        xp, xs = _ffn_call(xp2, xs2, vecs, ROW_FFN2, w_ffn2_gu, w_ffn2_down, l, final=last, tm_a=tm_p,
                           tm_b=tm_s, tf=256, name=f"ffn2_{l}")

    s_stack = ss_s[-1] if depth > 1 else ss_s[0][None]
    return (xp.reshape(bp, lp, d), xs[:ts].reshape(bs, ls, d),
            jnp.stack(hs_p), jnp.stack(convs_p), jnp.stack(ss_p),
            jnp.stack(hs_s), jnp.stack(convs_s), s_stack)
```

```python
import functools

import jax
import jax.numpy as jnp
from jax import lax
from jax.experimental import pallas as pl
from jax.experimental.pallas import tpu as pltpu

F32 = jnp.float32
BF16 = jnp.bfloat16

D_MODEL = 1024
D_FF = 2816
D_LRU = 512
LRU_BLOCKS = 8
CONV_W = 4
LRU_C = 8.0
GLA_HEADS = 4
GLA_DV = 128
GLA_DK = 64
GLA_RANK = 16
GLA_GATE_NORM = 16.0
GLA_CHUNK = 64
EPS = 1e-6
HK = GLA_HEADS * GLA_DK
HV = GLA_HEADS * GLA_DV
O_GL, O_Q, O_K, O_V, O_GO = D_LRU, 2 * D_LRU, 2 * D_LRU + HK, 2 * D_LRU + 2 * HK, 2 * D_LRU + 2 * HK + HV
D_MAIN = O_GO + HV
D_U = D_MAIN + HK
SUBLANES = 8
LANES = 128
VMEM_LIMIT = 58 * 1024 * 1024
ROW_MIX, ROW_GATE_B, ROW_LAM_CONVB, ROW_GLA, ROW_FFN1, ROW_FFN2, ROW_FINAL = range(7)


def _dot(a, b):
    if a.dtype != b.dtype:
        a = a.astype(b.dtype)
    return jnp.dot(a, b, preferred_element_type=F32)


def _dot_nt(a, b):
    return lax.dot_general(a, b, (((1,), (1,)), ((), ())), preferred_element_type=F32)


def _dot_tn(a, b):
    return lax.dot_general(a, b, (((0,), (0,)), ((), ())), preferred_element_type=F32)


def _rms(x, g):
    ms = jnp.mean(x * x, axis=-1, keepdims=True)
    return (x * lax.rsqrt(ms + EPS)) * g


def _sigmoid(x):
    return 0.5 * jnp.tanh(0.5 * x) + 0.5


def _silu(x):
    return x * _sigmoid(x)


def _log_sigmoid(x):
    return jnp.minimum(x, 0.0) - jnp.log(1.0 + jnp.exp(-jnp.abs(x)))


def _pack_vectors(norm_mix, ba, bx, lam, conv_b, b_gate, gla_norm, norm_ffn1, norm_ffn2, norm_final):
    depth, d = norm_mix.shape
    rows = [norm_mix, jnp.concatenate([ba, bx], axis=1), jnp.concatenate([lam, conv_b], axis=1),
            jnp.pad(jnp.concatenate([b_gate, gla_norm], axis=1), ((0, 0), (0, d - HK - GLA_DV))),
            norm_ffn1, norm_ffn2, jnp.broadcast_to(norm_final[None, :], (depth, d)),
            jnp.zeros((depth, d), F32)]
    return jnp.stack(rows, axis=1)


def _mixer_vectors(vecs_ref):
    row = lambda r, a, b: vecs_ref.at[r:r + 1, a:b]
    return (row(ROW_MIX, 0, D_MODEL), row(ROW_GLA, 0, HK), row(ROW_LAM_CONVB, D_LRU, 2 * D_LRU),
            row(ROW_GATE_B, 0, D_LRU), row(ROW_GATE_B, D_LRU, 2 * D_LRU), row(ROW_LAM_CONVB, 0, D_LRU),
            row(ROW_GLA, HK, HK + GLA_DV))


def _layer_spec(a, l):
    nd = a.ndim
    return pl.BlockSpec((None,) + a.shape[1:], lambda *_: (l,) + (0,) * (nd - 1),
                        pipeline_mode=pl.Buffered(1))


def _whole_spec(a):
    nd = a.ndim
    return pl.BlockSpec(a.shape, lambda *_: (0,) * nd, pipeline_mode=pl.Buffered(1))


def _main_cols_spec(l):
    return pl.BlockSpec((None, D_MODEL, D_MAIN), lambda *_: (l, 0, 0), pipeline_mode=pl.Buffered(1))


def _ffn_kernel(*refs, tf, norm_row, has_final, n_a, n_cast):
    xa_ref, xb_ref, vecs_ref, wgu_ref, wd_ref = refs[:5]
    cast_src = refs[5:5 + n_cast]
    oa_ref, ob_ref = refs[5 + n_cast:7 + n_cast]
    cast_dst = refs[7 + n_cast:]
    g_ref = vecs_ref.at[norm_row:norm_row + 1, :]
    gfin_ref = vecs_ref.at[ROW_FINAL:ROW_FINAL + 1, :]

    def tile(x_ref, xo_ref):
        x = x_ref[...]
        xn = _rms(x, g_ref[...])
        acc = None
        for c in range(D_FF // tf):
            gate = _dot(xn, wgu_ref[:, c * tf:(c + 1) * tf])
            up = _dot(xn, wgu_ref[:, D_FF + c * tf:D_FF + (c + 1) * tf])
            d = _dot(_silu(gate) * up, wd_ref[c * tf:(c + 1) * tf, :])
            acc = d if acc is None else acc + d
        x = x + 0.5 * acc
        xo_ref[...] = _rms(x, gfin_ref[...]) if has_final else x

    i = pl.program_id(0)

    @pl.when(i < n_a)
    def _():
        tile(xa_ref, oa_ref)
        for s_ref, d_ref in zip(cast_src, cast_dst):
            d_ref[...] = s_ref[...].astype(d_ref.dtype)

    @pl.when(i >= n_a)
    def _():
        tile(xb_ref, ob_ref)


def _ffn_call(xa, xb, vecs, norm_row, wgu, wd, l, *, final, tm_a, tm_b, tf, name, cast=()):
    (ta, d), (tb, _) = xa.shape, xb.shape
    assert ta % tm_a == 0 and tb % tm_b == 0 and D_FF % tf == 0
    n_a, n_b = ta // tm_a, tb // tm_b
    rows_a = pl.BlockSpec((tm_a, d), lambda i: (jnp.minimum(i, n_a - 1), 0))
    rows_b = pl.BlockSpec((tm_b, d), lambda i: (jnp.maximum(i - n_a, 0), 0))
    args = [xa, xb, vecs, wgu, wd] + list(cast)
    specs = [rows_a, rows_b, _layer_spec(vecs, l), _layer_spec(wgu, l), _layer_spec(wd, l)]
    out_specs = [rows_a, rows_b]
    out_shape = [jax.ShapeDtypeStruct((ta, d), F32), jax.ShapeDtypeStruct((tb, d), F32)]
    for w in cast:
        _, r, c = w.shape
        assert r % n_a == 0 and (r // n_a) % (2 * SUBLANES) == 0
        slab = lambda i: (jnp.minimum(i, n_a - 1), 0)
        specs.append(pl.BlockSpec((None, r // n_a, c), lambda i: (l, jnp.minimum(i, n_a - 1), 0)))
        out_specs.append(pl.BlockSpec((r // n_a, c), slab))
        out_shape.append(jax.ShapeDtypeStruct((r, c), BF16))
    return pl.pallas_call(
        functools.partial(_ffn_kernel, tf=tf, norm_row=norm_row, has_final=final, n_a=n_a,
                          n_cast=len(cast)),
        grid=(n_a + n_b,), in_specs=specs, out_specs=out_specs, out_shape=out_shape,
        compiler_params=pltpu.CompilerParams(dimension_semantics=("arbitrary",),
                                             vmem_limit_bytes=VMEM_LIMIT),
        name=name,
    )(*args)


def _in_projection(x, gmix_ref, wmain_ref, wlr_ref, wg2_ref, bg_ref, u_sc):
    hn = _rms(x, gmix_ref[...]).astype(wmain_ref.dtype)
    u_sc[:, 0:D_MAIN] = _dot(hn, wmain_ref[...])
    lr = _dot(hn, wlr_ref[...])
    zg = _dot(lr, wg2_ref[...]) + bg_ref[...]
    u_sc[:, D_MAIN:D_U] = _log_sigmoid(zg) / GLA_GATE_NORM


def _lru_gates(xc, p, ba, bx, lam):
    half = xc.shape[1]
    r = _sigmoid(p[:, 0:half] + ba)
    i = _sigmoid(p[:, half:] + bx)
    log_a = -LRU_C * r * jax.nn.softplus(-lam)
    a = jnp.exp(log_a)
    y = -jnp.tanh(log_a) * (a * a + 1.0)
    mult = jnp.where(y > 0.0, y * lax.rsqrt(y), 0.0)
    return a, mult, i * xc


def _scan_groups(a, b):
    rows, cols = a.shape
    a3 = a.reshape(rows // SUBLANES, SUBLANES, cols)
    b3 = b.reshape(rows // SUBLANES, SUBLANES, cols)
    t = lax.broadcasted_iota(jnp.int32, a3.shape, 1)
    s = 1
    while s < SUBLANES:
        keep = t >= s
        b3 = jnp.where(keep, a3 * pltpu.roll(b3, s, 1) + b3, b3)
        a3 = jnp.where(keep, a3 * pltpu.roll(a3, s, 1), a3)
        s *= 2
    return a3, b3


def _scan_rows(a, b, h0):
    a3, b3 = _scan_groups(a, b)
    carry, out = h0, []
    for g in range(a3.shape[0]):
        hg = a3[g] * carry + b3[g]
        out.append(hg)
        carry = hg[SUBLANES - 1:SUBLANES]
    return out[0] if len(out) == 1 else jnp.concatenate(out, axis=0)


def _cumsum_rows(x, t, n):
    s = 1
    while s < n:
        x = x + jnp.where(t >= s, pltpu.roll(x, s, 0), 0.0)
        s *= 2
    return x


def _head_stack(q_s, lane_head):
    return jnp.concatenate([jnp.where(lane_head == hd, q_s, 0.0) for hd in range(GLA_HEADS)], axis=0)


def _gla_kv(k_end, v, per_head):
    if per_head:
        return [_dot_tn(k_end[:, hd * GLA_DK:(hd + 1) * GLA_DK], v[:, hd * GLA_DV:(hd + 1) * GLA_DV])
                for hd in range(GLA_HEADS)]
    out = []
    for pr in range(GLA_HEADS // 2):
        kv = _dot_tn(k_end[:, 2 * pr * GLA_DK:2 * (pr + 1) * GLA_DK], v[:, 2 * pr * GLA_DV:2 * (pr + 1) * GLA_DV])
        out += [kv[0:GLA_DK, 0:GLA_DV], kv[GLA_DK:2 * GLA_DK, GLA_DV:2 * GLA_DV]]
    return out


def _state_decay(el):
    return jnp.broadcast_to(el, (LANES, HK)).T


def _gla_next_state(el_t, s_all, kv):
    return [el_t[hd * GLA_DK:(hd + 1) * GLA_DK] * s_all[hd * GLA_DK:(hd + 1) * GLA_DK] + kv[hd]
            for hd in range(GLA_HEADS)]


def _gla_intra(att_raw, causal, o_inter, v, n):
    att = jnp.where(causal, att_raw, 0.0)
    return [o_inter[hd * n:(hd + 1) * n] + _dot(att[hd * n:(hd + 1) * n], v[:, hd * GLA_DV:(hd + 1) * GLA_DV])
            for hd in range(GLA_HEADS)]


def _gla_chunk(q_s, k_s, k_end, el, v, s_all, causal, lane_head, n):
    stack = _head_stack(q_s, lane_head)
    o_inter = _dot(stack, s_all)
    att_raw = _dot_nt(stack, k_s)
    kv = _gla_kv(k_end, v, per_head=False)
    el_t = _state_decay(el)
    return _gla_intra(att_raw, causal, o_inter, v, n), _gla_next_state(el_t, s_all, kv)


def _gla_output(o, go, gn):
    return _rms(o, gn) * _silu(go)


def _mixer_seq_kernel(*refs, tc, cg, nj, reset_first):
    (xin_ref, frame0_ref, h0_ref, s0_ref, vecs_ref, wmain_ref, wlr_ref, wg2_ref, convw_ref, wg_ref,
     wout_ref) = refs[:11]
    gmix_ref, bg_ref, convb_ref, ba_ref, bx_ref, lam_ref, gn_ref = _mixer_vectors(vecs_ref)
    xo_ref, hl_ref, so_ref, tailo_ref, u, z, tail_sc, h_sc, s_sc = refs[11:]
    c = pl.program_id(0)

    @pl.when(c % nj == 0)
    def _():
        tail_sc[...] = frame0_ref[...]
        h_sc[...] = h0_ref[...]
        s_sc[...] = s0_ref[...].reshape(HK, GLA_DV)

    x = xin_ref[...]
    hn = _rms(x, gmix_ref[...]).astype(wmain_ref.dtype)
    tile_w = 2 * LANES
    carry = {}
    half = D_LRU // 2

    def in_tile(t):
        cols = slice(t * tile_w, (t + 1) * tile_w)
        u[:, cols] = _dot(hn, wmain_ref[:, cols])

    def gate_tile():
        lr = _dot(hn, wlr_ref[...])
        zg = _dot(lr, wg2_ref[...]) + bg_ref[...]
        u[:, D_MAIN:D_U] = _log_sigmoid(zg) / GLA_GATE_NORM

    def out_lru():
        xo_ref[...] = x + _dot(z[:, 0:D_LRU], wout_ref[0:D_LRU, :])

    def out_gla():
        xo_ref[...] += _dot(z[:, D_LRU:D_MODEL], wout_ref[D_LRU:D_MODEL, :])

    def lru_front(hf):
        cs = slice(hf * half, (hf + 1) * half)
        xl = u[:, cs]
        row8 = lax.broadcasted_iota(jnp.int32, (SUBLANES, half), 0)
        xc = convb_ref[:, cs]
        for sft in range(CONV_W - 1, -1, -1):
            if sft == 0:
                sh = xl
            else:
                rolled = pltpu.roll(xl, sft, 0)
                top = jnp.where(row8 < sft, pltpu.roll(tail_sc[:, cs], sft, 0), rolled[0:SUBLANES])
                sh = top if tc == SUBLANES else jnp.concatenate([top, rolled[SUBLANES:]], axis=0)
            xc = xc + convw_ref[CONV_W - 1 - sft:CONV_W - sft, cs] * sh
        carry["tail", hf] = xl[tc - SUBLANES:tc]
        tail_sc[:, cs] = carry["tail", hf]
        carry["xc", hf] = xc

    def lru_gate_mm(hf):
        carry["p", hf] = _dot(carry["xc", hf], wg_ref[hf])

    def lru_mid(hf):
        cs = slice(hf * half, (hf + 1) * half)
        a, mult, ix = _lru_gates(carry["xc", hf], carry["p", hf], ba_ref[:, cs], bx_ref[:, cs],
                                 lam_ref[:, cs])
        if reset_first:
            row = lax.broadcasted_iota(jnp.int32, (tc, half), 0)
            first = jnp.logical_and(row == 0, c % nj == 0)
            mult = jnp.where(first, 1.0, mult)
            a = jnp.where(first, 0.0, a)
        carry["a", hf], carry["b", hf] = a, mult * ix

    def lru_back(hf):
        cs = slice(hf * half, (hf + 1) * half)
        h = _scan_rows(carry["a", hf], carry["b", hf], h_sc[:, cs])
        carry["h", hf] = h[tc - 1:tc]
        h_sc[:, cs] = carry["h", hf]
        z[:, cs] = h * jax.nn.gelu(u[:, O_GL + hf * half:O_GL + (hf + 1) * half])

    lane_head = lax.broadcasted_iota(jnp.int32, (cg, HK), 1) // GLA_DK
    n_sub = tc // cg
    sub = lambda ci: slice(ci * cg, (ci + 1) * cg)

    def gla_prep(ci):
        tq = lax.broadcasted_iota(jnp.int32, (cg, HK), 0)
        b = _cumsum_rows(u[sub(ci), D_MAIN:D_U], tq, cg)
        bl = b[cg - 1:cg]
        k = u[sub(ci), O_K:O_K + HK]
        carry["stack", ci] = _head_stack(u[sub(ci), O_Q:O_Q + HK] * (GLA_DK ** -0.5) * jnp.exp(b),
                                         lane_head)
        carry["ks", ci], carry["ke", ci], carry["el", ci] = k * jnp.exp(-b), k * jnp.exp(bl - b), jnp.exp(bl)

    def gla_free_mm(ci):
        carry["att", ci] = _dot_nt(carry["stack", ci], carry["ks", ci])
        carry["kv", ci] = _gla_kv(carry["ke", ci], u[sub(ci), O_V:O_V + HV], per_head=True)
        carry["el_t", ci] = _state_decay(carry["el", ci])

    def gla_state(ci):
        s_all = carry["S"]
        carry["oi", ci] = _dot(carry["stack", ci], s_all)
        carry["S"] = jnp.concatenate(_gla_next_state(carry["el_t", ci], s_all, carry["kv", ci]), axis=0)

    def gla_out(ci):
        causal = ((lax.broadcasted_iota(jnp.int32, (GLA_HEADS * cg, cg), 0) % cg)
                  >= lax.broadcasted_iota(jnp.int32, (GLA_HEADS * cg, cg), 1))
        o = _gla_intra(carry["att", ci], causal, carry["oi", ci], u[sub(ci), O_V:O_V + HV], cg)
        for hd in range(GLA_HEADS):
            z[sub(ci), D_LRU + hd * GLA_DV:D_LRU + (hd + 1) * GLA_DV] = _gla_output(
                o[hd], u[sub(ci), O_GO + hd * GLA_DV:O_GO + (hd + 1) * GLA_DV], gn_ref[...])

    def gla_begin():
        carry["S"] = s_sc[...]

    def gla_end():
        s_sc[...] = carry["S"]

    lru = {n: functools.partial(f, hf) for hf in range(2)
           for n, f in ((f"front{hf}", lru_front), (f"gmm{hf}", lru_gate_mm), (f"mid{hf}", lru_mid),
                        (f"back{hf}", lru_back))}
    tiles = [functools.partial(in_tile, t) for t in range(D_MAIN // tile_w)]
    assert len(tiles) == 10
    prep = [functools.partial(gla_prep, ci) for ci in range(n_sub)]
    free = [functools.partial(gla_free_mm, ci) for ci in range(n_sub)]
    head = (tiles[0:3] + [lru["front0"], tiles[3], lru["front1"], lru["gmm0"], lru["gmm1"],
                          tiles[4], lru["mid0"], tiles[5], gate_tile, lru["back0"], tiles[6], lru["mid1"],
                          tiles[7], lru["back1"], tiles[8]])
    mid = [prep[0], tiles[9]] + ([prep[1], out_lru] if n_sub > 1 else [out_lru])
    for ci in range(2, n_sub):
        mid += [prep[ci], free[ci - 2]]
    mid += free[max(n_sub - 2, 0):]
    order = (head + mid + [gla_begin] + [functools.partial(gla_state, ci) for ci in range(n_sub)]
             + [gla_end] + [functools.partial(gla_out, ci) for ci in range(n_sub)] + [out_gla])
    for piece in order:
        piece()

    for hf in range(2):
        cs = slice(hf * half, (hf + 1) * half)
        hl_ref[:, cs] = carry["h", hf]
        tailo_ref[:, cs] = carry["tail", hf]
    so_ref[...] = s_sc[...].reshape(GLA_HEADS, GLA_DK, GLA_DV)


def _mixer_seq_call(x, frame0, h0, s0, mix, l, *, nb, sl, tc, cg, row_block0, reset_first,
                    shared_state, name):
    assert sl % tc == 0 and tc % cg == 0 and tc % SUBLANES == 0
    nj = sl // tc
    xrow = pl.BlockSpec((tc, D_MODEL), lambda s: (row_block0 + s, 0))
    seq = lambda s: s // nj
    if shared_state:
        st = lambda s: (0, 0, 0)
        st4 = lambda s: (0, 0, 0, 0)
    else:
        st = lambda s: (seq(s), 0, 0)
        st4 = lambda s: (seq(s), 0, 0, 0)
    (vecs, w_in, wlr, wg2, convw, wg, wout) = mix
    args = [x, frame0, h0, s0, vecs, w_in, wlr, wg2, convw, wg, wout]
    in_specs = [
        xrow,
        pl.BlockSpec((None, SUBLANES, D_LRU), st),
        pl.BlockSpec((None, 1, D_LRU), st),
        pl.BlockSpec((None, GLA_HEADS, GLA_DK, GLA_DV), st4),
        _layer_spec(vecs, l), _main_cols_spec(l), _layer_spec(wlr, l), _layer_spec(wg2, l),
        _layer_spec(convw, l), _layer_spec(wg, l), _whole_spec(wout),
    ]
    out_shape = [
        jax.ShapeDtypeStruct((nb * sl, D_MODEL), F32),
        jax.ShapeDtypeStruct((nb, 1, D_LRU), F32),
        jax.ShapeDtypeStruct((nb, GLA_HEADS, GLA_DK, GLA_DV), F32),
        jax.ShapeDtypeStruct((nb, SUBLANES, D_LRU), F32),
    ]
    out_specs = [
        pl.BlockSpec((tc, D_MODEL), lambda s: (s, 0)),
        pl.BlockSpec((None, 1, D_LRU), lambda s: (seq(s), 0, 0)),
        pl.BlockSpec((None, GLA_HEADS, GLA_DK, GLA_DV), lambda s: (seq(s), 0, 0, 0)),
        pl.BlockSpec((None, SUBLANES, D_LRU), lambda s: (seq(s), 0, 0)),
    ]
    scratch = [
        pltpu.VMEM((tc, D_U), F32),
        pltpu.VMEM((tc, D_MODEL), F32),
        pltpu.VMEM((SUBLANES, D_LRU), F32),
        pltpu.VMEM((1, D_LRU), F32),
        pltpu.VMEM((HK, GLA_DV), F32),
    ]
    kern = functools.partial(_mixer_seq_kernel, tc=tc, cg=cg, nj=nj, reset_first=reset_first)
    return pl.pallas_call(
        kern, grid=(nb * nj,), in_specs=in_specs, out_specs=out_specs, out_shape=out_shape,
        scratch_shapes=scratch,
        compiler_params=pltpu.CompilerParams(dimension_semantics=("arbitrary",),
                                             vmem_limit_bytes=VMEM_LIMIT),
        name=name,
    )(*args)


def _mixer_dec_kernel(*refs, nb, n_prev):
    (x_ref, frame_ref, h0_ref, s0_ref, vecs_ref, wmain_ref, wlr_ref, wg2_ref, convw_ref, wg_ref,
     wout_ref) = refs[:11]
    gmix_ref, bg_ref, convb_ref, ba_ref, bx_ref, lam_ref, gn_ref = _mixer_vectors(vecs_ref)
    prev_refs = refs[11:11 + n_prev]
    (xo_ref, h_ref, xl_ref, so_ref,
     u_sc, z_sc, qs_sc, ks_sc, ke_sc, el_sc, o_sc) = refs[11 + n_prev:]
    for j, prev in enumerate(prev_refs):
        so_ref[j] = prev[...]
    so_new = so_ref.at[n_prev] if n_prev else so_ref
    ls = SUBLANES
    rows = nb * ls
    x = x_ref[...]
    _in_projection(x, gmix_ref, wmain_ref, wlr_ref, wg2_ref, bg_ref, u_sc)

    t = lax.broadcasted_iota(jnp.int32, (rows, D_LRU), 0) % ls
    xl = u_sc[:, 0:D_LRU]
    xl_ref[...] = xl
    frame = frame_ref[...]
    xc = convb_ref[...]
    for s in range(CONV_W - 1, -1, -1):
        if s == 0:
            sh = xl
        else:
            sh = jnp.where(t >= s, pltpu.roll(xl, s, 0), pltpu.roll(frame, (rows - ls + s) % rows, 0))
        xc = xc + convw_ref[CONV_W - 1 - s:CONV_W - s, :] * sh
    half = D_LRU // 2
    parts = []
    for hf in range(2):
        cs = slice(hf * half, (hf + 1) * half)
        parts.append(_lru_gates(xc[:, cs], _dot(xc[:, cs], wg_ref[hf]), ba_ref[:, cs], bx_ref[:, cs],
                                lam_ref[:, cs]))
    a, mult, ix = (jnp.concatenate([p[n] for p in parts], axis=1) for n in range(3))
    a3, b3 = _scan_groups(a, mult * ix)
    h = (a3 * h0_ref[...].reshape(a3.shape) + b3).reshape(rows, D_LRU)
    h_ref[...] = h
    z_sc[:, 0:D_LRU] = h * jax.nn.gelu(u_sc[:, O_GL:O_GL + D_LRU])

    tq = lax.broadcasted_iota(jnp.int32, (rows, HK), 0) % ls
    b = _cumsum_rows(u_sc[:, D_MAIN:D_U], tq, ls)
    bl = jnp.where(tq == ls - 1, b, 0.0)
    s = 1
    while s < ls:
        bl = bl + jnp.where(tq + s < ls, pltpu.roll(bl, rows - s, 0), 0.0)
        s *= 2
    k = u_sc[:, O_K:O_K + HK]
    qs_sc[...] = (u_sc[:, O_Q:O_Q + HK] * (GLA_DK ** -0.5)) * jnp.exp(b)
    ks_sc[...] = k * jnp.exp(-b)
    ke_sc[...] = k * jnp.exp(bl - b)
    el_sc[...] = jnp.exp(bl)

    lane_head = lax.broadcasted_iota(jnp.int32, (ls, HK), 1) // GLA_DK
    causal = ((lax.broadcasted_iota(jnp.int32, (GLA_HEADS * ls, ls), 0) % ls)
              >= lax.broadcasted_iota(jnp.int32, (GLA_HEADS * ls, ls), 1))

    def body(bi, carry):
        rs = pl.ds(pl.multiple_of(bi * ls, ls), ls)
        s_all = s0_ref[bi].reshape(HK, GLA_DV)
        o, s_new = _gla_chunk(qs_sc[rs, :], ks_sc[rs, :], ke_sc[rs, :], el_sc[rs, :][0:1],
                              u_sc[rs, O_V:O_V + HV], s_all, causal, lane_head, ls)
        for hd in range(GLA_HEADS):
            so_new[bi, hd] = s_new[hd]
            o_sc[rs, hd * GLA_DV:(hd + 1) * GLA_DV] = o[hd]
        return carry

    lax.fori_loop(0, nb, body, 0, unroll=16)

    gn = gn_ref[...]
    for hd in range(GLA_HEADS):
        vs = slice(hd * GLA_DV, (hd + 1) * GLA_DV)
        z_sc[:, D_LRU + hd * GLA_DV:D_LRU + (hd + 1) * GLA_DV] = _gla_output(
            o_sc[:, vs], u_sc[:, O_GO + hd * GLA_DV:O_GO + (hd + 1) * GLA_DV], gn)
    xo_ref[...] = x + _dot(z_sc[...], wout_ref[...])


def _mixer_dec_call(x, frame, h0, s0, s_prev, mix, l, *, nseq, nb, name):
    ls = SUBLANES
    assert nseq % nb == 0
    rows = nb * ls
    (vecs, w_in, wlr, wg2, convw, wg, wout) = mix
    rowspec = lambda w: pl.BlockSpec((rows, w), lambda i: (i, 0))
    sspec = pl.BlockSpec((None, nb, GLA_HEADS, GLA_DK, GLA_DV), lambda i: (l, i, 0, 0, 0))
    in_specs = [
        rowspec(D_MODEL), rowspec(D_LRU), rowspec(D_LRU), sspec,
        _layer_spec(vecs, l), _main_cols_spec(l), _layer_spec(wlr, l), _layer_spec(wg2, l),
        _layer_spec(convw, l), _layer_spec(wg, l), _whole_spec(wout),
    ]
    out_shape = [
        jax.ShapeDtypeStruct((nseq * ls, D_MODEL), F32),
        jax.ShapeDtypeStruct((nseq * ls, D_LRU), F32),
        jax.ShapeDtypeStruct((nseq * ls, D_LRU), F32),
    ]
    state = (nseq, GLA_HEADS, GLA_DK, GLA_DV)
    sblock = pl.BlockSpec((nb,) + state[1:], lambda i: (i, 0, 0, 0))
    n_prev = len(s_prev)
    if n_prev:
        out_shape.append(jax.ShapeDtypeStruct((n_prev + 1,) + state, F32))
        s_out_spec = pl.BlockSpec((n_prev + 1, nb) + state[1:], lambda i: (0, i, 0, 0, 0))
    else:
        out_shape.append(jax.ShapeDtypeStruct(state, F32))
        s_out_spec = sblock
    out_specs = [rowspec(D_MODEL), rowspec(D_LRU), rowspec(D_LRU), s_out_spec]
    args = [x, frame, h0, s0, vecs, w_in, wlr, wg2, convw, wg, wout]
    args += list(s_prev)
    in_specs += [sblock] * n_prev
    scratch = [
        pltpu.VMEM((rows, D_U), F32), pltpu.VMEM((rows, D_MODEL), F32),
        pltpu.VMEM((rows, HK), F32), pltpu.VMEM((rows, HK), F32), pltpu.VMEM((rows, HK), F32),
        pltpu.VMEM((rows, HK), F32), pltpu.VMEM((rows, HV), F32),
    ]
    return pl.pallas_call(
        functools.partial(_mixer_dec_kernel, nb=nb, n_prev=n_prev),
        grid=(nseq // nb,), in_specs=in_specs, out_specs=out_specs, out_shape=out_shape,
        scratch_shapes=scratch,
        compiler_params=pltpu.CompilerParams(dimension_semantics=("arbitrary",),
                                             vmem_limit_bytes=VMEM_LIMIT),
        name=name,
    )(*args)


def _block_diag_gates(wa, wx):
    per = LRU_BLOCKS // 2
    bw = wa.shape[-1]
    eye = jnp.eye(per, dtype=wa.dtype)

    def bd(w):
        return jnp.einsum("lncd,nm->lncmd", w, eye).reshape(w.shape[0], per * bw, per * bw)

    halves = [jnp.concatenate([bd(wa[:, hf * per:(hf + 1) * per]), bd(wx[:, hf * per:(hf + 1) * per])],
                              axis=-1) for hf in range(2)]
    return jnp.stack(halves, axis=1)


def kernel(x_prompt, x_sample, state_lru_h, state_lru_conv, state_gla_S, meta, norm_ffn1, w_ffn1_gu,
           w_ffn1_down, norm_mix, w_in, lru_conv_w, lru_conv_b, lru_wa, lru_ba, lru_wx, lru_bx,
           lru_lambda, gla_w_gate2, gla_b_gate, gla_norm, w_out, norm_ffn2, w_ffn2_gu, w_ffn2_down,
           norm_final):
    bp, lp, d = x_prompt.shape
    bs, ls, _ = x_sample.shape
    n_meta = meta.shape[0]
    depth = w_in.shape[0]
    assert d == D_MODEL and ls == SUBLANES and n_meta % SUBLANES == 0
    tp, ts = bp * lp, bs * ls
    tsm = ts + n_meta
    assert ts % n_meta == 0
    tm_p = 512 if tp % 512 == 0 else lp
    tm_s = tsm // 2 if (tsm // 2) % SUBLANES == 0 else tsm
    tc_p = 1024 if lp % 1024 == 0 else GLA_CHUNK
    nb_s = 32 if bs % 32 == 0 else bs
    tail = CONV_W - 1

    vecs = _pack_vectors(norm_mix, lru_ba, lru_bx, lru_lambda, lru_conv_b, gla_b_gate, gla_norm,
                         norm_ffn1, norm_ffn2, norm_final)
    w_in_b = w_in.astype(BF16)
    w_lr = jnp.pad(w_in[:, :, D_MAIN:], ((0, 0), (0, 0), (0, LANES - GLA_RANK))).astype(BF16)
    w_g2 = jnp.pad(gla_w_gate2, ((0, 0), (0, LANES - GLA_RANK), (0, 0))).astype(BF16)
    w_gates = _block_diag_gates(lru_wa, lru_wx).astype(BF16)
    frames_s = jnp.pad(state_lru_conv, ((0, 0), (0, 0), (SUBLANES - tail, 0), (0, 0)))
    frames_s = frames_s.reshape(depth, ts, D_LRU)
    h0_s = jnp.broadcast_to(state_lru_h[:, :, None, :], (depth, bs, ls, D_LRU)).reshape(depth, ts, D_LRU)
    zero_frame = jnp.zeros((1, SUBLANES, D_LRU), F32)
    zero_h = jnp.zeros((1, 1, D_LRU), F32)
    zero_s = jnp.zeros((1, GLA_HEADS, GLA_DK, GLA_DV), F32)

    xp = x_prompt.reshape(tp, d)
    xs = jnp.concatenate([x_sample.reshape(ts, d), meta.astype(F32)], axis=0)

    hs_p, convs_p, ss_p, hs_s, convs_s, ss_s = [], [], [], [], [], []
    for l in range(depth):
        xp1, xs1, w_out_l = _ffn_call(xp, xs, vecs, ROW_FFN1, w_ffn1_gu, w_ffn1_down, l, final=False,
                                      tm_a=tm_p, tm_b=tm_s, tf=256, name=f"ffn1_{l}", cast=(w_out,))
        mix = (vecs, w_in_b, w_lr, w_g2, lru_conv_w, w_gates, w_out_l)

        last = l == depth - 1
        xs2, h_s, xl_s, s_s = _mixer_dec_call(xs1, frames_s[l], h0_s[l], state_gla_S,
                                              ss_s if last else (), mix, l, nseq=bs, nb=nb_s,
                                              name=f"mix_s{l}")
        ss_s.append(s_s)
        hs_s.append(h_s.reshape(bs, ls, D_LRU)[:, ls - 1])
        convs_s.append(xl_s.reshape(bs, ls, D_LRU)[:, ls - tail:])

        xm2, h_m, s_m, tail_m = _mixer_seq_call(
            xs1, zero_frame, zero_h, zero_s, mix, l, nb=1, sl=n_meta, tc=n_meta, cg=n_meta,
            row_block0=ts // n_meta, reset_first=True, shared_state=False, name=f"mix_m{l}")
        xs2 = jnp.concatenate([xs2, xm2], axis=0)

        xp2, h_p, s_p, tail_p = _mixer_seq_call(
            xp1, tail_m, h_m, s_m, mix, l, nb=bp, sl=lp, tc=tc_p, cg=GLA_CHUNK, row_block0=0,
            reset_first=False, shared_state=True, name=f"mix_p{l}")
        hs_p.append(h_p[:, 0])
        convs_p.append(tail_p[:, SUBLANES - tail:])
        ss_p.append(s_p)

        xp, xs = _ffn_call(xp2, xs2, vecs, ROW_FFN2, w_ffn2_gu, w_ffn2_down, l, final=last, tm_a=tm_p,
                           tm_b=tm_s, tf=256, name=f"ffn2_{l}")

    s_stack = ss_s[-1] if depth > 1 else ss_s[0][None]
    return (xp.reshape(bp, lp, d), xs[:ts].reshape(bs, ls, d),
            jnp.stack(hs_p), jnp.stack(convs_p), jnp.stack(ss_p),
            jnp.stack(hs_s), jnp.stack(convs_s), s_stack)
```

```python
import functools

import jax
import jax.numpy as jnp
from jax import lax
from jax.experimental import pallas as pl
from jax.experimental.pallas import tpu as pltpu

F32 = jnp.float32
BF16 = jnp.bfloat16

D_MODEL = 1024
D_FF = 2816
D_LRU = 512
LRU_BLOCKS = 8
CONV_W = 4
LRU_C = 8.0
GLA_HEADS = 4
GLA_DV = 128
GLA_DK = 64
GLA_RANK = 16
GLA_GATE_NORM = 16.0
GLA_CHUNK = 64
EPS = 1e-6
HK = GLA_HEADS * GLA_DK
HV = GLA_HEADS * GLA_DV
O_GL, O_Q, O_K, O_V, O_GO = D_LRU, 2 * D_LRU, 2 * D_LRU + HK, 2 * D_LRU + 2 * HK, 2 * D_LRU + 2 * HK + HV
D_MAIN = O_GO + HV
D_U = D_MAIN + HK
SUBLANES = 8
LANES = 128
VMEM_LIMIT = 58 * 1024 * 1024
ROW_MIX, ROW_GATE_B, ROW_LAM_CONVB, ROW_GLA, ROW_FFN1, ROW_FFN2, ROW_FINAL = range(7)


def _dot(a, b):
    if a.dtype != b.dtype:
        a = a.astype(b.dtype)
    return jnp.dot(a, b, preferred_element_type=F32)


def _dot_nt(a, b):
    return lax.dot_general(a, b, (((1,), (1,)), ((), ())), preferred_element_type=F32)


def _dot_tn(a, b):
    return lax.dot_general(a, b, (((0,), (0,)), ((), ())), preferred_element_type=F32)


def _rms(x, g):
    ms = jnp.mean(x * x, axis=-1, keepdims=True)
    return (x * lax.rsqrt(ms + EPS)) * g


def _sigmoid(x):
    return 0.5 * jnp.tanh(0.5 * x) + 0.5


def _silu(x):
    return x * _sigmoid(x)


def _log_sigmoid(x):
    return jnp.minimum(x, 0.0) - jnp.log(1.0 + jnp.exp(-jnp.abs(x)))


def _pack_vectors(norm_mix, ba, bx, lam, conv_b, b_gate, gla_norm, norm_ffn1, norm_ffn2, norm_final):
    depth, d = norm_mix.shape
    rows = [norm_mix, jnp.concatenate([ba, bx], axis=1), jnp.concatenate([lam, conv_b], axis=1),
            jnp.pad(jnp.concatenate([b_gate, gla_norm], axis=1), ((0, 0), (0, d - HK - GLA_DV))),
            norm_ffn1, norm_ffn2, jnp.broadcast_to(norm_final[None, :], (depth, d)),
            jnp.zeros((depth, d), F32)]
    return jnp.stack(rows, axis=1)


def _mixer_vectors(vecs_ref):
    row = lambda r, a, b: vecs_ref.at[r:r + 1, a:b]
    return (row(ROW_MIX, 0, D_MODEL), row(ROW_GLA, 0, HK), row(ROW_LAM_CONVB, D_LRU, 2 * D_LRU),
            row(ROW_GATE_B, 0, D_LRU), row(ROW_GATE_B, D_LRU, 2 * D_LRU), row(ROW_LAM_CONVB, 0, D_LRU),
            row(ROW_GLA, HK, HK + GLA_DV))


def _layer_spec(a, l):
    nd = a.ndim
    return pl.BlockSpec((None,) + a.shape[1:], lambda *_: (l,) + (0,) * (nd - 1),
                        pipeline_mode=pl.Buffered(1))


def _whole_spec(a):
    nd = a.ndim
    return pl.BlockSpec(a.shape, lambda *_: (0,) * nd, pipeline_mode=pl.Buffered(1))


_MAIN_COLS_SPEC = pl.BlockSpec((D_MODEL, D_MAIN), lambda *_: (0, 0), pipeline_mode=pl.Buffered(1))


def _ffn_kernel(*refs, tf, norm_row, has_final, n_a, n_cast):
    xa_ref, xb_ref, vecs_ref, wgu_ref, wd_ref = refs[:5]
    cast_src = refs[5:5 + n_cast]
    oa_ref, ob_ref = refs[5 + n_cast:7 + n_cast]
    cast_dst = refs[7 + n_cast:]
    g_ref = vecs_ref.at[norm_row:norm_row + 1, :]
    gfin_ref = vecs_ref.at[ROW_FINAL:ROW_FINAL + 1, :]

    def tile(x_ref, xo_ref):
        x = x_ref[...]
        xn = _rms(x, g_ref[...])
        acc = None
        for c in range(D_FF // tf):
            gate = _dot(xn, wgu_ref[:, c * tf:(c + 1) * tf])
            up = _dot(xn, wgu_ref[:, D_FF + c * tf:D_FF + (c + 1) * tf])
            d = _dot(_silu(gate) * up, wd_ref[c * tf:(c + 1) * tf, :])
            acc = d if acc is None else acc + d
        x = x + 0.5 * acc
        xo_ref[...] = _rms(x, gfin_ref[...]) if has_final else x

    i = pl.program_id(0)

    @pl.when(i < n_a)
    def _():
        tile(xa_ref, oa_ref)
        for s_ref, d_ref in zip(cast_src, cast_dst):
            d_ref[...] = s_ref[...].astype(d_ref.dtype)

    @pl.when(i >= n_a)
    def _():
        tile(xb_ref, ob_ref)


def _ffn_call(xa, xb, vecs, norm_row, wgu, wd, l, *, final, tm_a, tm_b, tf, name, cast=()):
    (ta, d), (tb, _) = xa.shape, xb.shape
    assert ta % tm_a == 0 and tb % tm_b == 0 and D_FF % tf == 0
    n_a, n_b = ta // tm_a, tb // tm_b
    rows_a = pl.BlockSpec((tm_a, d), lambda i: (jnp.minimum(i, n_a - 1), 0))
    rows_b = pl.BlockSpec((tm_b, d), lambda i: (jnp.maximum(i - n_a, 0), 0))
    args = [xa, xb, vecs, wgu, wd] + list(cast)
    specs = [rows_a, rows_b, _layer_spec(vecs, l), _layer_spec(wgu, l), _layer_spec(wd, l)]
    out_specs = [rows_a, rows_b]
    out_shape = [jax.ShapeDtypeStruct((ta, d), F32), jax.ShapeDtypeStruct((tb, d), F32)]
    for w in cast:
        _, r, c = w.shape
        assert r % n_a == 0 and (r // n_a) % (2 * SUBLANES) == 0
        slab = lambda i: (jnp.minimum(i, n_a - 1), 0)
        specs.append(pl.BlockSpec((None, r // n_a, c), lambda i: (l, jnp.minimum(i, n_a - 1), 0)))
        out_specs.append(pl.BlockSpec((r // n_a, c), slab))
        out_shape.append(jax.ShapeDtypeStruct((r, c), BF16))
    return pl.pallas_call(
        functools.partial(_ffn_kernel, tf=tf, norm_row=norm_row, has_final=final, n_a=n_a,
                          n_cast=len(cast)),
        grid=(n_a + n_b,), in_specs=specs, out_specs=out_specs, out_shape=out_shape,
        compiler_params=pltpu.CompilerParams(dimension_semantics=("arbitrary",),
                                             vmem_limit_bytes=VMEM_LIMIT),
        name=name,
    )(*args)


def _in_projection(x, gmix_ref, wmain_ref, wlr_ref, wg2_ref, bg_ref, u_sc):
    hn = _rms(x, gmix_ref[...]).astype(wmain_ref.dtype)
    u_sc[:, 0:D_MAIN] = _dot(hn, wmain_ref[...])
    lr = _dot(hn, wlr_ref[...])
    zg = _dot(lr, wg2_ref[...]) + bg_ref[...]
    u_sc[:, D_MAIN:D_U] = _log_sigmoid(zg) / GLA_GATE_NORM


def _lru_gates(xc, p, ba, bx, lam):
    half = xc.shape[1]
    r = _sigmoid(p[:, 0:half] + ba)
    i = _sigmoid(p[:, half:] + bx)
    log_a = -LRU_C * r * jax.nn.softplus(-lam)
    a = jnp.exp(log_a)
    y = -jnp.tanh(log_a) * (a * a + 1.0)
    mult = jnp.where(y > 0.0, y * lax.rsqrt(y), 0.0)
    return a, mult, i * xc


def _scan_groups(a, b):
    rows, cols = a.shape
    a3 = a.reshape(rows // SUBLANES, SUBLANES, cols)
    b3 = b.reshape(rows // SUBLANES, SUBLANES, cols)
    t = lax.broadcasted_iota(jnp.int32, a3.shape, 1)
    s = 1
    while s < SUBLANES:
        keep = t >= s
        b3 = jnp.where(keep, a3 * pltpu.roll(b3, s, 1) + b3, b3)
        a3 = jnp.where(keep, a3 * pltpu.roll(a3, s, 1), a3)
        s *= 2
    return a3, b3


def _scan_rows(a, b, h0):
    a3, b3 = _scan_groups(a, b)
    carry, out = h0, []
    for g in range(a3.shape[0]):
        hg = a3[g] * carry + b3[g]
        out.append(hg)
        carry = hg[SUBLANES - 1:SUBLANES]
    return out[0] if len(out) == 1 else jnp.concatenate(out, axis=0)


def _cumsum_rows(x, t, n):
    s = 1
    while s < n:
        x = x + jnp.where(t >= s, pltpu.roll(x, s, 0), 0.0)
        s *= 2
    return x


def _head_stack(q_s, lane_head):
    return jnp.concatenate([jnp.where(lane_head == hd, q_s, 0.0) for hd in range(GLA_HEADS)], axis=0)


def _gla_kv(k_end, v, per_head):
    if per_head:
        return [_dot_tn(k_end[:, hd * GLA_DK:(hd + 1) * GLA_DK], v[:, hd * GLA_DV:(hd + 1) * GLA_DV])
                for hd in range(GLA_HEADS)]
    out = []
    for pr in range(GLA_HEADS // 2):
        kv = _dot_tn(k_end[:, 2 * pr * GLA_DK:2 * (pr + 1) * GLA_DK], v[:, 2 * pr * GLA_DV:2 * (pr + 1) * GLA_DV])
        out += [kv[0:GLA_DK, 0:GLA_DV], kv[GLA_DK:2 * GLA_DK, GLA_DV:2 * GLA_DV]]
    return out


def _state_decay(el):
    return jnp.broadcast_to(el, (LANES, HK)).T


def _gla_next_state(el_t, s_all, kv):
    return [el_t[hd * GLA_DK:(hd + 1) * GLA_DK] * s_all[hd * GLA_DK:(hd + 1) * GLA_DK] + kv[hd]
            for hd in range(GLA_HEADS)]


def _gla_intra(att_raw, causal, o_inter, v, n):
    att = jnp.where(causal, att_raw, 0.0)
    return [o_inter[hd * n:(hd + 1) * n] + _dot(att[hd * n:(hd + 1) * n], v[:, hd * GLA_DV:(hd + 1) * GLA_DV])
            for hd in range(GLA_HEADS)]


def _gla_chunk(q_s, k_s, k_end, el, v, s_all, causal, lane_head, n):
    stack = _head_stack(q_s, lane_head)
    o_inter = _dot(stack, s_all)
    att_raw = _dot_nt(stack, k_s)
    kv = _gla_kv(k_end, v, per_head=False)
    el_t = _state_decay(el)
    return _gla_intra(att_raw, causal, o_inter, v, n), _gla_next_state(el_t, s_all, kv)


def _gla_output(o, go, gn):
    return _rms(o, gn) * _silu(go)


def _mixer_seq_kernel(*refs, tc, cg, nj, reset_first):
    (xin_ref, frame0_ref, h0_ref, s0_ref, vecs_ref, wmain_ref, wlr_ref, wg2_ref, convw_ref, wg_ref,
     wout_ref) = refs[:11]
    gmix_ref, bg_ref, convb_ref, ba_ref, bx_ref, lam_ref, gn_ref = _mixer_vectors(vecs_ref)
    xo_ref, hl_ref, so_ref, tailo_ref, u, z, tail_sc, h_sc, s_sc = refs[11:]
    c = pl.program_id(0)

    @pl.when(c % nj == 0)
    def _():
        tail_sc[...] = frame0_ref[...]
        h_sc[...] = h0_ref[...]
        s_sc[...] = s0_ref[...].reshape(HK, GLA_DV)

    x = xin_ref[...]
    hn = _rms(x, gmix_ref[...]).astype(wmain_ref.dtype)
    tile_w = 2 * LANES
    carry = {}
    half = D_LRU // 2

    def in_tile(t):
        cols = slice(t * tile_w, (t + 1) * tile_w)
        u[:, cols] = _dot(hn, wmain_ref[:, cols])

    def gate_tile():
        lr = _dot(hn, wlr_ref[...])
        zg = _dot(lr, wg2_ref[...]) + bg_ref[...]
        u[:, D_MAIN:D_U] = _log_sigmoid(zg) / GLA_GATE_NORM

    def out_lru():
        xo_ref[...] = x + _dot(z[:, 0:D_LRU], wout_ref[0:D_LRU, :])

    def out_gla():
        xo_ref[...] += _dot(z[:, D_LRU:D_MODEL], wout_ref[D_LRU:D_MODEL, :])

    def lru_front(hf):
        cs = slice(hf * half, (hf + 1) * half)
        xl = u[:, cs]
        row8 = lax.broadcasted_iota(jnp.int32, (SUBLANES, half), 0)
        xc = convb_ref[:, cs]
        for sft in range(CONV_W - 1, -1, -1):
            if sft == 0:
                sh = xl
            else:
                rolled = pltpu.roll(xl, sft, 0)
                top = jnp.where(row8 < sft, pltpu.roll(tail_sc[:, cs], sft, 0), rolled[0:SUBLANES])
                sh = top if tc == SUBLANES else jnp.concatenate([top, rolled[SUBLANES:]], axis=0)
            xc = xc + convw_ref[CONV_W - 1 - sft:CONV_W - sft, cs] * sh
        carry["tail", hf] = xl[tc - SUBLANES:tc]
        tail_sc[:, cs] = carry["tail", hf]
        carry["xc", hf] = xc

    def lru_gate_mm(hf):
        carry["p", hf] = _dot(carry["xc", hf], wg_ref[hf])

    def lru_mid(hf):
        cs = slice(hf * half, (hf + 1) * half)
        a, mult, ix = _lru_gates(carry["xc", hf], carry["p", hf], ba_ref[:, cs], bx_ref[:, cs],
                                 lam_ref[:, cs])
        if reset_first:
            row = lax.broadcasted_iota(jnp.int32, (tc, half), 0)
            first = jnp.logical_and(row == 0, c % nj == 0)
            mult = jnp.where(first, 1.0, mult)
            a = jnp.where(first, 0.0, a)
        carry["a", hf], carry["b", hf] = a, mult * ix

    def lru_back(hf):
        cs = slice(hf * half, (hf + 1) * half)
        h = _scan_rows(carry["a", hf], carry["b", hf], h_sc[:, cs])
        carry["h", hf] = h[tc - 1:tc]
        h_sc[:, cs] = carry["h", hf]
        z[:, cs] = h * jax.nn.gelu(u[:, O_GL + hf * half:O_GL + (hf + 1) * half])

    lane_head = lax.broadcasted_iota(jnp.int32, (cg, HK), 1) // GLA_DK
    n_sub = tc // cg
    sub = lambda ci: slice(ci * cg, (ci + 1) * cg)

    def gla_prep(ci):
        tq = lax.broadcasted_iota(jnp.int32, (cg, HK), 0)
        b = _cumsum_rows(u[sub(ci), D_MAIN:D_U], tq, cg)
        bl = b[cg - 1:cg]
        k = u[sub(ci), O_K:O_K + HK]
        carry["stack", ci] = _head_stack(u[sub(ci), O_Q:O_Q + HK] * (GLA_DK ** -0.5) * jnp.exp(b),
                                         lane_head)
        k_s = k * jnp.exp(-b)
        carry["ks_t", ci] = jnp.concatenate([k_s, jnp.zeros((LANES - cg, HK), F32)], axis=0).T
        carry["ke", ci], carry["el", ci] = k * jnp.exp(bl - b), jnp.exp(bl)

    def gla_free_mm(ci):
        carry["kv", ci] = _gla_kv(carry["ke", ci], u[sub(ci), O_V:O_V + HV], per_head=True)
        carry["el_t", ci] = _state_decay(carry["el", ci])

    def gla_state(ci):
        s_all = carry["S"]
        both = _dot(carry["stack", ci], jnp.concatenate([s_all, carry["ks_t", ci]], axis=1))
        carry["oi", ci], carry["att", ci] = both[:, 0:GLA_DV], both[:, GLA_DV:GLA_DV + cg]
        carry["S"] = jnp.concatenate(_gla_next_state(carry["el_t", ci], s_all, carry["kv", ci]), axis=0)

    def gla_out(ci):
        causal = ((lax.broadcasted_iota(jnp.int32, (GLA_HEADS * cg, cg), 0) % cg)
                  >= lax.broadcasted_iota(jnp.int32, (GLA_HEADS * cg, cg), 1))
        o = _gla_intra(carry["att", ci], causal, carry["oi", ci], u[sub(ci), O_V:O_V + HV], cg)
        for hd in range(GLA_HEADS):
            z[sub(ci), D_LRU + hd * GLA_DV:D_LRU + (hd + 1) * GLA_DV] = _gla_output(
                o[hd], u[sub(ci), O_GO + hd * GLA_DV:O_GO + (hd + 1) * GLA_DV], gn_ref[...])

    def gla_begin():
        carry["S"] = s_sc[...]

    def gla_end():
        s_sc[...] = carry["S"]

    lru = {n: functools.partial(f, hf) for hf in range(2)
           for n, f in ((f"front{hf}", lru_front), (f"gmm{hf}", lru_gate_mm), (f"mid{hf}", lru_mid),
                        (f"back{hf}", lru_back))}
    tiles = [functools.partial(in_tile, t) for t in range(D_MAIN // tile_w)]
    assert len(tiles) == 10
    prep = [functools.partial(gla_prep, ci) for ci in range(n_sub)]
    free = [functools.partial(gla_free_mm, ci) for ci in range(n_sub)]
    head = (tiles[0:3] + [lru["front0"], tiles[3], lru["front1"], lru["gmm0"], lru["gmm1"],
                          tiles[4], lru["mid0"], tiles[5], gate_tile, lru["back0"], tiles[6], lru["mid1"],
                          tiles[7], lru["back1"], tiles[8]])
    mid = [prep[0], tiles[9]] + ([prep[1], out_lru] if n_sub > 1 else [out_lru])
    for ci in range(2, n_sub):
        mid += [prep[ci], free[ci - 2]]
    mid += free[max(n_sub - 2, 0):]
    order = (head + mid + [gla_begin] + [functools.partial(gla_state, ci) for ci in range(n_sub)]
             + [gla_end] + [functools.partial(gla_out, ci) for ci in range(n_sub)] + [out_gla])
    for piece in order:
        piece()

    for hf in range(2):
        cs = slice(hf * half, (hf + 1) * half)
        hl_ref[:, cs] = carry["h", hf]
        tailo_ref[:, cs] = carry["tail", hf]
    so_ref[...] = s_sc[...].reshape(GLA_HEADS, GLA_DK, GLA_DV)


def _mixer_seq_call(x, frame0, h0, s0, mix, l, *, nb, sl, tc, cg, row_block0, reset_first,
                    shared_state, name):
    assert sl % tc == 0 and tc % cg == 0 and tc % SUBLANES == 0
    nj = sl // tc
    xrow = pl.BlockSpec((tc, D_MODEL), lambda s: (row_block0 + s, 0))
    seq = lambda s: s // nj
    if shared_state:
        st = lambda s: (0, 0, 0)
        st4 = lambda s: (0, 0, 0, 0)
    else:
        st = lambda s: (seq(s), 0, 0)
        st4 = lambda s: (seq(s), 0, 0, 0)
    (vecs, w_in, wlr, wg2, convw, wg, wout) = mix
    args = [x, frame0, h0, s0, vecs, w_in, wlr, wg2, convw, wg, wout]
    in_specs = [
        xrow,
        pl.BlockSpec((None, SUBLANES, D_LRU), st),
        pl.BlockSpec((None, 1, D_LRU), st),
        pl.BlockSpec((None, GLA_HEADS, GLA_DK, GLA_DV), st4),
        _layer_spec(vecs, l), _MAIN_COLS_SPEC, _layer_spec(wlr, l), _layer_spec(wg2, l),
        _layer_spec(convw, l), _layer_spec(wg, l), _whole_spec(wout),
    ]
    out_shape = [
        jax.ShapeDtypeStruct((nb * sl, D_MODEL), F32),
        jax.ShapeDtypeStruct((nb, 1, D_LRU), F32),
        jax.ShapeDtypeStruct((nb, GLA_HEADS, GLA_DK, GLA_DV), F32),
        jax.ShapeDtypeStruct((nb, SUBLANES, D_LRU), F32),
    ]
    out_specs = [
        pl.BlockSpec((tc, D_MODEL), lambda s: (s, 0)),
        pl.BlockSpec((None, 1, D_LRU), lambda s: (seq(s), 0, 0)),
        pl.BlockSpec((None, GLA_HEADS, GLA_DK, GLA_DV), lambda s: (seq(s), 0, 0, 0)),
        pl.BlockSpec((None, SUBLANES, D_LRU), lambda s: (seq(s), 0, 0)),
    ]
    scratch = [
        pltpu.VMEM((tc, D_U), F32),
        pltpu.VMEM((tc, D_MODEL), F32),
        pltpu.VMEM((SUBLANES, D_LRU), F32),
        pltpu.VMEM((1, D_LRU), F32),
        pltpu.VMEM((HK, GLA_DV), F32),
    ]
    kern = functools.partial(_mixer_seq_kernel, tc=tc, cg=cg, nj=nj, reset_first=reset_first)
    return pl.pallas_call(
        kern, grid=(nb * nj,), in_specs=in_specs, out_specs=out_specs, out_shape=out_shape,
        scratch_shapes=scratch,
        compiler_params=pltpu.CompilerParams(dimension_semantics=("arbitrary",),
                                             vmem_limit_bytes=VMEM_LIMIT),
        name=name,
    )(*args)


def _mixer_dec_kernel(*refs, nb, n_prev):
    (x_ref, frame_ref, h0_ref, s0_ref, vecs_ref, wmain_ref, wlr_ref, wg2_ref, convw_ref, wg_ref,
     wout_ref) = refs[:11]
    gmix_ref, bg_ref, convb_ref, ba_ref, bx_ref, lam_ref, gn_ref = _mixer_vectors(vecs_ref)
    prev_refs = refs[11:11 + n_prev]
    (xo_ref, h_ref, xl_ref, so_ref,
     u_sc, z_sc, qs_sc, ks_sc, ke_sc, el_sc, o_sc) = refs[11 + n_prev:]
    for j, prev in enumerate(prev_refs):
        so_ref[j] = prev[...]
    so_new = so_ref.at[n_prev] if n_prev else so_ref
    ls = SUBLANES
    rows = nb * ls
    x = x_ref[...]
    _in_projection(x, gmix_ref, wmain_ref, wlr_ref, wg2_ref, bg_ref, u_sc)

    t = lax.broadcasted_iota(jnp.int32, (rows, D_LRU), 0) % ls
    xl = u_sc[:, 0:D_LRU]
    xl_ref[...] = xl
    frame = frame_ref[...]
    xc = convb_ref[...]
    for s in range(CONV_W - 1, -1, -1):
        if s == 0:
            sh = xl
        else:
            sh = jnp.where(t >= s, pltpu.roll(xl, s, 0), pltpu.roll(frame, (rows - ls + s) % rows, 0))
        xc = xc + convw_ref[CONV_W - 1 - s:CONV_W - s, :] * sh
    half = D_LRU // 2
    parts = []
    for hf in range(2):
        cs = slice(hf * half, (hf + 1) * half)
        parts.append(_lru_gates(xc[:, cs], _dot(xc[:, cs], wg_ref[hf]), ba_ref[:, cs], bx_ref[:, cs],
                                lam_ref[:, cs]))
    a, mult, ix = (jnp.concatenate([p[n] for p in parts], axis=1) for n in range(3))
    a3, b3 = _scan_groups(a, mult * ix)
    h = (a3 * h0_ref[...].reshape(a3.shape) + b3).reshape(rows, D_LRU)
    h_ref[...] = h
    z_sc[:, 0:D_LRU] = h * jax.nn.gelu(u_sc[:, O_GL:O_GL + D_LRU])

    tq = lax.broadcasted_iota(jnp.int32, (rows, HK), 0) % ls
    b = _cumsum_rows(u_sc[:, D_MAIN:D_U], tq, ls)
    bl = jnp.where(tq == ls - 1, b, 0.0)
    s = 1
    while s < ls:
        bl = bl + jnp.where(tq + s < ls, pltpu.roll(bl, rows - s, 0), 0.0)
        s *= 2
    k = u_sc[:, O_K:O_K + HK]
    qs_sc[...] = (u_sc[:, O_Q:O_Q + HK] * (GLA_DK ** -0.5)) * jnp.exp(b)
    ks_sc[...] = k * jnp.exp(-b)
    ke_sc[...] = k * jnp.exp(bl - b)
    el_sc[...] = jnp.exp(bl)

    lane_head = lax.broadcasted_iota(jnp.int32, (ls, HK), 1) // GLA_DK
    causal = ((lax.broadcasted_iota(jnp.int32, (GLA_HEADS * ls, ls), 0) % ls)
              >= lax.broadcasted_iota(jnp.int32, (GLA_HEADS * ls, ls), 1))

    def body(bi, carry):
        rs = pl.ds(pl.multiple_of(bi * ls, ls), ls)
        s_all = s0_ref[bi].reshape(HK, GLA_DV)
        o, s_new = _gla_chunk(qs_sc[rs, :], ks_sc[rs, :], ke_sc[rs, :], el_sc[rs, :][0:1],
                              u_sc[rs, O_V:O_V + HV], s_all, causal, lane_head, ls)
        for hd in range(GLA_HEADS):
            so_new[bi, hd] = s_new[hd]
            o_sc[rs, hd * GLA_DV:(hd + 1) * GLA_DV] = o[hd]
        return carry

    lax.fori_loop(0, nb, body, 0, unroll=8)

    gn = gn_ref[...]
    for hd in range(GLA_HEADS):
        vs = slice(hd * GLA_DV, (hd + 1) * GLA_DV)
        z_sc[:, D_LRU + hd * GLA_DV:D_LRU + (hd + 1) * GLA_DV] = _gla_output(
            o_sc[:, vs], u_sc[:, O_GO + hd * GLA_DV:O_GO + (hd + 1) * GLA_DV], gn)
    xo_ref[...] = x + _dot(z_sc[...], wout_ref[...])


def _mixer_dec_call(x, frame, h0, s0, s_prev, mix, l, *, nseq, nb, name):
    ls = SUBLANES
    assert nseq % nb == 0
    rows = nb * ls
    (vecs, w_in, wlr, wg2, convw, wg, wout) = mix
    rowspec = lambda w: pl.BlockSpec((rows, w), lambda i: (i, 0))
    sspec = pl.BlockSpec((None, nb, GLA_HEADS, GLA_DK, GLA_DV), lambda i: (l, i, 0, 0, 0))
    in_specs = [
        rowspec(D_MODEL), rowspec(D_LRU), rowspec(D_LRU), sspec,
        _layer_spec(vecs, l), _MAIN_COLS_SPEC, _layer_spec(wlr, l), _layer_spec(wg2, l),
        _layer_spec(convw, l), _layer_spec(wg, l), _whole_spec(wout),
    ]
    out_shape = [
        jax.ShapeDtypeStruct((nseq * ls, D_MODEL), F32),
        jax.ShapeDtypeStruct((nseq * ls, D_LRU), F32),
        jax.ShapeDtypeStruct((nseq * ls, D_LRU), F32),
    ]
    state = (nseq, GLA_HEADS, GLA_DK, GLA_DV)
    sblock = pl.BlockSpec((nb,) + state[1:], lambda i: (i, 0, 0, 0))
    n_prev = len(s_prev)
    if n_prev:
        out_shape.append(jax.ShapeDtypeStruct((n_prev + 1,) + state, F32))
        s_out_spec = pl.BlockSpec((n_prev + 1, nb) + state[1:], lambda i: (0, i, 0, 0, 0))
    else:
        out_shape.append(jax.ShapeDtypeStruct(state, F32))
        s_out_spec = sblock
    out_specs = [rowspec(D_MODEL), rowspec(D_LRU), rowspec(D_LRU), s_out_spec]
    args = [x, frame, h0, s0, vecs, w_in, wlr, wg2, convw, wg, wout]
    args += list(s_prev)
    in_specs += [sblock] * n_prev
    scratch = [
        pltpu.VMEM((rows, D_U), F32), pltpu.VMEM((rows, D_MODEL), F32),
        pltpu.VMEM((rows, HK), F32), pltpu.VMEM((rows, HK), F32), pltpu.VMEM((rows, HK), F32),
        pltpu.VMEM((rows, HK), F32), pltpu.VMEM((rows, HV), F32),
    ]
    return pl.pallas_call(
        functools.partial(_mixer_dec_kernel, nb=nb, n_prev=n_prev),
        grid=(nseq // nb,), in_specs=in_specs, out_specs=out_specs, out_shape=out_shape,
        scratch_shapes=scratch,
        compiler_params=pltpu.CompilerParams(dimension_semantics=("arbitrary",),
                                             vmem_limit_bytes=VMEM_LIMIT),
        name=name,
    )(*args)


def _block_diag_gates(wa, wx):
    per = LRU_BLOCKS // 2
    bw = wa.shape[-1]
    eye = jnp.eye(per, dtype=wa.dtype)

    def bd(w):
        return jnp.einsum("lncd,nm->lncmd", w, eye).reshape(w.shape[0], per * bw, per * bw)

    halves = [jnp.concatenate([bd(wa[:, hf * per:(hf + 1) * per]), bd(wx[:, hf * per:(hf + 1) * per])],
                              axis=-1) for hf in range(2)]
    return jnp.stack(halves, axis=1)


def kernel(x_prompt, x_sample, state_lru_h, state_lru_conv, state_gla_S, meta, norm_ffn1, w_ffn1_gu,
           w_ffn1_down, norm_mix, w_in, lru_conv_w, lru_conv_b, lru_wa, lru_ba, lru_wx, lru_bx,
           lru_lambda, gla_w_gate2, gla_b_gate, gla_norm, w_out, norm_ffn2, w_ffn2_gu, w_ffn2_down,
           norm_final):
    bp, lp, d = x_prompt.shape
    bs, ls, _ = x_sample.shape
    n_meta = meta.shape[0]
    depth = w_in.shape[0]
    assert d == D_MODEL and ls == SUBLANES and n_meta % SUBLANES == 0
    tp, ts = bp * lp, bs * ls
    tsm = ts + n_meta
    assert ts % n_meta == 0
    tm_p = 512 if tp % 512 == 0 else lp
    tm_s = tsm // 2 if (tsm // 2) % SUBLANES == 0 else tsm
    tc_p = 1024 if lp % 1024 == 0 else GLA_CHUNK
    nb_s = 32 if bs % 32 == 0 else bs
    tail = CONV_W - 1

    vecs = _pack_vectors(norm_mix, lru_ba, lru_bx, lru_lambda, lru_conv_b, gla_b_gate, gla_norm,
                         norm_ffn1, norm_ffn2, norm_final)
    w_lr = jnp.pad(w_in[:, :, D_MAIN:], ((0, 0), (0, 0), (0, LANES - GLA_RANK))).astype(BF16)
    w_g2 = jnp.pad(gla_w_gate2, ((0, 0), (0, LANES - GLA_RANK), (0, 0))).astype(BF16)
    w_gates = _block_diag_gates(lru_wa, lru_wx).astype(BF16)
    frames_s = jnp.pad(state_lru_conv, ((0, 0), (0, 0), (SUBLANES - tail, 0), (0, 0)))
    frames_s = frames_s.reshape(depth, ts, D_LRU)
    h0_s = jnp.broadcast_to(state_lru_h[:, :, None, :], (depth, bs, ls, D_LRU)).reshape(depth, ts, D_LRU)
    zero_frame = jnp.zeros((1, SUBLANES, D_LRU), F32)
    zero_h = jnp.zeros((1, 1, D_LRU), F32)
    zero_s = jnp.zeros((1, GLA_HEADS, GLA_DK, GLA_DV), F32)

    xp = x_prompt.reshape(tp, d)
    xs = jnp.concatenate([x_sample.reshape(ts, d), meta.astype(F32)], axis=0)

    hs_p, convs_p, ss_p, hs_s, convs_s, ss_s = [], [], [], [], [], []
    for l in range(depth):
        xp1, xs1, w_in_l, w_out_l = _ffn_call(xp, xs, vecs, ROW_FFN1, w_ffn1_gu, w_ffn1_down, l, final=False,
                                              tm_a=tm_p, tm_b=tm_s, tf=256, name=f"ffn1_{l}",
                                              cast=(w_in, w_out))
        mix = (vecs, w_in_l, w_lr, w_g2, lru_conv_w, w_gates, w_out_l)

        last = l == depth - 1
        xs2, h_s, xl_s, s_s = _mixer_dec_call(xs1, frames_s[l], h0_s[l], state_gla_S,
                                              ss_s if last else (), mix, l, nseq=bs, nb=nb_s,
                                              name=f"mix_s{l}")
        ss_s.append(s_s)
        hs_s.append(h_s.reshape(bs, ls, D_LRU)[:, ls - 1])
        convs_s.append(xl_s.reshape(bs, ls, D_LRU)[:, ls - tail:])

        xm2, h_m, s_m, tail_m = _mixer_seq_call(
            xs1, zero_frame, zero_h, zero_s, mix, l, nb=1, sl=n_meta, tc=n_meta, cg=n_meta,
            row_block0=ts // n_meta, reset_first=True, shared_state=False, name=f"mix_m{l}")
        xs2 = jnp.concatenate([xs2, xm2], axis=0)

        xp2, h_p, s_p, tail_p = _mixer_seq_call(
            xp1, tail_m, h_m, s_m, mix, l, nb=bp, sl=lp, tc=tc_p, cg=GLA_CHUNK, row_block0=0,
            reset_first=False, shared_state=True, name=f"mix_p{l}")
        hs_p.append(h_p[:, 0])
        convs_p.append(tail_p[:, SUBLANES - tail:])
        ss_p.append(s_p)

        xp, xs = _ffn_call(xp2, xs2, vecs, ROW_FFN2, w_ffn2_gu, w_ffn2_down, l, final=last, tm_a=tm_p,
                           tm_b=tm_s, tf=256, name=f"ffn2_{l}")

    s_stack = ss_s[-1] if depth > 1 else ss_s[0][None]
    return (xp.reshape(bp, lp, d), xs[:ts].reshape(bs, ls, d),
            jnp.stack(hs_p), jnp.stack(convs_p), jnp.stack(ss_p),
            jnp.stack(hs_s), jnp.stack(convs_s), s_stack)
```

```python
import functools

import jax
import jax.numpy as jnp
from jax import lax
from jax.experimental import pallas as pl
from jax.experimental.pallas import tpu as pltpu

F32 = jnp.float32
BF16 = jnp.bfloat16

D_MODEL = 1024
D_FF = 2816
D_LRU = 512
LRU_BLOCKS = 8
CONV_W = 4
LRU_C = 8.0
GLA_HEADS = 4
GLA_DV = 128
GLA_DK = 64
GLA_RANK = 16
GLA_GATE_NORM = 16.0
GLA_CHUNK = 64
EPS = 1e-6
HK = GLA_HEADS * GLA_DK
HV = GLA_HEADS * GLA_DV
O_GL, O_Q, O_K, O_V, O_GO = D_LRU, 2 * D_LRU, 2 * D_LRU + HK, 2 * D_LRU + 2 * HK, 2 * D_LRU + 2 * HK + HV
D_MAIN = O_GO + HV
D_U = D_MAIN + HK
SUBLANES = 8
LANES = 128
VMEM_LIMIT = 58 * 1024 * 1024
ROW_MIX, ROW_GATE_B, ROW_LAM_CONVB, ROW_GLA, ROW_FFN1, ROW_FFN2, ROW_FINAL = range(7)


def _dot(a, b):
    if a.dtype != b.dtype:
        a = a.astype(b.dtype)
    return jnp.dot(a, b, preferred_element_type=F32)


def _dot_nt(a, b):
    return lax.dot_general(a, b, (((1,), (1,)), ((), ())), preferred_element_type=F32)


def _dot_tn(a, b):
    return lax.dot_general(a, b, (((0,), (0,)), ((), ())), preferred_element_type=F32)


def _rms(x, g):
    ms = jnp.mean(x * x, axis=-1, keepdims=True)
    return (x * lax.rsqrt(ms + EPS)) * g


def _sigmoid(x):
    return 0.5 * jnp.tanh(0.5 * x) + 0.5


def _silu(x):
    return x * _sigmoid(x)


def _log_sigmoid(x):
    return jnp.minimum(x, 0.0) - jnp.log(1.0 + jnp.exp(-jnp.abs(x)))


def _pack_vectors(norm_mix, ba, bx, lam, conv_b, b_gate, gla_norm, norm_ffn1, norm_ffn2, norm_final):
    depth, d = norm_mix.shape
    rows = [norm_mix, jnp.concatenate([ba, bx], axis=1), jnp.concatenate([lam, conv_b], axis=1),
            jnp.pad(jnp.concatenate([b_gate, gla_norm], axis=1), ((0, 0), (0, d - HK - GLA_DV))),
            norm_ffn1, norm_ffn2, jnp.broadcast_to(norm_final[None, :], (depth, d)),
            jnp.zeros((depth, d), F32)]
    return jnp.stack(rows, axis=1)


def _mixer_vectors(vecs_ref):
    row = lambda r, a, b: vecs_ref.at[r:r + 1, a:b]
    return (row(ROW_MIX, 0, D_MODEL), row(ROW_GLA, 0, HK), row(ROW_LAM_CONVB, D_LRU, 2 * D_LRU),
            row(ROW_GATE_B, 0, D_LRU), row(ROW_GATE_B, D_LRU, 2 * D_LRU), row(ROW_LAM_CONVB, 0, D_LRU),
            row(ROW_GLA, HK, HK + GLA_DV))


def _layer_spec(a, l):
    nd = a.ndim
    return pl.BlockSpec((None,) + a.shape[1:], lambda *_: (l,) + (0,) * (nd - 1),
                        pipeline_mode=pl.Buffered(1))


def _whole_spec(a):
    nd = a.ndim
    return pl.BlockSpec(a.shape, lambda *_: (0,) * nd, pipeline_mode=pl.Buffered(1))


def _main_cols_spec(l):
    return pl.BlockSpec((None, D_MODEL, D_MAIN), lambda *_: (l, 0, 0), pipeline_mode=pl.Buffered(1))


def _ffn_kernel(*refs, tf, norm_row, has_final, n_a, n_cast):
    xa_ref, xb_ref, vecs_ref, wgu_ref, wd_ref = refs[:5]
    cast_src = refs[5:5 + n_cast]
    oa_ref, ob_ref = refs[5 + n_cast:7 + n_cast]
    cast_dst = refs[7 + n_cast:]
    g_ref = vecs_ref.at[norm_row:norm_row + 1, :]
    gfin_ref = vecs_ref.at[ROW_FINAL:ROW_FINAL + 1, :]

    def tile(x_ref, xo_ref):
        x = x_ref[...]
        xn = _rms(x, g_ref[...])
        acc = None
        for c in range(D_FF // tf):
            gate = _dot(xn, wgu_ref[:, c * tf:(c + 1) * tf])
            up = _dot(xn, wgu_ref[:, D_FF + c * tf:D_FF + (c + 1) * tf])
            d = _dot(_silu(gate) * up, wd_ref[c * tf:(c + 1) * tf, :])
            acc = d if acc is None else acc + d
        x = x + 0.5 * acc
        xo_ref[...] = _rms(x, gfin_ref[...]) if has_final else x

    i = pl.program_id(0)

    @pl.when(i < n_a)
    def _():
        tile(xa_ref, oa_ref)
        for s_ref, d_ref in zip(cast_src, cast_dst):
            d_ref[...] = s_ref[...].astype(d_ref.dtype)

    @pl.when(i >= n_a)
    def _():
        tile(xb_ref, ob_ref)


def _ffn_call(xa, xb, vecs, norm_row, wgu, wd, l, *, final, tm_a, tm_b, tf, name, cast=()):
    (ta, d), (tb, _) = xa.shape, xb.shape
    assert ta % tm_a == 0 and tb % tm_b == 0 and D_FF % tf == 0
    n_a, n_b = ta // tm_a, tb // tm_b
    rows_a = pl.BlockSpec((tm_a, d), lambda i: (jnp.minimum(i, n_a - 1), 0))
    rows_b = pl.BlockSpec((tm_b, d), lambda i: (jnp.maximum(i - n_a, 0), 0))
    args = [xa, xb, vecs, wgu, wd] + list(cast)
    specs = [rows_a, rows_b, _layer_spec(vecs, l), _layer_spec(wgu, l), _layer_spec(wd, l)]
    out_specs = [rows_a, rows_b]
    out_shape = [jax.ShapeDtypeStruct((ta, d), F32), jax.ShapeDtypeStruct((tb, d), F32)]
    for w in cast:
        _, r, c = w.shape
        assert r % n_a == 0 and (r // n_a) % (2 * SUBLANES) == 0
        slab = lambda i: (jnp.minimum(i, n_a - 1), 0)
        specs.append(pl.BlockSpec((None, r // n_a, c), lambda i: (l, jnp.minimum(i, n_a - 1), 0)))
        out_specs.append(pl.BlockSpec((r // n_a, c), slab))
        out_shape.append(jax.ShapeDtypeStruct((r, c), BF16))
    return pl.pallas_call(
        functools.partial(_ffn_kernel, tf=tf, norm_row=norm_row, has_final=final, n_a=n_a,
                          n_cast=len(cast)),
        grid=(n_a + n_b,), in_specs=specs, out_specs=out_specs, out_shape=out_shape,
        compiler_params=pltpu.CompilerParams(dimension_semantics=("arbitrary",),
                                             vmem_limit_bytes=VMEM_LIMIT),
        name=name,
    )(*args)


def _in_projection(x, gmix_ref, wmain_ref, wlr_ref, wg2_ref, bg_ref, u_sc):
    hn = _rms(x, gmix_ref[...]).astype(wmain_ref.dtype)
    u_sc[:, 0:D_MAIN] = _dot(hn, wmain_ref[...])
    lr = _dot(hn, wlr_ref[...])
    zg = _dot(lr, wg2_ref[...]) + bg_ref[...]
    u_sc[:, D_MAIN:D_U] = _log_sigmoid(zg) / GLA_GATE_NORM


def _lru_gates(xc, p, ba, bx, lam):
    half = xc.shape[1]
    r = _sigmoid(p[:, 0:half] + ba)
    i = _sigmoid(p[:, half:] + bx)
    log_a = -LRU_C * r * jax.nn.softplus(-lam)
    a = jnp.exp(log_a)
    y = -jnp.tanh(log_a) * (a * a + 1.0)
    mult = jnp.where(y > 0.0, y * lax.rsqrt(y), 0.0)
    return a, mult, i * xc


def _scan_groups(a, b):
    rows, cols = a.shape
    a3 = a.reshape(rows // SUBLANES, SUBLANES, cols)
    b3 = b.reshape(rows // SUBLANES, SUBLANES, cols)
    t = lax.broadcasted_iota(jnp.int32, a3.shape, 1)
    s = 1
    while s < SUBLANES:
        keep = t >= s
        b3 = jnp.where(keep, a3 * pltpu.roll(b3, s, 1) + b3, b3)
        a3 = jnp.where(keep, a3 * pltpu.roll(a3, s, 1), a3)
        s *= 2
    return a3, b3


def _scan_rows(a, b, h0):
    a3, b3 = _scan_groups(a, b)
    carry, out = h0, []
    for g in range(a3.shape[0]):
        hg = a3[g] * carry + b3[g]
        out.append(hg)
        carry = hg[SUBLANES - 1:SUBLANES]
    return out[0] if len(out) == 1 else jnp.concatenate(out, axis=0)


def _cumsum_rows(x, t, n):
    s = 1
    while s < n:
        x = x + jnp.where(t >= s, pltpu.roll(x, s, 0), 0.0)
        s *= 2
    return x


def _head_stack(q_s, lane_head):
    return jnp.concatenate([jnp.where(lane_head == hd, q_s, 0.0) for hd in range(GLA_HEADS)], axis=0)


def _gla_kv(k_end, v, per_head):
    if per_head:
        return [_dot_tn(k_end[:, hd * GLA_DK:(hd + 1) * GLA_DK], v[:, hd * GLA_DV:(hd + 1) * GLA_DV])
                for hd in range(GLA_HEADS)]
    out = []
    for pr in range(GLA_HEADS // 2):
        kv = _dot_tn(k_end[:, 2 * pr * GLA_DK:2 * (pr + 1) * GLA_DK], v[:, 2 * pr * GLA_DV:2 * (pr + 1) * GLA_DV])
        out += [kv[0:GLA_DK, 0:GLA_DV], kv[GLA_DK:2 * GLA_DK, GLA_DV:2 * GLA_DV]]
    return out


def _state_decay(el):
    return jnp.broadcast_to(el, (LANES, HK)).T


def _gla_next_state(el_t, s_all, kv):
    return [el_t[hd * GLA_DK:(hd + 1) * GLA_DK] * s_all[hd * GLA_DK:(hd + 1) * GLA_DK] + kv[hd]
            for hd in range(GLA_HEADS)]


def _gla_intra(att_raw, causal, o_inter, v, n):
    att = jnp.where(causal, att_raw, 0.0)
    return [o_inter[hd * n:(hd + 1) * n] + _dot(att[hd * n:(hd + 1) * n], v[:, hd * GLA_DV:(hd + 1) * GLA_DV])
            for hd in range(GLA_HEADS)]


def _gla_chunk(q_s, k_s, k_end, el, v, s_all, causal, lane_head, n):
    stack = _head_stack(q_s, lane_head)
    o_inter = _dot(stack, s_all)
    att_raw = _dot_nt(stack, k_s)
    kv = _gla_kv(k_end, v, per_head=False)
    el_t = _state_decay(el)
    return _gla_intra(att_raw, causal, o_inter, v, n), _gla_next_state(el_t, s_all, kv)


def _gla_output(o, go, gn):
    return _rms(o, gn) * _silu(go)


def _mixer_seq_kernel(*refs, tc, cg, nj, reset_first):
    (xin_ref, frame0_ref, h0_ref, s0_ref, vecs_ref, wmain_ref, wlr_ref, wg2_ref, convw_ref, wg_ref,
     wout_ref) = refs[:11]
    gmix_ref, bg_ref, convb_ref, ba_ref, bx_ref, lam_ref, gn_ref = _mixer_vectors(vecs_ref)
    xo_ref, hl_ref, so_ref, tailo_ref, u, z, tail_sc, h_sc, s_sc = refs[11:]
    c = pl.program_id(0)

    @pl.when(c % nj == 0)
    def _():
        tail_sc[...] = frame0_ref[...]
        h_sc[...] = h0_ref[...]
        s_sc[...] = s0_ref[...].reshape(HK, GLA_DV)

    x = xin_ref[...]
    hn = _rms(x, gmix_ref[...]).astype(wmain_ref.dtype)
    tile_w = 2 * LANES
    carry = {}
    half = D_LRU // 2

    def in_tile(t):
        cols = slice(t * tile_w, (t + 1) * tile_w)
        u[:, cols] = _dot(hn, wmain_ref[:, cols])

    def gate_tile():
        lr = _dot(hn, wlr_ref[...])
        zg = _dot(lr, wg2_ref[...]) + bg_ref[...]
        u[:, D_MAIN:D_U] = _log_sigmoid(zg) / GLA_GATE_NORM

    def out_lru():
        xo_ref[...] = x + _dot(z[:, 0:D_LRU], wout_ref[0:D_LRU, :])

    def out_gla():
        xo_ref[...] += _dot(z[:, D_LRU:D_MODEL], wout_ref[D_LRU:D_MODEL, :])

    def lru_front(hf):
        cs = slice(hf * half, (hf + 1) * half)
        xl = u[:, cs]
        row8 = lax.broadcasted_iota(jnp.int32, (SUBLANES, half), 0)
        xc = convb_ref[:, cs]
        for sft in range(CONV_W - 1, -1, -1):
            if sft == 0:
                sh = xl
            else:
                rolled = pltpu.roll(xl, sft, 0)
                top = jnp.where(row8 < sft, pltpu.roll(tail_sc[:, cs], sft, 0), rolled[0:SUBLANES])
                sh = top if tc == SUBLANES else jnp.concatenate([top, rolled[SUBLANES:]], axis=0)
            xc = xc + convw_ref[CONV_W - 1 - sft:CONV_W - sft, cs] * sh
        carry["tail", hf] = xl[tc - SUBLANES:tc]
        tail_sc[:, cs] = carry["tail", hf]
        carry["xc", hf] = xc

    def lru_gate_mm(hf):
        carry["p", hf] = _dot(carry["xc", hf], wg_ref[hf])

    def lru_mid(hf):
        cs = slice(hf * half, (hf + 1) * half)
        a, mult, ix = _lru_gates(carry["xc", hf], carry["p", hf], ba_ref[:, cs], bx_ref[:, cs],
                                 lam_ref[:, cs])
        if reset_first:
            row = lax.broadcasted_iota(jnp.int32, (tc, half), 0)
            first = jnp.logical_and(row == 0, c % nj == 0)
            mult = jnp.where(first, 1.0, mult)
            a = jnp.where(first, 0.0, a)
        carry["a", hf], carry["b", hf] = a, mult * ix

    def lru_back(hf):
        cs = slice(hf * half, (hf + 1) * half)
        h = _scan_rows(carry["a", hf], carry["b", hf], h_sc[:, cs])
        carry["h", hf] = h[tc - 1:tc]
        h_sc[:, cs] = carry["h", hf]
        z[:, cs] = h * jax.nn.gelu(u[:, O_GL + hf * half:O_GL + (hf + 1) * half])

    lane_head = lax.broadcasted_iota(jnp.int32, (cg, HK), 1) // GLA_DK
    n_sub = tc // cg
    sub = lambda ci: slice(ci * cg, (ci + 1) * cg)

    def gla_prep(ci):
        tq = lax.broadcasted_iota(jnp.int32, (cg, HK), 0)
        b = _cumsum_rows(u[sub(ci), D_MAIN:D_U], tq, cg)
        bl = b[cg - 1:cg]
        k = u[sub(ci), O_K:O_K + HK]
        carry["stack", ci] = _head_stack(u[sub(ci), O_Q:O_Q + HK] * (GLA_DK ** -0.5) * jnp.exp(b),
                                         lane_head)
        k_s = k * jnp.exp(-b)
        carry["ks_t", ci] = jnp.concatenate([k_s, jnp.zeros((LANES - cg, HK), F32)], axis=0).T
        carry["ke", ci], carry["el", ci] = k * jnp.exp(bl - b), jnp.exp(bl)

    def gla_free_mm(ci):
        carry["kv", ci] = _gla_kv(carry["ke", ci], u[sub(ci), O_V:O_V + HV], per_head=True)
        carry["el_t", ci] = _state_decay(carry["el", ci])

    def gla_state(ci):
        s_all = carry["S"]
        both = _dot(carry["stack", ci], jnp.concatenate([s_all, carry["ks_t", ci]], axis=1))
        carry["oi", ci], carry["att", ci] = both[:, 0:GLA_DV], both[:, GLA_DV:GLA_DV + cg]
        carry["S"] = jnp.concatenate(_gla_next_state(carry["el_t", ci], s_all, carry["kv", ci]), axis=0)

    def gla_out(ci):
        causal = ((lax.broadcasted_iota(jnp.int32, (GLA_HEADS * cg, cg), 0) % cg)
                  >= lax.broadcasted_iota(jnp.int32, (GLA_HEADS * cg, cg), 1))
        o = _gla_intra(carry["att", ci], causal, carry["oi", ci], u[sub(ci), O_V:O_V + HV], cg)
        for hd in range(GLA_HEADS):
            z[sub(ci), D_LRU + hd * GLA_DV:D_LRU + (hd + 1) * GLA_DV] = _gla_output(
                o[hd], u[sub(ci), O_GO + hd * GLA_DV:O_GO + (hd + 1) * GLA_DV], gn_ref[...])

    def gla_begin():
        carry["S"] = s_sc[...]

    def gla_end():
        s_sc[...] = carry["S"]

    lru = {n: functools.partial(f, hf) for hf in range(2)
           for n, f in ((f"front{hf}", lru_front), (f"gmm{hf}", lru_gate_mm), (f"mid{hf}", lru_mid),
                        (f"back{hf}", lru_back))}
    tiles = [functools.partial(in_tile, t) for t in range(D_MAIN // tile_w)]
    assert len(tiles) == 10
    prep = [functools.partial(gla_prep, ci) for ci in range(n_sub)]
    free = [functools.partial(gla_free_mm, ci) for ci in range(n_sub)]
    head = (tiles[0:3] + [lru["front0"], tiles[3], lru["front1"], lru["gmm0"], lru["gmm1"],
                          tiles[4], lru["mid0"], tiles[5], gate_tile, lru["back0"], tiles[6], lru["mid1"],
                          tiles[7], lru["back1"], tiles[8]])
    mid = [prep[0], tiles[9]] + ([prep[1], out_lru] if n_sub > 1 else [out_lru])
    for ci in range(2, n_sub):
        mid += [prep[ci], free[ci - 2]]
    mid += free[max(n_sub - 2, 0):]
    order = (head + mid + [gla_begin] + [functools.partial(gla_state, ci) for ci in range(n_sub)]
             + [gla_end] + [functools.partial(gla_out, ci) for ci in range(n_sub)] + [out_gla])
    for piece in order:
        piece()

    for hf in range(2):
        cs = slice(hf * half, (hf + 1) * half)
        hl_ref[:, cs] = carry["h", hf]
        tailo_ref[:, cs] = carry["tail", hf]
    so_ref[...] = s_sc[...].reshape(GLA_HEADS, GLA_DK, GLA_DV)


def _mixer_seq_call(x, frame0, h0, s0, mix, l, *, nb, sl, tc, cg, row_block0, reset_first,
                    shared_state, name):
    assert sl % tc == 0 and tc % cg == 0 and tc % SUBLANES == 0
    nj = sl // tc
    xrow = pl.BlockSpec((tc, D_MODEL), lambda s: (row_block0 + s, 0))
    seq = lambda s: s // nj
    if shared_state:
        st = lambda s: (0, 0, 0)
        st4 = lambda s: (0, 0, 0, 0)
    else:
        st = lambda s: (seq(s), 0, 0)
        st4 = lambda s: (seq(s), 0, 0, 0)
    (vecs, w_in, wlr, wg2, convw, wg, wout) = mix
    args = [x, frame0, h0, s0, vecs, w_in, wlr, wg2, convw, wg, wout]
    in_specs = [
        xrow,
        pl.BlockSpec((None, SUBLANES, D_LRU), st),
        pl.BlockSpec((None, 1, D_LRU), st),
        pl.BlockSpec((None, GLA_HEADS, GLA_DK, GLA_DV), st4),
        _layer_spec(vecs, l), _main_cols_spec(l), _layer_spec(wlr, l), _layer_spec(wg2, l),
        _layer_spec(convw, l), _layer_spec(wg, l), _whole_spec(wout),
    ]
    out_shape = [
        jax.ShapeDtypeStruct((nb * sl, D_MODEL), F32),
        jax.ShapeDtypeStruct((nb, 1, D_LRU), F32),
        jax.ShapeDtypeStruct((nb, GLA_HEADS, GLA_DK, GLA_DV), F32),
        jax.ShapeDtypeStruct((nb, SUBLANES, D_LRU), F32),
    ]
    out_specs = [
        pl.BlockSpec((tc, D_MODEL), lambda s: (s, 0)),
        pl.BlockSpec((None, 1, D_LRU), lambda s: (seq(s), 0, 0)),
        pl.BlockSpec((None, GLA_HEADS, GLA_DK, GLA_DV), lambda s: (seq(s), 0, 0, 0)),
        pl.BlockSpec((None, SUBLANES, D_LRU), lambda s: (seq(s), 0, 0)),
    ]
    scratch = [
        pltpu.VMEM((tc, D_U), F32),
        pltpu.VMEM((tc, D_MODEL), F32),
        pltpu.VMEM((SUBLANES, D_LRU), F32),
        pltpu.VMEM((1, D_LRU), F32),
        pltpu.VMEM((HK, GLA_DV), F32),
    ]
    kern = functools.partial(_mixer_seq_kernel, tc=tc, cg=cg, nj=nj, reset_first=reset_first)
    return pl.pallas_call(
        kern, grid=(nb * nj,), in_specs=in_specs, out_specs=out_specs, out_shape=out_shape,
        scratch_shapes=scratch,
        compiler_params=pltpu.CompilerParams(dimension_semantics=("arbitrary",),
                                             vmem_limit_bytes=VMEM_LIMIT),
        name=name,
    )(*args)


def _mixer_dec_kernel(*refs, nb, n_prev):
    (x_ref, frame_ref, h0_ref, s0_ref, vecs_ref, wmain_ref, wlr_ref, wg2_ref, convw_ref, wg_ref,
     wout_ref) = refs[:11]
    gmix_ref, bg_ref, convb_ref, ba_ref, bx_ref, lam_ref, gn_ref = _mixer_vectors(vecs_ref)
    prev_refs = refs[11:11 + n_prev]
    (xo_ref, h_ref, xl_ref, so_ref,
     u_sc, z_sc, qs_sc, ks_sc, ke_sc, el_sc, o_sc) = refs[11 + n_prev:]
    for j, prev in enumerate(prev_refs):
        so_ref[j] = prev[...]
    so_new = so_ref.at[n_prev] if n_prev else so_ref
    ls = SUBLANES
    rows = nb * ls
    x = x_ref[...]
    _in_projection(x, gmix_ref, wmain_ref, wlr_ref, wg2_ref, bg_ref, u_sc)

    t = lax.broadcasted_iota(jnp.int32, (rows, D_LRU), 0) % ls
    xl = u_sc[:, 0:D_LRU]
    xl_ref[...] = xl
    frame = frame_ref[...]
    xc = convb_ref[...]
    for s in range(CONV_W - 1, -1, -1):
        if s == 0:
            sh = xl
        else:
            sh = jnp.where(t >= s, pltpu.roll(xl, s, 0), pltpu.roll(frame, (rows - ls + s) % rows, 0))
        xc = xc + convw_ref[CONV_W - 1 - s:CONV_W - s, :] * sh
    half = D_LRU // 2
    parts = []
    for hf in range(2):
        cs = slice(hf * half, (hf + 1) * half)
        parts.append(_lru_gates(xc[:, cs], _dot(xc[:, cs], wg_ref[hf]), ba_ref[:, cs], bx_ref[:, cs],
                                lam_ref[:, cs]))
    a, mult, ix = (jnp.concatenate([p[n] for p in parts], axis=1) for n in range(3))
    a3, b3 = _scan_groups(a, mult * ix)
    h = (a3 * h0_ref[...].reshape(a3.shape) + b3).reshape(rows, D_LRU)
    h_ref[...] = h
    z_sc[:, 0:D_LRU] = h * jax.nn.gelu(u_sc[:, O_GL:O_GL + D_LRU])

    tq = lax.broadcasted_iota(jnp.int32, (rows, HK), 0) % ls
    b = _cumsum_rows(u_sc[:, D_MAIN:D_U], tq, ls)
    bl = jnp.where(tq == ls - 1, b, 0.0)
    s = 1
    while s < ls:
        bl = bl + jnp.where(tq + s < ls, pltpu.roll(bl, rows - s, 0), 0.0)
        s *= 2
    k = u_sc[:, O_K:O_K + HK]
    qs_sc[...] = (u_sc[:, O_Q:O_Q + HK] * (GLA_DK ** -0.5)) * jnp.exp(b)
    ks_sc[...] = k * jnp.exp(-b)
    ke_sc[...] = k * jnp.exp(bl - b)
    el_sc[...] = jnp.exp(bl)

    lane_head = lax.broadcasted_iota(jnp.int32, (ls, HK), 1) // GLA_DK
    causal = ((lax.broadcasted_iota(jnp.int32, (GLA_HEADS * ls, ls), 0) % ls)
              >= lax.broadcasted_iota(jnp.int32, (GLA_HEADS * ls, ls), 1))

    def body(bi, carry):
        rs = pl.ds(pl.multiple_of(bi * ls, ls), ls)
        s_all = s0_ref[bi].reshape(HK, GLA_DV)
        o, s_new = _gla_chunk(qs_sc[rs, :], ks_sc[rs, :], ke_sc[rs, :], el_sc[rs, :][0:1],
                              u_sc[rs, O_V:O_V + HV], s_all, causal, lane_head, ls)
        for hd in range(GLA_HEADS):
            so_new[bi, hd] = s_new[hd]
            o_sc[rs, hd * GLA_DV:(hd + 1) * GLA_DV] = o[hd]
        return carry

    lax.fori_loop(0, nb, body, 0, unroll=16)

    gn = gn_ref[...]
    for hd in range(GLA_HEADS):
        vs = slice(hd * GLA_DV, (hd + 1) * GLA_DV)
        z_sc[:, D_LRU + hd * GLA_DV:D_LRU + (hd + 1) * GLA_DV] = _gla_output(
            o_sc[:, vs], u_sc[:, O_GO + hd * GLA_DV:O_GO + (hd + 1) * GLA_DV], gn)
    xo_ref[...] = x + _dot(z_sc[...], wout_ref[...])


def _mixer_dec_call(x, frame, h0, s0, s_prev, mix, l, *, nseq, nb, name):
    ls = SUBLANES
    assert nseq % nb == 0
    rows = nb * ls
    (vecs, w_in, wlr, wg2, convw, wg, wout) = mix
    rowspec = lambda w: pl.BlockSpec((rows, w), lambda i: (i, 0))
    sspec = pl.BlockSpec((None, nb, GLA_HEADS, GLA_DK, GLA_DV), lambda i: (l, i, 0, 0, 0))
    in_specs = [
        rowspec(D_MODEL), rowspec(D_LRU), rowspec(D_LRU), sspec,
        _layer_spec(vecs, l), _main_cols_spec(l), _layer_spec(wlr, l), _layer_spec(wg2, l),
        _layer_spec(convw, l), _layer_spec(wg, l), _whole_spec(wout),
    ]
    out_shape = [
        jax.ShapeDtypeStruct((nseq * ls, D_MODEL), F32),
        jax.ShapeDtypeStruct((nseq * ls, D_LRU), F32),
        jax.ShapeDtypeStruct((nseq * ls, D_LRU), F32),
    ]
    state = (nseq, GLA_HEADS, GLA_DK, GLA_DV)
    sblock = pl.BlockSpec((nb,) + state[1:], lambda i: (i, 0, 0, 0))
    n_prev = len(s_prev)
    if n_prev:
        out_shape.append(jax.ShapeDtypeStruct((n_prev + 1,) + state, F32))
        s_out_spec = pl.BlockSpec((n_prev + 1, nb) + state[1:], lambda i: (0, i, 0, 0, 0))
    else:
        out_shape.append(jax.ShapeDtypeStruct(state, F32))
        s_out_spec = sblock
    out_specs = [rowspec(D_MODEL), rowspec(D_LRU), rowspec(D_LRU), s_out_spec]
    args = [x, frame, h0, s0, vecs, w_in, wlr, wg2, convw, wg, wout]
    args += list(s_prev)
    in_specs += [sblock] * n_prev
    scratch = [
        pltpu.VMEM((rows, D_U), F32), pltpu.VMEM((rows, D_MODEL), F32),
        pltpu.VMEM((rows, HK), F32), pltpu.VMEM((rows, HK), F32), pltpu.VMEM((rows, HK), F32),
        pltpu.VMEM((rows, HK), F32), pltpu.VMEM((rows, HV), F32),
    ]
    return pl.pallas_call(
        functools.partial(_mixer_dec_kernel, nb=nb, n_prev=n_prev),
        grid=(nseq // nb,), in_specs=in_specs, out_specs=out_specs, out_shape=out_shape,
        scratch_shapes=scratch,
        compiler_params=pltpu.CompilerParams(dimension_semantics=("arbitrary",),
                                             vmem_limit_bytes=VMEM_LIMIT),
        name=name,
    )(*args)


def _block_diag_gates(wa, wx):
    per = LRU_BLOCKS // 2
    bw = wa.shape[-1]
    eye = jnp.eye(per, dtype=wa.dtype)

    def bd(w):
        return jnp.einsum("lncd,nm->lncmd", w, eye).reshape(w.shape[0], per * bw, per * bw)

    halves = [jnp.concatenate([bd(wa[:, hf * per:(hf + 1) * per]), bd(wx[:, hf * per:(hf + 1) * per])],
                              axis=-1) for hf in range(2)]
    return jnp.stack(halves, axis=1)


def kernel(x_prompt, x_sample, state_lru_h, state_lru_conv, state_gla_S, meta, norm_ffn1, w_ffn1_gu,
           w_ffn1_down, norm_mix, w_in, lru_conv_w, lru_conv_b, lru_wa, lru_ba, lru_wx, lru_bx,
           lru_lambda, gla_w_gate2, gla_b_gate, gla_norm, w_out, norm_ffn2, w_ffn2_gu, w_ffn2_down,
           norm_final):
    bp, lp, d = x_prompt.shape
    bs, ls, _ = x_sample.shape
    n_meta = meta.shape[0]
    depth = w_in.shape[0]
    assert d == D_MODEL and ls == SUBLANES and n_meta % SUBLANES == 0
    tp, ts = bp * lp, bs * ls
    tsm = ts + n_meta
    assert ts % n_meta == 0
    tm_p = 512 if tp % 512 == 0 else lp
    tm_s = tsm // 2 if (tsm // 2) % SUBLANES == 0 else tsm
    tc_p = 1024 if lp % 1024 == 0 else GLA_CHUNK
    nb_s = 32 if bs % 32 == 0 else bs
    tail = CONV_W - 1

    vecs = _pack_vectors(norm_mix, lru_ba, lru_bx, lru_lambda, lru_conv_b, gla_b_gate, gla_norm,
                         norm_ffn1, norm_ffn2, norm_final)
    w_in_b = w_in.astype(BF16)
    w_lr = jnp.pad(w_in[:, :, D_MAIN:], ((0, 0), (0, 0), (0, LANES - GLA_RANK))).astype(BF16)
    w_g2 = jnp.pad(gla_w_gate2, ((0, 0), (0, LANES - GLA_RANK), (0, 0))).astype(BF16)
    w_gates = _block_diag_gates(lru_wa, lru_wx).astype(BF16)
    frames_s = jnp.pad(state_lru_conv, ((0, 0), (0, 0), (SUBLANES - tail, 0), (0, 0)))
    frames_s = frames_s.reshape(depth, ts, D_LRU)
    h0_s = jnp.broadcast_to(state_lru_h[:, :, None, :], (depth, bs, ls, D_LRU)).reshape(depth, ts, D_LRU)
    zero_frame = jnp.zeros((1, SUBLANES, D_LRU), F32)
    zero_h = jnp.zeros((1, 1, D_LRU), F32)
    zero_s = jnp.zeros((1, GLA_HEADS, GLA_DK, GLA_DV), F32)

    xp = x_prompt.reshape(tp, d)
    xs = jnp.concatenate([x_sample.reshape(ts, d), meta.astype(F32)], axis=0)

    hs_p, convs_p, ss_p, hs_s, convs_s, ss_s = [], [], [], [], [], []
    for l in range(depth):
        xp1, xs1, w_out_l = _ffn_call(xp, xs, vecs, ROW_FFN1, w_ffn1_gu, w_ffn1_down, l, final=False,
                                      tm_a=tm_p, tm_b=tm_s, tf=256, name=f"ffn1_{l}", cast=(w_out,))
        mix = (vecs, w_in_b, w_lr, w_g2, lru_conv_w, w_gates, w_out_l)

        last = l == depth - 1
        xs2, h_s, xl_s, s_s = _mixer_dec_call(xs1, frames_s[l], h0_s[l], state_gla_S,
                                              ss_s if last else (), mix, l, nseq=bs, nb=nb_s,
                                              name=f"mix_s{l}")
        ss_s.append(s_s)
        hs_s.append(h_s.reshape(bs, ls, D_LRU)[:, ls - 1])
        convs_s.append(xl_s.reshape(bs, ls, D_LRU)[:, ls - tail:])

        xm2, h_m, s_m, tail_m = _mixer_seq_call(
            xs1, zero_frame, zero_h, zero_s, mix, l, nb=1, sl=n_meta, tc=n_meta, cg=n_meta,
            row_block0=ts // n_meta, reset_first=True, shared_state=False, name=f"mix_m{l}")
        xs2 = jnp.concatenate([xs2, xm2], axis=0)

        xp2, h_p, s_p, tail_p = _mixer_seq_call(
            xp1, tail_m, h_m, s_m, mix, l, nb=bp, sl=lp, tc=tc_p, cg=GLA_CHUNK, row_block0=0,
            reset_first=False, shared_state=True, name=f"mix_p{l}")
        hs_p.append(h_p[:, 0])
        convs_p.append(tail_p[:, SUBLANES - tail:])
        ss_p.append(s_p)

        xp, xs = _ffn_call(xp2, xs2, vecs, ROW_FFN2, w_ffn2_gu, w_ffn2_down, l, final=last, tm_a=tm_p,
                           tm_b=tm_s, tf=256, name=f"ffn2_{l}")

    s_stack = ss_s[-1] if depth > 1 else ss_s[0][None]
    return (xp.reshape(bp, lp, d), xs[:ts].reshape(bs, ls, d),
            jnp.stack(hs_p), jnp.stack(convs_p), jnp.stack(ss_p),
            jnp.stack(hs_s), jnp.stack(convs_s), s_stack)
```

```python
import functools

import jax
import jax.numpy as jnp
from jax import lax
from jax.experimental import pallas as pl
from jax.experimental.pallas import tpu as pltpu

F32 = jnp.float32
BF16 = jnp.bfloat16

D_MODEL = 1024
D_FF = 2816
D_LRU = 512
LRU_BLOCKS = 8
CONV_W = 4
LRU_C = 8.0
GLA_HEADS = 4
GLA_DV = 128
GLA_DK = 64
GLA_RANK = 16
GLA_GATE_NORM = 16.0
GLA_CHUNK = 64
EPS = 1e-6
HK = GLA_HEADS * GLA_DK
HV = GLA_HEADS * GLA_DV
O_GL, O_Q, O_K, O_V, O_GO = D_LRU, 2 * D_LRU, 2 * D_LRU + HK, 2 * D_LRU + 2 * HK, 2 * D_LRU + 2 * HK + HV
D_MAIN = O_GO + HV
D_U = D_MAIN + HK
SUBLANES = 8
LANES = 128
VMEM_LIMIT = 58 * 1024 * 1024
ROW_MIX, ROW_GATE_B, ROW_LAM_CONVB, ROW_GLA, ROW_FFN1, ROW_FFN2, ROW_FINAL = range(7)


def _dot(a, b):
    if a.dtype != b.dtype:
        a = a.astype(b.dtype)
    return jnp.dot(a, b, preferred_element_type=F32)


def _dot_nt(a, b):
    return lax.dot_general(a, b, (((1,), (1,)), ((), ())), preferred_element_type=F32)


def _dot_tn(a, b):
    return lax.dot_general(a, b, (((0,), (0,)), ((), ())), preferred_element_type=F32)


def _rms(x, g):
    ms = jnp.mean(x * x, axis=-1, keepdims=True)
    return (x * lax.rsqrt(ms + EPS)) * g


def _sigmoid(x):
    return 0.5 * jnp.tanh(0.5 * x) + 0.5


def _silu(x):
    return x * _sigmoid(x)


def _log_sigmoid(x):
    return jnp.minimum(x, 0.0) - jnp.log(1.0 + jnp.exp(-jnp.abs(x)))


def _pack_vectors(norm_mix, ba, bx, lam, conv_b, b_gate, gla_norm, norm_ffn1, norm_ffn2, norm_final):
    depth, d = norm_mix.shape
    rows = [norm_mix, jnp.concatenate([ba, bx], axis=1), jnp.concatenate([lam, conv_b], axis=1),
            jnp.pad(jnp.concatenate([b_gate, gla_norm], axis=1), ((0, 0), (0, d - HK - GLA_DV))),
            norm_ffn1, norm_ffn2, jnp.broadcast_to(norm_final[None, :], (depth, d)),
            jnp.zeros((depth, d), F32)]
    return jnp.stack(rows, axis=1)


def _mixer_vectors(vecs_ref):
    row = lambda r, a, b: vecs_ref.at[r:r + 1, a:b]
    return (row(ROW_MIX, 0, D_MODEL), row(ROW_GLA, 0, HK), row(ROW_LAM_CONVB, D_LRU, 2 * D_LRU),
            row(ROW_GATE_B, 0, D_LRU), row(ROW_GATE_B, D_LRU, 2 * D_LRU), row(ROW_LAM_CONVB, 0, D_LRU),
            row(ROW_GLA, HK, HK + GLA_DV))


def _layer_spec(a, l):
    nd = a.ndim
    return pl.BlockSpec((None,) + a.shape[1:], lambda *_: (l,) + (0,) * (nd - 1),
                        pipeline_mode=pl.Buffered(1))


def _whole_spec(a):
    nd = a.ndim
    return pl.BlockSpec(a.shape, lambda *_: (0,) * nd, pipeline_mode=pl.Buffered(1))


def _main_cols_spec(l):
    return pl.BlockSpec((None, D_MODEL, D_MAIN), lambda *_: (l, 0, 0), pipeline_mode=pl.Buffered(1))


def _ffn_kernel(*refs, tf, norm_row, has_final, n_a, n_cast):
    xa_ref, xb_ref, vecs_ref, wgu_ref, wd_ref = refs[:5]
    cast_src = refs[5:5 + n_cast]
    oa_ref, ob_ref = refs[5 + n_cast:7 + n_cast]
    cast_dst = refs[7 + n_cast:]
    g_ref = vecs_ref.at[norm_row:norm_row + 1, :]
    gfin_ref = vecs_ref.at[ROW_FINAL:ROW_FINAL + 1, :]

    def tile(x_ref, xo_ref):
        x = x_ref[...]
        xn = _rms(x, g_ref[...])
        acc = None
        for c in range(D_FF // tf):
            gate = _dot(xn, wgu_ref[:, c * tf:(c + 1) * tf])
            up = _dot(xn, wgu_ref[:, D_FF + c * tf:D_FF + (c + 1) * tf])
            d = _dot(_silu(gate) * up, wd_ref[c * tf:(c + 1) * tf, :])
            acc = d if acc is None else acc + d
        x = x + 0.5 * acc
        xo_ref[...] = _rms(x, gfin_ref[...]) if has_final else x

    i = pl.program_id(0)

    @pl.when(i < n_a)
    def _():
        tile(xa_ref, oa_ref)
        for s_ref, d_ref in zip(cast_src, cast_dst):
            d_ref[...] = s_ref[...].astype(d_ref.dtype)

    @pl.when(i >= n_a)
    def _():
        tile(xb_ref, ob_ref)


def _ffn_call(xa, xb, vecs, norm_row, wgu, wd, l, *, final, tm_a, tm_b, tf, name, cast=()):
    (ta, d), (tb, _) = xa.shape, xb.shape
    assert ta % tm_a == 0 and tb % tm_b == 0 and D_FF % tf == 0
    n_a, n_b = ta // tm_a, tb // tm_b
    rows_a = pl.BlockSpec((tm_a, d), lambda i: (jnp.minimum(i, n_a - 1), 0))
    rows_b = pl.BlockSpec((tm_b, d), lambda i: (jnp.maximum(i - n_a, 0), 0))
    args = [xa, xb, vecs, wgu, wd] + list(cast)
    specs = [rows_a, rows_b, _layer_spec(vecs, l), _layer_spec(wgu, l), _layer_spec(wd, l)]
    out_specs = [rows_a, rows_b]
    out_shape = [jax.ShapeDtypeStruct((ta, d), F32), jax.ShapeDtypeStruct((tb, d), F32)]
    for w in cast:
        _, r, c = w.shape
        assert r % n_a == 0 and (r // n_a) % (2 * SUBLANES) == 0
        slab = lambda i: (jnp.minimum(i, n_a - 1), 0)
        specs.append(pl.BlockSpec((None, r // n_a, c), lambda i: (l, jnp.minimum(i, n_a - 1), 0)))
        out_specs.append(pl.BlockSpec((r // n_a, c), slab))
        out_shape.append(jax.ShapeDtypeStruct((r, c), BF16))
    return pl.pallas_call(
        functools.partial(_ffn_kernel, tf=tf, norm_row=norm_row, has_final=final, n_a=n_a,
                          n_cast=len(cast)),
        grid=(n_a + n_b,), in_specs=specs, out_specs=out_specs, out_shape=out_shape,
        compiler_params=pltpu.CompilerParams(dimension_semantics=("arbitrary",),
                                             vmem_limit_bytes=VMEM_LIMIT),
        name=name,
    )(*args)


def _in_projection(x, gmix_ref, wmain_ref, wlr_ref, wg2_ref, bg_ref, u_sc):
    hn = _rms(x, gmix_ref[...]).astype(wmain_ref.dtype)
    u_sc[:, 0:D_MAIN] = _dot(hn, wmain_ref[...])
    lr = _dot(hn, wlr_ref[...])
    zg = _dot(lr, wg2_ref[...]) + bg_ref[...]
    u_sc[:, D_MAIN:D_U] = _log_sigmoid(zg) / GLA_GATE_NORM


def _lru_gates(xc, p, ba, bx, lam):
    half = xc.shape[1]
    r = _sigmoid(p[:, 0:half] + ba)
    i = _sigmoid(p[:, half:] + bx)
    log_a = -LRU_C * r * jax.nn.softplus(-lam)
    a = jnp.exp(log_a)
    y = -jnp.tanh(log_a) * (a * a + 1.0)
    mult = jnp.where(y > 0.0, y * lax.rsqrt(y), 0.0)
    return a, mult, i * xc


def _scan_groups(a, b):
    rows, cols = a.shape
    a3 = a.reshape(rows // SUBLANES, SUBLANES, cols)
    b3 = b.reshape(rows // SUBLANES, SUBLANES, cols)
    t = lax.broadcasted_iota(jnp.int32, a3.shape, 1)
    s = 1
    while s < SUBLANES:
        keep = t >= s
        b3 = jnp.where(keep, a3 * pltpu.roll(b3, s, 1) + b3, b3)
        a3 = jnp.where(keep, a3 * pltpu.roll(a3, s, 1), a3)
        s *= 2
    return a3, b3


def _scan_rows(a, b, h0):
    a3, b3 = _scan_groups(a, b)
    carry, out = h0, []
    for g in range(a3.shape[0]):
        hg = a3[g] * carry + b3[g]
        out.append(hg)
        carry = hg[SUBLANES - 1:SUBLANES]
    return out[0] if len(out) == 1 else jnp.concatenate(out, axis=0)


def _cumsum_rows(x, t, n):
    s = 1
    while s < n:
        x = x + jnp.where(t >= s, pltpu.roll(x, s, 0), 0.0)
        s *= 2
    return x


def _head_stack(q_s, lane_head):
    return jnp.concatenate([jnp.where(lane_head == hd, q_s, 0.0) for hd in range(GLA_HEADS)], axis=0)


def _gla_kv(k_end, v, per_head):
    if per_head:
        return [_dot_tn(k_end[:, hd * GLA_DK:(hd + 1) * GLA_DK], v[:, hd * GLA_DV:(hd + 1) * GLA_DV])
                for hd in range(GLA_HEADS)]
    out = []
    for pr in range(GLA_HEADS // 2):
        kv = _dot_tn(k_end[:, 2 * pr * GLA_DK:2 * (pr + 1) * GLA_DK], v[:, 2 * pr * GLA_DV:2 * (pr + 1) * GLA_DV])
        out += [kv[0:GLA_DK, 0:GLA_DV], kv[GLA_DK:2 * GLA_DK, GLA_DV:2 * GLA_DV]]
    return out


def _state_decay(el):
    return jnp.broadcast_to(el, (LANES, HK)).T


def _gla_next_state(el_t, s_all, kv):
    return [el_t[hd * GLA_DK:(hd + 1) * GLA_DK] * s_all[hd * GLA_DK:(hd + 1) * GLA_DK] + kv[hd]
            for hd in range(GLA_HEADS)]


def _gla_intra(att_raw, causal, o_inter, v, n):
    att = jnp.where(causal, att_raw, 0.0)
    return [o_inter[hd * n:(hd + 1) * n] + _dot(att[hd * n:(hd + 1) * n], v[:, hd * GLA_DV:(hd + 1) * GLA_DV])
            for hd in range(GLA_HEADS)]


def _gla_chunk(q_s, k_s, k_end, el, v, s_all, causal, lane_head, n):
    stack = _head_stack(q_s, lane_head)
    o_inter = _dot(stack, s_all)
    att_raw = _dot_nt(stack, k_s)
    kv = _gla_kv(k_end, v, per_head=False)
    el_t = _state_decay(el)
    return _gla_intra(att_raw, causal, o_inter, v, n), _gla_next_state(el_t, s_all, kv)


def _gla_output(o, go, gn):
    return _rms(o, gn) * _silu(go)


def _mixer_seq_kernel(*refs, tc, cg, nj, reset_first):
    (xin_ref, frame0_ref, h0_ref, s0_ref, vecs_ref, wmain_ref, wlr_ref, wg2_ref, convw_ref, wg_ref,
     wout_ref) = refs[:11]
    gmix_ref, bg_ref, convb_ref, ba_ref, bx_ref, lam_ref, gn_ref = _mixer_vectors(vecs_ref)
    xo_ref, hl_ref, so_ref, tailo_ref, u, z, tail_sc, h_sc, s_sc = refs[11:]
    c = pl.program_id(0)

    @pl.when(c % nj == 0)
    def _():
        tail_sc[...] = frame0_ref[...]
        h_sc[...] = h0_ref[...]
        s_sc[...] = s0_ref[...].reshape(HK, GLA_DV)

    x = xin_ref[...]
    hn = _rms(x, gmix_ref[...]).astype(wmain_ref.dtype)
    tile_w = 2 * LANES
    carry = {}
    half = D_LRU // 2

    def in_tile(t):
        cols = slice(t * tile_w, (t + 1) * tile_w)
        u[:, cols] = _dot(hn, wmain_ref[:, cols])

    def gate_tile():
        lr = _dot(hn, wlr_ref[...])
        zg = _dot(lr, wg2_ref[...]) + bg_ref[...]
        u[:, D_MAIN:D_U] = _log_sigmoid(zg) / GLA_GATE_NORM

    def out_lru():
        xo_ref[...] = x + _dot(z[:, 0:D_LRU], wout_ref[0:D_LRU, :])

    def out_gla():
        xo_ref[...] += _dot(z[:, D_LRU:D_MODEL], wout_ref[D_LRU:D_MODEL, :])

    def lru_front(hf):
        cs = slice(hf * half, (hf + 1) * half)
        xl = u[:, cs]
        row8 = lax.broadcasted_iota(jnp.int32, (SUBLANES, half), 0)
        xc = convb_ref[:, cs]
        for sft in range(CONV_W - 1, -1, -1):
            if sft == 0:
                sh = xl
            else:
                rolled = pltpu.roll(xl, sft, 0)
                top = jnp.where(row8 < sft, pltpu.roll(tail_sc[:, cs], sft, 0), rolled[0:SUBLANES])
                sh = top if tc == SUBLANES else jnp.concatenate([top, rolled[SUBLANES:]], axis=0)
            xc = xc + convw_ref[CONV_W - 1 - sft:CONV_W - sft, cs] * sh
        carry["tail", hf] = xl[tc - SUBLANES:tc]
        tail_sc[:, cs] = carry["tail", hf]
        carry["xc", hf] = xc

    def lru_gate_mm(hf):
        carry["p", hf] = _dot(carry["xc", hf], wg_ref[hf])

    def lru_mid(hf):
        cs = slice(hf * half, (hf + 1) * half)
        a, mult, ix = _lru_gates(carry["xc", hf], carry["p", hf], ba_ref[:, cs], bx_ref[:, cs],
                                 lam_ref[:, cs])
        if reset_first:
            row = lax.broadcasted_iota(jnp.int32, (tc, half), 0)
            first = jnp.logical_and(row == 0, c % nj == 0)
            mult = jnp.where(first, 1.0, mult)
            a = jnp.where(first, 0.0, a)
        carry["a", hf], carry["b", hf] = a, mult * ix

    def lru_back(hf):
        cs = slice(hf * half, (hf + 1) * half)
        h = _scan_rows(carry["a", hf], carry["b", hf], h_sc[:, cs])
        carry["h", hf] = h[tc - 1:tc]
        h_sc[:, cs] = carry["h", hf]
        z[:, cs] = h * jax.nn.gelu(u[:, O_GL + hf * half:O_GL + (hf + 1) * half])

    lane_head = lax.broadcasted_iota(jnp.int32, (cg, HK), 1) // GLA_DK
    n_sub = tc // cg
    sub = lambda ci: slice(ci * cg, (ci + 1) * cg)

    def gla_prep(ci):
        tq = lax.broadcasted_iota(jnp.int32, (cg, HK), 0)
        b = _cumsum_rows(u[sub(ci), D_MAIN:D_U], tq, cg)
        bl = b[cg - 1:cg]
        k = u[sub(ci), O_K:O_K + HK]
        carry["stack", ci] = _head_stack(u[sub(ci), O_Q:O_Q + HK] * (GLA_DK ** -0.5) * jnp.exp(b),
                                         lane_head)
        k_s = k * jnp.exp(-b)
        carry["ks_t", ci] = jnp.concatenate([k_s, jnp.zeros((LANES - cg, HK), F32)], axis=0).T
        carry["ke", ci], carry["el", ci] = k * jnp.exp(bl - b), jnp.exp(bl)

    def gla_free_mm(ci):
        carry["kv", ci] = _gla_kv(carry["ke", ci], u[sub(ci), O_V:O_V + HV], per_head=True)
        carry["el_t", ci] = _state_decay(carry["el", ci])

    def gla_state(ci):
        s_all = carry["S"]
        both = _dot(carry["stack", ci], jnp.concatenate([s_all, carry["ks_t", ci]], axis=1))
        carry["oi", ci], carry["att", ci] = both[:, 0:GLA_DV], both[:, GLA_DV:GLA_DV + cg]
        carry["S"] = jnp.concatenate(_gla_next_state(carry["el_t", ci], s_all, carry["kv", ci]), axis=0)

    def gla_out(ci):
        causal = ((lax.broadcasted_iota(jnp.int32, (GLA_HEADS * cg, cg), 0) % cg)
                  >= lax.broadcasted_iota(jnp.int32, (GLA_HEADS * cg, cg), 1))
        o = _gla_intra(carry["att", ci], causal, carry["oi", ci], u[sub(ci), O_V:O_V + HV], cg)
        for hd in range(GLA_HEADS):
            z[sub(ci), D_LRU + hd * GLA_DV:D_LRU + (hd + 1) * GLA_DV] = _gla_output(
                o[hd], u[sub(ci), O_GO + hd * GLA_DV:O_GO + (hd + 1) * GLA_DV], gn_ref[...])

    def gla_begin():
        carry["S"] = s_sc[...]

    def gla_end():
        s_sc[...] = carry["S"]

    lru = {n: functools.partial(f, hf) for hf in range(2)
           for n, f in ((f"front{hf}", lru_front), (f"gmm{hf}", lru_gate_mm), (f"mid{hf}", lru_mid),
                        (f"back{hf}", lru_back))}
    tiles = [functools.partial(in_tile, t) for t in range(D_MAIN // tile_w)]
    assert len(tiles) == 10
    prep = [functools.partial(gla_prep, ci) for ci in range(n_sub)]
    free = [functools.partial(gla_free_mm, ci) for ci in range(n_sub)]
    head = (tiles[0:3] + [lru["front0"], tiles[3], lru["front1"], lru["gmm0"], lru["gmm1"],
                          tiles[4], lru["mid0"], tiles[5], gate_tile, lru["back0"], tiles[6], lru["mid1"],
                          tiles[7], lru["back1"], tiles[8]])
    state = [functools.partial(gla_state, ci) for ci in range(n_sub)]
    outs = [functools.partial(gla_out, ci) for ci in range(n_sub)]
    grp = 8
    order = head + [tiles[9], out_lru, gla_begin]
    for g0 in range(0, n_sub, grp):
        ids = range(g0, min(g0 + grp, n_sub))
        for ci in ids:
            order += [prep[ci], free[ci]]
        order += [state[ci] for ci in ids] + [outs[ci] for ci in ids]
    order += [gla_end, out_gla]
    for piece in order:
        piece()

    for hf in range(2):
        cs = slice(hf * half, (hf + 1) * half)
        hl_ref[:, cs] = carry["h", hf]
        tailo_ref[:, cs] = carry["tail", hf]
    so_ref[...] = s_sc[...].reshape(GLA_HEADS, GLA_DK, GLA_DV)


def _mixer_seq_call(x, frame0, h0, s0, mix, l, *, nb, sl, tc, cg, row_block0, reset_first,
                    shared_state, name):
    assert sl % tc == 0 and tc % cg == 0 and tc % SUBLANES == 0
    nj = sl // tc
    xrow = pl.BlockSpec((tc, D_MODEL), lambda s: (row_block0 + s, 0))
    seq = lambda s: s // nj
    if shared_state:
        st = lambda s: (0, 0, 0)
        st4 = lambda s: (0, 0, 0, 0)
    else:
        st = lambda s: (seq(s), 0, 0)
        st4 = lambda s: (seq(s), 0, 0, 0)
    (vecs, w_in, wlr, wg2, convw, wg, wout) = mix
    args = [x, frame0, h0, s0, vecs, w_in, wlr, wg2, convw, wg, wout]
    in_specs = [
        xrow,
        pl.BlockSpec((None, SUBLANES, D_LRU), st),
        pl.BlockSpec((None, 1, D_LRU), st),
        pl.BlockSpec((None, GLA_HEADS, GLA_DK, GLA_DV), st4),
        _layer_spec(vecs, l), _main_cols_spec(l), _layer_spec(wlr, l), _layer_spec(wg2, l),
        _layer_spec(convw, l), _layer_spec(wg, l), _whole_spec(wout),
    ]
    out_shape = [
        jax.ShapeDtypeStruct((nb * sl, D_MODEL), F32),
        jax.ShapeDtypeStruct((nb, 1, D_LRU), F32),
        jax.ShapeDtypeStruct((nb, GLA_HEADS, GLA_DK, GLA_DV), F32),
        jax.ShapeDtypeStruct((nb, SUBLANES, D_LRU), F32),
    ]
    out_specs = [
        pl.BlockSpec((tc, D_MODEL), lambda s: (s, 0)),
        pl.BlockSpec((None, 1, D_LRU), lambda s: (seq(s), 0, 0)),
        pl.BlockSpec((None, GLA_HEADS, GLA_DK, GLA_DV), lambda s: (seq(s), 0, 0, 0)),
        pl.BlockSpec((None, SUBLANES, D_LRU), lambda s: (seq(s), 0, 0)),
    ]
    scratch = [
        pltpu.VMEM((tc, D_U), F32),
        pltpu.VMEM((tc, D_MODEL), F32),
        pltpu.VMEM((SUBLANES, D_LRU), F32),
        pltpu.VMEM((1, D_LRU), F32),
        pltpu.VMEM((HK, GLA_DV), F32),
    ]
    kern = functools.partial(_mixer_seq_kernel, tc=tc, cg=cg, nj=nj, reset_first=reset_first)
    return pl.pallas_call(
        kern, grid=(nb * nj,), in_specs=in_specs, out_specs=out_specs, out_shape=out_shape,
        scratch_shapes=scratch,
        compiler_params=pltpu.CompilerParams(dimension_semantics=("arbitrary",),
                                             vmem_limit_bytes=VMEM_LIMIT),
        name=name,
    )(*args)


def _mixer_dec_kernel(*refs, nb, n_prev):
    (x_ref, frame_ref, h0_ref, s0_ref, vecs_ref, wmain_ref, wlr_ref, wg2_ref, convw_ref, wg_ref,
     wout_ref) = refs[:11]
    gmix_ref, bg_ref, convb_ref, ba_ref, bx_ref, lam_ref, gn_ref = _mixer_vectors(vecs_ref)
    prev_refs = refs[11:11 + n_prev]
    (xo_ref, h_ref, xl_ref, so_ref,
     u_sc, z_sc, qs_sc, ks_sc, ke_sc, el_sc, o_sc) = refs[11 + n_prev:]
    for j, prev in enumerate(prev_refs):
        so_ref[j] = prev[...]
    so_new = so_ref.at[n_prev] if n_prev else so_ref
    ls = SUBLANES
    rows = nb * ls
    x = x_ref[...]
    _in_projection(x, gmix_ref, wmain_ref, wlr_ref, wg2_ref, bg_ref, u_sc)

    t = lax.broadcasted_iota(jnp.int32, (rows, D_LRU), 0) % ls
    xl = u_sc[:, 0:D_LRU]
    xl_ref[...] = xl
    frame = frame_ref[...]
    xc = convb_ref[...]
    for s in range(CONV_W - 1, -1, -1):
        if s == 0:
            sh = xl
        else:
            sh = jnp.where(t >= s, pltpu.roll(xl, s, 0), pltpu.roll(frame, (rows - ls + s) % rows, 0))
        xc = xc + convw_ref[CONV_W - 1 - s:CONV_W - s, :] * sh
    half = D_LRU // 2
    parts = []
    for hf in range(2):
        cs = slice(hf * half, (hf + 1) * half)
        parts.append(_lru_gates(xc[:, cs], _dot(xc[:, cs], wg_ref[hf]), ba_ref[:, cs], bx_ref[:, cs],
                                lam_ref[:, cs]))
    a, mult, ix = (jnp.concatenate([p[n] for p in parts], axis=1) for n in range(3))
    a3, b3 = _scan_groups(a, mult * ix)
    h = (a3 * h0_ref[...].reshape(a3.shape) + b3).reshape(rows, D_LRU)
    h_ref[...] = h
    z_sc[:, 0:D_LRU] = h * jax.nn.gelu(u_sc[:, O_GL:O_GL + D_LRU])

    tq = lax.broadcasted_iota(jnp.int32, (rows, HK), 0) % ls
    b = _cumsum_rows(u_sc[:, D_MAIN:D_U], tq, ls)
    bl = jnp.where(tq == ls - 1, b, 0.0)
    s = 1
    while s < ls:
        bl = bl + jnp.where(tq + s < ls, pltpu.roll(bl, rows - s, 0), 0.0)
        s *= 2
    k = u_sc[:, O_K:O_K + HK]
    qs_sc[...] = (u_sc[:, O_Q:O_Q + HK] * (GLA_DK ** -0.5)) * jnp.exp(b)
    ks_sc[...] = k * jnp.exp(-b)
    ke_sc[...] = k * jnp.exp(bl - b)
    el_sc[...] = jnp.exp(bl)

    lane_head = lax.broadcasted_iota(jnp.int32, (ls, HK), 1) // GLA_DK
    causal = ((lax.broadcasted_iota(jnp.int32, (GLA_HEADS * ls, ls), 0) % ls)
              >= lax.broadcasted_iota(jnp.int32, (GLA_HEADS * ls, ls), 1))

    def body(bi, carry):
        rs = pl.ds(pl.multiple_of(bi * ls, ls), ls)
        s_all = s0_ref[bi].reshape(HK, GLA_DV)
        o, s_new = _gla_chunk(qs_sc[rs, :], ks_sc[rs, :], ke_sc[rs, :], el_sc[rs, :][0:1],
                              u_sc[rs, O_V:O_V + HV], s_all, causal, lane_head, ls)
        for hd in range(GLA_HEADS):
            so_new[bi, hd] = s_new[hd]
            o_sc[rs, hd * GLA_DV:(hd + 1) * GLA_DV] = o[hd]
        return carry

    lax.fori_loop(0, nb, body, 0, unroll=16)

    gn = gn_ref[...]
    for hd in range(GLA_HEADS):
        vs = slice(hd * GLA_DV, (hd + 1) * GLA_DV)
        z_sc[:, D_LRU + hd * GLA_DV:D_LRU + (hd + 1) * GLA_DV] = _gla_output(
            o_sc[:, vs], u_sc[:, O_GO + hd * GLA_DV:O_GO + (hd + 1) * GLA_DV], gn)
    xo_ref[...] = x + _dot(z_sc[...], wout_ref[...])


def _mixer_dec_call(x, frame, h0, s0, s_prev, mix, l, *, nseq, nb, name):
    ls = SUBLANES
    assert nseq % nb == 0
    rows = nb * ls
    (vecs, w_in, wlr, wg2, convw, wg, wout) = mix
    rowspec = lambda w: pl.BlockSpec((rows, w), lambda i: (i, 0))
    sspec = pl.BlockSpec((None, nb, GLA_HEADS, GLA_DK, GLA_DV), lambda i: (l, i, 0, 0, 0))
    in_specs = [
        rowspec(D_MODEL), rowspec(D_LRU), rowspec(D_LRU), sspec,
        _layer_spec(vecs, l), _main_cols_spec(l), _layer_spec(wlr, l), _layer_spec(wg2, l),
        _layer_spec(convw, l), _layer_spec(wg, l), _whole_spec(wout),
    ]
    out_shape = [
        jax.ShapeDtypeStruct((nseq * ls, D_MODEL), F32),
        jax.ShapeDtypeStruct((nseq * ls, D_LRU), F32),
        jax.ShapeDtypeStruct((nseq * ls, D_LRU), F32),
    ]
    state = (nseq, GLA_HEADS, GLA_DK, GLA_DV)
    sblock = pl.BlockSpec((nb,) + state[1:], lambda i: (i, 0, 0, 0))
    n_prev = len(s_prev)
    if n_prev:
        out_shape.append(jax.ShapeDtypeStruct((n_prev + 1,) + state, F32))
        s_out_spec = pl.BlockSpec((n_prev + 1, nb) + state[1:], lambda i: (0, i, 0, 0, 0))
    else:
        out_shape.append(jax.ShapeDtypeStruct(state, F32))
        s_out_spec = sblock
    out_specs = [rowspec(D_MODEL), rowspec(D_LRU), rowspec(D_LRU), s_out_spec]
    args = [x, frame, h0, s0, vecs, w_in, wlr, wg2, convw, wg, wout]
    args += list(s_prev)
    in_specs += [sblock] * n_prev
    scratch = [
        pltpu.VMEM((rows, D_U), F32), pltpu.VMEM((rows, D_MODEL), F32),
        pltpu.VMEM((rows, HK), F32), pltpu.VMEM((rows, HK), F32), pltpu.VMEM((rows, HK), F32),
        pltpu.VMEM((rows, HK), F32), pltpu.VMEM((rows, HV), F32),
    ]
    return pl.pallas_call(
        functools.partial(_mixer_dec_kernel, nb=nb, n_prev=n_prev),
        grid=(nseq // nb,), in_specs=in_specs, out_specs=out_specs, out_shape=out_shape,
        scratch_shapes=scratch,
        compiler_params=pltpu.CompilerParams(dimension_semantics=("arbitrary",),
                                             vmem_limit_bytes=VMEM_LIMIT),
        name=name,
    )(*args)


def _block_diag_gates(wa, wx):
    per = LRU_BLOCKS // 2
    bw = wa.shape[-1]
    eye = jnp.eye(per, dtype=wa.dtype)

    def bd(w):
        return jnp.einsum("lncd,nm->lncmd", w, eye).reshape(w.shape[0], per * bw, per * bw)

    halves = [jnp.concatenate([bd(wa[:, hf * per:(hf + 1) * per]), bd(wx[:, hf * per:(hf + 1) * per])],
                              axis=-1) for hf in range(2)]
    return jnp.stack(halves, axis=1)


def kernel(x_prompt, x_sample, state_lru_h, state_lru_conv, state_gla_S, meta, norm_ffn1, w_ffn1_gu,
           w_ffn1_down, norm_mix, w_in, lru_conv_w, lru_conv_b, lru_wa, lru_ba, lru_wx, lru_bx,
           lru_lambda, gla_w_gate2, gla_b_gate, gla_norm, w_out, norm_ffn2, w_ffn2_gu, w_ffn2_down,
           norm_final):
    bp, lp, d = x_prompt.shape
    bs, ls, _ = x_sample.shape
    n_meta = meta.shape[0]
    depth = w_in.shape[0]
    assert d == D_MODEL and ls == SUBLANES and n_meta % SUBLANES == 0
    tp, ts = bp * lp, bs * ls
    tsm = ts + n_meta
    assert ts % n_meta == 0
    tm_p = 512 if tp % 512 == 0 else lp
    tm_s = tsm // 2 if (tsm // 2) % SUBLANES == 0 else tsm
    tc_p = 1024 if lp % 1024 == 0 else GLA_CHUNK
    nb_s = 32 if bs % 32 == 0 else bs
    tail = CONV_W - 1

    vecs = _pack_vectors(norm_mix, lru_ba, lru_bx, lru_lambda, lru_conv_b, gla_b_gate, gla_norm,
                         norm_ffn1, norm_ffn2, norm_final)
    w_in_b = w_in.astype(BF16)
    w_lr = jnp.pad(w_in[:, :, D_MAIN:], ((0, 0), (0, 0), (0, LANES - GLA_RANK))).astype(BF16)
    w_g2 = jnp.pad(gla_w_gate2, ((0, 0), (0, LANES - GLA_RANK), (0, 0))).astype(BF16)
    w_gates = _block_diag_gates(lru_wa, lru_wx).astype(BF16)
    frames_s = jnp.pad(state_lru_conv, ((0, 0), (0, 0), (SUBLANES - tail, 0), (0, 0)))
    frames_s = frames_s.reshape(depth, ts, D_LRU)
    h0_s = jnp.broadcast_to(state_lru_h[:, :, None, :], (depth, bs, ls, D_LRU)).reshape(depth, ts, D_LRU)
    zero_frame = jnp.zeros((1, SUBLANES, D_LRU), F32)
    zero_h = jnp.zeros((1, 1, D_LRU), F32)
    zero_s = jnp.zeros((1, GLA_HEADS, GLA_DK, GLA_DV), F32)

    xp = x_prompt.reshape(tp, d)
    xs = jnp.concatenate([x_sample.reshape(ts, d), meta.astype(F32)], axis=0)

    hs_p, convs_p, ss_p, hs_s, convs_s, ss_s = [], [], [], [], [], []
    for l in range(depth):
        xp1, xs1, w_out_l = _ffn_call(xp, xs, vecs, ROW_FFN1, w_ffn1_gu, w_ffn1_down, l, final=False,
                                      tm_a=tm_p, tm_b=tm_s, tf=256, name=f"ffn1_{l}", cast=(w_out,))
        mix = (vecs, w_in_b, w_lr, w_g2, lru_conv_w, w_gates, w_out_l)

        last = l == depth - 1
        xs2, h_s, xl_s, s_s = _mixer_dec_call(xs1, frames_s[l], h0_s[l], state_gla_S,
                                              ss_s if last else (), mix, l, nseq=bs, nb=nb_s,
                                              name=f"mix_s{l}")
        ss_s.append(s_s)
        hs_s.append(h_s.reshape(bs, ls, D_LRU)[:, ls - 1])
        convs_s.append(xl_s.reshape(bs, ls, D_LRU)[:, ls - tail:])

        xm2, h_m, s_m, tail_m = _mixer_seq_call(
            xs1, zero_frame, zero_h, zero_s, mix, l, nb=1, sl=n_meta, tc=n_meta, cg=n_meta,
            row_block0=ts // n_meta, reset_first=True, shared_state=False, name=f"mix_m{l}")
        xs2 = jnp.concatenate([xs2, xm2], axis=0)

        xp2, h_p, s_p, tail_p = _mixer_seq_call(
            xp1, tail_m, h_m, s_m, mix, l, nb=bp, sl=lp, tc=tc_p, cg=GLA_CHUNK, row_block0=0,
            reset_first=False, shared_state=True, name=f"mix_p{l}")
        hs_p.append(h_p[:, 0])
        convs_p.append(tail_p[:, SUBLANES - tail:])
        ss_p.append(s_p)

        xp, xs = _ffn_call(xp2, xs2, vecs, ROW_FFN2, w_ffn2_gu, w_ffn2_down, l, final=last, tm_a=tm_p,
                           tm_b=tm_s, tf=256, name=f"ffn2_{l}")

    s_stack = ss_s[-1] if depth > 1 else ss_s[0][None]
    return (xp.reshape(bp, lp, d), xs[:ts].reshape(bs, ls, d),
            jnp.stack(hs_p), jnp.stack(convs_p), jnp.stack(ss_p),
            jnp.stack(hs_s), jnp.stack(convs_s), s_stack)
```

```python
import functools

import jax
import jax.numpy as jnp
from jax import lax
from jax.experimental import pallas as pl
from jax.experimental.pallas import tpu as pltpu

F32 = jnp.float32
BF16 = jnp.bfloat16

D_MODEL = 1024
D_FF = 2816
D_LRU = 512
LRU_BLOCKS = 8
CONV_W = 4
LRU_C = 8.0
GLA_HEADS = 4
GLA_DV = 128
GLA_DK = 64
GLA_RANK = 16
GLA_GATE_NORM = 16.0
GLA_CHUNK = 64
EPS = 1e-6
HK = GLA_HEADS * GLA_DK
HV = GLA_HEADS * GLA_DV
O_GL, O_Q, O_K, O_V, O_GO = D_LRU, 2 * D_LRU, 2 * D_LRU + HK, 2 * D_LRU + 2 * HK, 2 * D_LRU + 2 * HK + HV
D_MAIN = O_GO + HV
D_U = D_MAIN + HK
SUBLANES = 8
LANES = 128
VMEM_LIMIT = 58 * 1024 * 1024
ROW_MIX, ROW_GATE_B, ROW_LAM_CONVB, ROW_GLA, ROW_FFN1, ROW_FFN2, ROW_FINAL = range(7)


def _dot(a, b):
    if a.dtype != b.dtype:
        a = a.astype(b.dtype)
    return jnp.dot(a, b, preferred_element_type=F32)


def _dot_nt(a, b):
    return lax.dot_general(a, b, (((1,), (1,)), ((), ())), preferred_element_type=F32)


def _dot_tn(a, b):
    return lax.dot_general(a, b, (((0,), (0,)), ((), ())), preferred_element_type=F32)


def _rms(x, g):
    ms = jnp.mean(x * x, axis=-1, keepdims=True)
    return (x * lax.rsqrt(ms + EPS)) * g


def _sigmoid(x):
    return 0.5 * jnp.tanh(0.5 * x) + 0.5


def _silu(x):
    return x * _sigmoid(x)


def _log_sigmoid(x):
    return jnp.minimum(x, 0.0) - jnp.log(1.0 + jnp.exp(-jnp.abs(x)))


def _pack_vectors(norm_mix, ba, bx, lam, conv_b, b_gate, gla_norm, norm_ffn1, norm_ffn2, norm_final):
    depth, d = norm_mix.shape
    rows = [norm_mix, jnp.concatenate([ba, bx], axis=1), jnp.concatenate([lam, conv_b], axis=1),
            jnp.pad(jnp.concatenate([b_gate, gla_norm], axis=1), ((0, 0), (0, d - HK - GLA_DV))),
            norm_ffn1, norm_ffn2, jnp.broadcast_to(norm_final[None, :], (depth, d)),
            jnp.zeros((depth, d), F32)]
    return jnp.stack(rows, axis=1)


def _mixer_vectors(vecs_ref):
    row = lambda r, a, b: vecs_ref.at[r:r + 1, a:b]
    return (row(ROW_MIX, 0, D_MODEL), row(ROW_GLA, 0, HK), row(ROW_LAM_CONVB, D_LRU, 2 * D_LRU),
            row(ROW_GATE_B, 0, D_LRU), row(ROW_GATE_B, D_LRU, 2 * D_LRU), row(ROW_LAM_CONVB, 0, D_LRU),
            row(ROW_GLA, HK, HK + GLA_DV))


def _layer_spec(a, l):
    nd = a.ndim
    return pl.BlockSpec((None,) + a.shape[1:], lambda *_: (l,) + (0,) * (nd - 1),
                        pipeline_mode=pl.Buffered(1))


def _whole_spec(a):
    nd = a.ndim
    return pl.BlockSpec(a.shape, lambda *_: (0,) * nd, pipeline_mode=pl.Buffered(1))


def _main_cols_spec(l):
    return pl.BlockSpec((None, D_MODEL, D_MAIN), lambda *_: (l, 0, 0), pipeline_mode=pl.Buffered(1))


def _ffn_kernel(*refs, tf, norm_row, has_final, n_a, n_cast):
    xa_ref, xb_ref, vecs_ref, wgu_ref, wd_ref = refs[:5]
    cast_src = refs[5:5 + n_cast]
    oa_ref, ob_ref = refs[5 + n_cast:7 + n_cast]
    cast_dst = refs[7 + n_cast:]
    g_ref = vecs_ref.at[norm_row:norm_row + 1, :]
    gfin_ref = vecs_ref.at[ROW_FINAL:ROW_FINAL + 1, :]

    def tile(x_ref, xo_ref):
        x = x_ref[...]
        xn = _rms(x, g_ref[...])
        acc = None
        for c in range(D_FF // tf):
            gate = _dot(xn, wgu_ref[:, c * tf:(c + 1) * tf])
            up = _dot(xn, wgu_ref[:, D_FF + c * tf:D_FF + (c + 1) * tf])
            d = _dot(_silu(gate) * up, wd_ref[c * tf:(c + 1) * tf, :])
            acc = d if acc is None else acc + d
        x = x + 0.5 * acc
        xo_ref[...] = _rms(x, gfin_ref[...]) if has_final else x

    i = pl.program_id(0)

    @pl.when(i < n_a)
    def _():
        tile(xa_ref, oa_ref)
        for s_ref, d_ref in zip(cast_src, cast_dst):
            d_ref[...] = s_ref[...].astype(d_ref.dtype)

    @pl.when(i >= n_a)
    def _():
        tile(xb_ref, ob_ref)


def _ffn_call(xa, xb, vecs, norm_row, wgu, wd, l, *, final, tm_a, tm_b, tf, name, cast=()):
    (ta, d), (tb, _) = xa.shape, xb.shape
    assert ta % tm_a == 0 and tb % tm_b == 0 and D_FF % tf == 0
    n_a, n_b = ta // tm_a, tb // tm_b
    rows_a = pl.BlockSpec((tm_a, d), lambda i: (jnp.minimum(i, n_a - 1), 0))
    rows_b = pl.BlockSpec((tm_b, d), lambda i: (jnp.maximum(i - n_a, 0), 0))
    args = [xa, xb, vecs, wgu, wd] + list(cast)
    specs = [rows_a, rows_b, _layer_spec(vecs, l), _layer_spec(wgu, l), _layer_spec(wd, l)]
    out_specs = [rows_a, rows_b]
    out_shape = [jax.ShapeDtypeStruct((ta, d), F32), jax.ShapeDtypeStruct((tb, d), F32)]
    for w in cast:
        _, r, c = w.shape
        assert r % n_a == 0 and (r // n_a) % (2 * SUBLANES) == 0
        slab = lambda i: (jnp.minimum(i, n_a - 1), 0)
        specs.append(pl.BlockSpec((None, r // n_a, c), lambda i: (l, jnp.minimum(i, n_a - 1), 0)))
        out_specs.append(pl.BlockSpec((r // n_a, c), slab))
        out_shape.append(jax.ShapeDtypeStruct((r, c), BF16))
    return pl.pallas_call(
        functools.partial(_ffn_kernel, tf=tf, norm_row=norm_row, has_final=final, n_a=n_a,
                          n_cast=len(cast)),
        grid=(n_a + n_b,), in_specs=specs, out_specs=out_specs, out_shape=out_shape,
        compiler_params=pltpu.CompilerParams(dimension_semantics=("arbitrary",),
                                             vmem_limit_bytes=VMEM_LIMIT),
        name=name,
    )(*args)


def _in_projection(x, gmix_ref, wmain_ref, wlr_ref, wg2_ref, bg_ref, u_sc):
    hn = _rms(x, gmix_ref[...]).astype(wmain_ref.dtype)
    u_sc[:, 0:D_MAIN] = _dot(hn, wmain_ref[...])
    lr = _dot(hn, wlr_ref[...])
    zg = _dot(lr, wg2_ref[...]) + bg_ref[...]
    u_sc[:, D_MAIN:D_U] = _log_sigmoid(zg) / GLA_GATE_NORM


def _lru_gates(xc, p, ba, bx, lam):
    half = xc.shape[1]
    r = _sigmoid(p[:, 0:half] + ba)
    i = _sigmoid(p[:, half:] + bx)
    log_a = -LRU_C * r * jax.nn.softplus(-lam)
    a = jnp.exp(log_a)
    y = -jnp.tanh(log_a) * (a * a + 1.0)
    mult = jnp.where(y > 0.0, y * lax.rsqrt(y), 0.0)
    return a, mult, i * xc


def _scan_groups(a, b):
    rows, cols = a.shape
    a3 = a.reshape(rows // SUBLANES, SUBLANES, cols)
    b3 = b.reshape(rows // SUBLANES, SUBLANES, cols)
    t = lax.broadcasted_iota(jnp.int32, a3.shape, 1)
    s = 1
    while s < SUBLANES:
        keep = t >= s
        b3 = jnp.where(keep, a3 * pltpu.roll(b3, s, 1) + b3, b3)
        a3 = jnp.where(keep, a3 * pltpu.roll(a3, s, 1), a3)
        s *= 2
    return a3, b3


def _scan_rows(a, b, h0):
    a3, b3 = _scan_groups(a, b)
    carry, out = h0, []
    for g in range(a3.shape[0]):
        hg = a3[g] * carry + b3[g]
        out.append(hg)
        carry = hg[SUBLANES - 1:SUBLANES]
    return out[0] if len(out) == 1 else jnp.concatenate(out, axis=0)


def _cumsum_rows(x, t, n):
    s = 1
    while s < n:
        x = x + jnp.where(t >= s, pltpu.roll(x, s, 0), 0.0)
        s *= 2
    return x


def _head_stack(q_s, lane_head):
    return jnp.concatenate([jnp.where(lane_head == hd, q_s, 0.0) for hd in range(GLA_HEADS)], axis=0)


def _gla_kv(k_end, v, per_head):
    if per_head:
        return [_dot_tn(k_end[:, hd * GLA_DK:(hd + 1) * GLA_DK], v[:, hd * GLA_DV:(hd + 1) * GLA_DV])
                for hd in range(GLA_HEADS)]
    out = []
    for pr in range(GLA_HEADS // 2):
        kv = _dot_tn(k_end[:, 2 * pr * GLA_DK:2 * (pr + 1) * GLA_DK], v[:, 2 * pr * GLA_DV:2 * (pr + 1) * GLA_DV])
        out += [kv[0:GLA_DK, 0:GLA_DV], kv[GLA_DK:2 * GLA_DK, GLA_DV:2 * GLA_DV]]
    return out


def _state_decay(el):
    return jnp.broadcast_to(el, (LANES, HK)).T


def _gla_next_state(el_t, s_all, kv):
    return [el_t[hd * GLA_DK:(hd + 1) * GLA_DK] * s_all[hd * GLA_DK:(hd + 1) * GLA_DK] + kv[hd]
            for hd in range(GLA_HEADS)]


def _gla_intra(att_raw, causal, o_inter, v, n):
    att = jnp.where(causal, att_raw, 0.0)
    return [o_inter[hd * n:(hd + 1) * n] + _dot(att[hd * n:(hd + 1) * n], v[:, hd * GLA_DV:(hd + 1) * GLA_DV])
            for hd in range(GLA_HEADS)]


def _gla_chunk(q_s, k_s, k_end, el, v, s_all, causal, lane_head, n):
    stack = _head_stack(q_s, lane_head)
    o_inter = _dot(stack, s_all)
    att_raw = _dot_nt(stack, k_s)
    kv = _gla_kv(k_end, v, per_head=False)
    el_t = _state_decay(el)
    return _gla_intra(att_raw, causal, o_inter, v, n), _gla_next_state(el_t, s_all, kv)


def _gla_output(o, go, gn):
    return _rms(o, gn) * _silu(go)


def _mixer_seq_kernel(*refs, tc, cg, nj, reset_first):
    (xin_ref, frame0_ref, h0_ref, s0_ref, vecs_ref, wmain_ref, wlr_ref, wg2_ref, convw_ref, wg_ref,
     wout_ref) = refs[:11]
    gmix_ref, bg_ref, convb_ref, ba_ref, bx_ref, lam_ref, gn_ref = _mixer_vectors(vecs_ref)
    xo_ref, hl_ref, so_ref, tailo_ref, u, z, hn_sc, tail_sc, h_sc, s_sc = refs[11:]
    c = pl.program_id(0)

    @pl.when(c % nj == 0)
    def _():
        tail_sc[...] = frame0_ref[...]
        h_sc[...] = h0_ref[...]
        s_sc[...] = s0_ref[...].reshape(HK, GLA_DV)

    hn_sc[...] = _rms(xin_ref[...], gmix_ref[...]).astype(hn_sc.dtype)
    tile_w = 2 * LANES
    carry = {}
    half = D_LRU // 2

    def in_tile(t):
        cols = slice(t * tile_w, (t + 1) * tile_w)
        u[:, cols] = _dot(hn_sc[...], wmain_ref[:, cols])

    def gate_tile():
        lr = _dot(hn_sc[...], wlr_ref[...])
        zg = _dot(lr, wg2_ref[...]) + bg_ref[...]
        u[:, D_MAIN:D_U] = _log_sigmoid(zg) / GLA_GATE_NORM

    def out_lru():
        xo_ref[...] = xin_ref[...] + _dot(z[:, 0:D_LRU], wout_ref[0:D_LRU, :])

    def out_gla():
        xo_ref[...] += _dot(z[:, D_LRU:D_MODEL], wout_ref[D_LRU:D_MODEL, :])

    def lru_front(hf):
        cs = slice(hf * half, (hf + 1) * half)
        xl = u[:, cs]
        row8 = lax.broadcasted_iota(jnp.int32, (SUBLANES, half), 0)
        xc = convb_ref[:, cs]
        for sft in range(CONV_W - 1, -1, -1):
            if sft == 0:
                sh = xl
            else:
                rolled = pltpu.roll(xl, sft, 0)
                top = jnp.where(row8 < sft, pltpu.roll(tail_sc[:, cs], sft, 0), rolled[0:SUBLANES])
                sh = top if tc == SUBLANES else jnp.concatenate([top, rolled[SUBLANES:]], axis=0)
            xc = xc + convw_ref[CONV_W - 1 - sft:CONV_W - sft, cs] * sh
        carry["tail", hf] = xl[tc - SUBLANES:tc]
        tail_sc[:, cs] = carry["tail", hf]
        carry["xc", hf] = xc

    def lru_gate_mm(hf):
        carry["p", hf] = _dot(carry["xc", hf], wg_ref[hf])

    def lru_mid(hf):
        cs = slice(hf * half, (hf + 1) * half)
        a, mult, ix = _lru_gates(carry["xc", hf], carry["p", hf], ba_ref[:, cs], bx_ref[:, cs],
                                 lam_ref[:, cs])
        if reset_first:
            row = lax.broadcasted_iota(jnp.int32, (tc, half), 0)
            first = jnp.logical_and(row == 0, c % nj == 0)
            mult = jnp.where(first, 1.0, mult)
            a = jnp.where(first, 0.0, a)
        carry["a", hf], carry["b", hf] = a, mult * ix

    def lru_back(hf):
        cs = slice(hf * half, (hf + 1) * half)
        h = _scan_rows(carry["a", hf], carry["b", hf], h_sc[:, cs])
        carry["h", hf] = h[tc - 1:tc]
        h_sc[:, cs] = carry["h", hf]
        z[:, cs] = h * jax.nn.gelu(u[:, O_GL + hf * half:O_GL + (hf + 1) * half])

    lane_head = lax.broadcasted_iota(jnp.int32, (cg, HK), 1) // GLA_DK
    n_sub = tc // cg
    sub = lambda ci: slice(ci * cg, (ci + 1) * cg)

    def gla_prep(ci):
        tq = lax.broadcasted_iota(jnp.int32, (cg, HK), 0)
        b = _cumsum_rows(u[sub(ci), D_MAIN:D_U], tq, cg)
        bl = b[cg - 1:cg]
        k = u[sub(ci), O_K:O_K + HK]
        carry["stack", ci] = _head_stack(u[sub(ci), O_Q:O_Q + HK] * (GLA_DK ** -0.5) * jnp.exp(b),
                                         lane_head)
        k_s = k * jnp.exp(-b)
        carry["ks_t", ci] = jnp.concatenate([k_s, jnp.zeros((LANES - cg, HK), F32)], axis=0).T
        carry["ke", ci], carry["el", ci] = k * jnp.exp(bl - b), jnp.exp(bl)

    def gla_free_mm(ci):
        carry["kv", ci] = _gla_kv(carry["ke", ci], u[sub(ci), O_V:O_V + HV], per_head=True)
        carry["el_t", ci] = _state_decay(carry["el", ci])

    def gla_state(ci):
        s_all = carry["S"]
        both = _dot(carry["stack", ci], jnp.concatenate([s_all, carry["ks_t", ci]], axis=1))
        carry["oi", ci], carry["att", ci] = both[:, 0:GLA_DV], both[:, GLA_DV:GLA_DV + cg]
        carry["S"] = jnp.concatenate(_gla_next_state(carry["el_t", ci], s_all, carry["kv", ci]), axis=0)

    def gla_out(ci):
        causal = ((lax.broadcasted_iota(jnp.int32, (GLA_HEADS * cg, cg), 0) % cg)
                  >= lax.broadcasted_iota(jnp.int32, (GLA_HEADS * cg, cg), 1))
        o = _gla_intra(carry["att", ci], causal, carry["oi", ci], u[sub(ci), O_V:O_V + HV], cg)
        for hd in range(GLA_HEADS):
            z[sub(ci), D_LRU + hd * GLA_DV:D_LRU + (hd + 1) * GLA_DV] = _gla_output(
                o[hd], u[sub(ci), O_GO + hd * GLA_DV:O_GO + (hd + 1) * GLA_DV], gn_ref[...])

    def gla_begin():
        carry["S"] = s_sc[...]

    def gla_end():
        s_sc[...] = carry["S"]

    lru = {n: functools.partial(f, hf) for hf in range(2)
           for n, f in ((f"front{hf}", lru_front), (f"gmm{hf}", lru_gate_mm), (f"mid{hf}", lru_mid),
                        (f"back{hf}", lru_back))}
    tiles = [functools.partial(in_tile, t) for t in range(D_MAIN // tile_w)]
    assert len(tiles) == 10
    prep = [functools.partial(gla_prep, ci) for ci in range(n_sub)]
    free = [functools.partial(gla_free_mm, ci) for ci in range(n_sub)]
    head = (tiles[0:3] + [lru["front0"], tiles[3], lru["front1"], lru["gmm0"], lru["gmm1"],
                          tiles[4], lru["mid0"], tiles[5], gate_tile, lru["back0"], tiles[6], lru["mid1"],
                          tiles[7], lru["back1"], tiles[8]])
    state = [functools.partial(gla_state, ci) for ci in range(n_sub)]
    outs = [functools.partial(gla_out, ci) for ci in range(n_sub)]
    grp = 8
    order = head + [tiles[9], out_lru, gla_begin]
    rounds = [range(g0, min(g0 + grp, n_sub)) for g0 in range(0, n_sub, grp)]
    pending = []
    for ids in rounds:
        for ci in ids:
            order += [prep[ci], free[ci]]
        order += pending + [state[ci] for ci in ids]
        pending = [outs[ci] for ci in ids]
    order += pending + [gla_end, out_gla]
    for piece in order:
        piece()

    for hf in range(2):
        cs = slice(hf * half, (hf + 1) * half)
        hl_ref[:, cs] = carry["h", hf]
        tailo_ref[:, cs] = carry["tail", hf]
    so_ref[...] = s_sc[...].reshape(GLA_HEADS, GLA_DK, GLA_DV)


def _mixer_seq_call(x, frame0, h0, s0, mix, l, *, nb, sl, tc, cg, row_block0, reset_first,
                    shared_state, name):
    assert sl % tc == 0 and tc % cg == 0 and tc % SUBLANES == 0
    nj = sl // tc
    xrow = pl.BlockSpec((tc, D_MODEL), lambda s: (row_block0 + s, 0))
    seq = lambda s: s // nj
    if shared_state:
        st = lambda s: (0, 0, 0)
        st4 = lambda s: (0, 0, 0, 0)
    else:
        st = lambda s: (seq(s), 0, 0)
        st4 = lambda s: (seq(s), 0, 0, 0)
    (vecs, w_in, wlr, wg2, convw, wg, wout) = mix
    args = [x, frame0, h0, s0, vecs, w_in, wlr, wg2, convw, wg, wout]
    in_specs = [
        xrow,
        pl.BlockSpec((None, SUBLANES, D_LRU), st),
        pl.BlockSpec((None, 1, D_LRU), st),
        pl.BlockSpec((None, GLA_HEADS, GLA_DK, GLA_DV), st4),
        _layer_spec(vecs, l), _main_cols_spec(l), _layer_spec(wlr, l), _layer_spec(wg2, l),
        _layer_spec(convw, l), _layer_spec(wg, l), _whole_spec(wout),
    ]
    out_shape = [
        jax.ShapeDtypeStruct((nb * sl, D_MODEL), F32),
        jax.ShapeDtypeStruct((nb, 1, D_LRU), F32),
        jax.ShapeDtypeStruct((nb, GLA_HEADS, GLA_DK, GLA_DV), F32),
        jax.ShapeDtypeStruct((nb, SUBLANES, D_LRU), F32),
    ]
    out_specs = [
        pl.BlockSpec((tc, D_MODEL), lambda s: (s, 0)),
        pl.BlockSpec((None, 1, D_LRU), lambda s: (seq(s), 0, 0)),
        pl.BlockSpec((None, GLA_HEADS, GLA_DK, GLA_DV), lambda s: (seq(s), 0, 0, 0)),
        pl.BlockSpec((None, SUBLANES, D_LRU), lambda s: (seq(s), 0, 0)),
    ]
    scratch = [
        pltpu.VMEM((tc, D_U), F32),
        pltpu.VMEM((tc, D_MODEL), F32),
        pltpu.VMEM((tc, D_MODEL), w_in.dtype),
        pltpu.VMEM((SUBLANES, D_LRU), F32),
        pltpu.VMEM((1, D_LRU), F32),
        pltpu.VMEM((HK, GLA_DV), F32),
    ]
    kern = functools.partial(_mixer_seq_kernel, tc=tc, cg=cg, nj=nj, reset_first=reset_first)
    return pl.pallas_call(
        kern, grid=(nb * nj,), in_specs=in_specs, out_specs=out_specs, out_shape=out_shape,
        scratch_shapes=scratch,
        compiler_params=pltpu.CompilerParams(dimension_semantics=("arbitrary",),
                                             vmem_limit_bytes=VMEM_LIMIT),
        name=name,
    )(*args)


def _mixer_dec_kernel(*refs, nb, n_prev):
    (x_ref, frame_ref, h0_ref, s0_ref, vecs_ref, wmain_ref, wlr_ref, wg2_ref, convw_ref, wg_ref,
     wout_ref) = refs[:11]
    gmix_ref, bg_ref, convb_ref, ba_ref, bx_ref, lam_ref, gn_ref = _mixer_vectors(vecs_ref)
    prev_refs = refs[11:11 + n_prev]
    (xo_ref, h_ref, xl_ref, so_ref,
     u_sc, z_sc, qs_sc, ks_sc, ke_sc, el_sc, o_sc) = refs[11 + n_prev:]
    for j, prev in enumerate(prev_refs):
        so_ref[j] = prev[...]
    so_new = so_ref.at[n_prev] if n_prev else so_ref
    ls = SUBLANES
    rows = nb * ls
    x = x_ref[...]
    _in_projection(x, gmix_ref, wmain_ref, wlr_ref, wg2_ref, bg_ref, u_sc)

    t = lax.broadcasted_iota(jnp.int32, (rows, D_LRU), 0) % ls
    xl = u_sc[:, 0:D_LRU]
    xl_ref[...] = xl
    frame = frame_ref[...]
    xc = convb_ref[...]
    for s in range(CONV_W - 1, -1, -1):
        if s == 0:
            sh = xl
        else:
            sh = jnp.where(t >= s, pltpu.roll(xl, s, 0), pltpu.roll(frame, (rows - ls + s) % rows, 0))
        xc = xc + convw_ref[CONV_W - 1 - s:CONV_W - s, :] * sh
    half = D_LRU // 2
    parts = []
    for hf in range(2):
        cs = slice(hf * half, (hf + 1) * half)
        parts.append(_lru_gates(xc[:, cs], _dot(xc[:, cs], wg_ref[hf]), ba_ref[:, cs], bx_ref[:, cs],
                                lam_ref[:, cs]))
    a, mult, ix = (jnp.concatenate([p[n] for p in parts], axis=1) for n in range(3))
    a3, b3 = _scan_groups(a, mult * ix)
    h = (a3 * h0_ref[...].reshape(a3.shape) + b3).reshape(rows, D_LRU)
    h_ref[...] = h
    z_sc[:, 0:D_LRU] = h * jax.nn.gelu(u_sc[:, O_GL:O_GL + D_LRU])

    tq = lax.broadcasted_iota(jnp.int32, (rows, HK), 0) % ls
    b = _cumsum_rows(u_sc[:, D_MAIN:D_U], tq, ls)
    bl = jnp.where(tq == ls - 1, b, 0.0)
    s = 1
    while s < ls:
        bl = bl + jnp.where(tq + s < ls, pltpu.roll(bl, rows - s, 0), 0.0)
        s *= 2
    k = u_sc[:, O_K:O_K + HK]
    qs_sc[...] = (u_sc[:, O_Q:O_Q + HK] * (GLA_DK ** -0.5)) * jnp.exp(b)
    ks_sc[...] = k * jnp.exp(-b)
    ke_sc[...] = k * jnp.exp(bl - b)
    el_sc[...] = jnp.exp(bl)

    lane_head = lax.broadcasted_iota(jnp.int32, (ls, HK), 1) // GLA_DK
    causal = ((lax.broadcasted_iota(jnp.int32, (GLA_HEADS * ls, ls), 0) % ls)
              >= lax.broadcasted_iota(jnp.int32, (GLA_HEADS * ls, ls), 1))

    def body(bi, carry):
        rs = pl.ds(pl.multiple_of(bi * ls, ls), ls)
        s_all = s0_ref[bi].reshape(HK, GLA_DV)
        o, s_new = _gla_chunk(qs_sc[rs, :], ks_sc[rs, :], ke_sc[rs, :], el_sc[rs, :][0:1],
                              u_sc[rs, O_V:O_V + HV], s_all, causal, lane_head, ls)
        for hd in range(GLA_HEADS):
            so_new[bi, hd] = s_new[hd]
            o_sc[rs, hd * GLA_DV:(hd + 1) * GLA_DV] = o[hd]
        return carry

    lax.fori_loop(0, nb, body, 0, unroll=16)

    gn = gn_ref[...]
    for hd in range(GLA_HEADS):
        vs = slice(hd * GLA_DV, (hd + 1) * GLA_DV)
        z_sc[:, D_LRU + hd * GLA_DV:D_LRU + (hd + 1) * GLA_DV] = _gla_output(
            o_sc[:, vs], u_sc[:, O_GO + hd * GLA_DV:O_GO + (hd + 1) * GLA_DV], gn)
    xo_ref[...] = x + _dot(z_sc[...], wout_ref[...])


def _mixer_dec_call(x, frame, h0, s0, s_prev, mix, l, *, nseq, nb, name):
    ls = SUBLANES
    assert nseq % nb == 0
    rows = nb * ls
    (vecs, w_in, wlr, wg2, convw, wg, wout) = mix
    rowspec = lambda w: pl.BlockSpec((rows, w), lambda i: (i, 0))
    sspec = pl.BlockSpec((None, nb, GLA_HEADS, GLA_DK, GLA_DV), lambda i: (l, i, 0, 0, 0))
    in_specs = [
        rowspec(D_MODEL), rowspec(D_LRU), rowspec(D_LRU), sspec,
        _layer_spec(vecs, l), _main_cols_spec(l), _layer_spec(wlr, l), _layer_spec(wg2, l),
        _layer_spec(convw, l), _layer_spec(wg, l), _whole_spec(wout),
    ]
    out_shape = [
        jax.ShapeDtypeStruct((nseq * ls, D_MODEL), F32),
        jax.ShapeDtypeStruct((nseq * ls, D_LRU), F32),
        jax.ShapeDtypeStruct((nseq * ls, D_LRU), F32),
    ]
    state = (nseq, GLA_HEADS, GLA_DK, GLA_DV)
    sblock = pl.BlockSpec((nb,) + state[1:], lambda i: (i, 0, 0, 0))
    n_prev = len(s_prev)
    if n_prev:
        out_shape.append(jax.ShapeDtypeStruct((n_prev + 1,) + state, F32))
        s_out_spec = pl.BlockSpec((n_prev + 1, nb) + state[1:], lambda i: (0, i, 0, 0, 0))
    else:
        out_shape.append(jax.ShapeDtypeStruct(state, F32))
        s_out_spec = sblock
    out_specs = [rowspec(D_MODEL), rowspec(D_LRU), rowspec(D_LRU), s_out_spec]
    args = [x, frame, h0, s0, vecs, w_in, wlr, wg2, convw, wg, wout]
    args += list(s_prev)
    in_specs += [sblock] * n_prev
    scratch = [
        pltpu.VMEM((rows, D_U), F32), pltpu.VMEM((rows, D_MODEL), F32),
        pltpu.VMEM((rows, HK), F32), pltpu.VMEM((rows, HK), F32), pltpu.VMEM((rows, HK), F32),
        pltpu.VMEM((rows, HK), F32), pltpu.VMEM((rows, HV), F32),
    ]
    return pl.pallas_call(
        functools.partial(_mixer_dec_kernel, nb=nb, n_prev=n_prev),
        grid=(nseq // nb,), in_specs=in_specs, out_specs=out_specs, out_shape=out_shape,
        scratch_shapes=scratch,
        compiler_params=pltpu.CompilerParams(dimension_semantics=("arbitrary",),
                                             vmem_limit_bytes=VMEM_LIMIT),
        name=name,
    )(*args)


def _block_diag_gates(wa, wx):
    per = LRU_BLOCKS // 2
    bw = wa.shape[-1]
    eye = jnp.eye(per, dtype=wa.dtype)

    def bd(w):
        return jnp.einsum("lncd,nm->lncmd", w, eye).reshape(w.shape[0], per * bw, per * bw)

    halves = [jnp.concatenate([bd(wa[:, hf * per:(hf + 1) * per]), bd(wx[:, hf * per:(hf + 1) * per])],
                              axis=-1) for hf in range(2)]
    return jnp.stack(halves, axis=1)


def kernel(x_prompt, x_sample, state_lru_h, state_lru_conv, state_gla_S, meta, norm_ffn1, w_ffn1_gu,
           w_ffn1_down, norm_mix, w_in, lru_conv_w, lru_conv_b, lru_wa, lru_ba, lru_wx, lru_bx,
           lru_lambda, gla_w_gate2, gla_b_gate, gla_norm, w_out, norm_ffn2, w_ffn2_gu, w_ffn2_down,
           norm_final):
    bp, lp, d = x_prompt.shape
    bs, ls, _ = x_sample.shape
    n_meta = meta.shape[0]
    depth = w_in.shape[0]
    assert d == D_MODEL and ls == SUBLANES and n_meta % SUBLANES == 0
    tp, ts = bp * lp, bs * ls
    tsm = ts + n_meta
    assert ts % n_meta == 0
    tm_p = 512 if tp % 512 == 0 else lp
    tm_s = tsm // 2 if (tsm // 2) % SUBLANES == 0 else tsm
    tc_p = 1024 if lp % 1024 == 0 else GLA_CHUNK
    nb_s = 32 if bs % 32 == 0 else bs
    tail = CONV_W - 1

    vecs = _pack_vectors(norm_mix, lru_ba, lru_bx, lru_lambda, lru_conv_b, gla_b_gate, gla_norm,
                         norm_ffn1, norm_ffn2, norm_final)
    w_in_b = w_in.astype(BF16)
    w_lr = jnp.pad(w_in[:, :, D_MAIN:], ((0, 0), (0, 0), (0, LANES - GLA_RANK))).astype(BF16)
    w_g2 = jnp.pad(gla_w_gate2, ((0, 0), (0, LANES - GLA_RANK), (0, 0))).astype(BF16)
    w_gates = _block_diag_gates(lru_wa, lru_wx).astype(BF16)
    frames_s = jnp.pad(state_lru_conv, ((0, 0), (0, 0), (SUBLANES - tail, 0), (0, 0)))
    frames_s = frames_s.reshape(depth, ts, D_LRU)
    h0_s = jnp.broadcast_to(state_lru_h[:, :, None, :], (depth, bs, ls, D_LRU)).reshape(depth, ts, D_LRU)
    zero_frame = jnp.zeros((1, SUBLANES, D_LRU), F32)
    zero_h = jnp.zeros((1, 1, D_LRU), F32)
    zero_s = jnp.zeros((1, GLA_HEADS, GLA_DK, GLA_DV), F32)

    xp = x_prompt.reshape(tp, d)
    xs = jnp.concatenate([x_sample.reshape(ts, d), meta.astype(F32)], axis=0)

    hs_p, convs_p, ss_p, hs_s, convs_s, ss_s = [], [], [], [], [], []
    for l in range(depth):
        xp1, xs1, w_out_l = _ffn_call(xp, xs, vecs, ROW_FFN1, w_ffn1_gu, w_ffn1_down, l, final=False,
                                      tm_a=tm_p, tm_b=tm_s, tf=256, name=f"ffn1_{l}", cast=(w_out,))
        mix = (vecs, w_in_b, w_lr, w_g2, lru_conv_w, w_gates, w_out_l)

        last = l == depth - 1
        xs2, h_s, xl_s, s_s = _mixer_dec_call(xs1, frames_s[l], h0_s[l], state_gla_S,
                                              ss_s if last else (), mix, l, nseq=bs, nb=nb_s,
                                              name=f"mix_s{l}")
        ss_s.append(s_s)
        hs_s.append(h_s.reshape(bs, ls, D_LRU)[:, ls - 1])
        convs_s.append(xl_s.reshape(bs, ls, D_LRU)[:, ls - tail:])

        xm2, h_m, s_m, tail_m = _mixer_seq_call(
            xs1, zero_frame, zero_h, zero_s, mix, l, nb=1, sl=n_meta, tc=n_meta, cg=n_meta,
            row_block0=ts // n_meta, reset_first=True, shared_state=False, name=f"mix_m{l}")
        xs2 = jnp.concatenate([xs2, xm2], axis=0)

        xp2, h_p, s_p, tail_p = _mixer_seq_call(
            xp1, tail_m, h_m, s_m, mix, l, nb=bp, sl=lp, tc=tc_p, cg=GLA_CHUNK, row_block0=0,
            reset_first=False, shared_state=True, name=f"mix_p{l}")
        hs_p.append(h_p[:, 0])
        convs_p.append(tail_p[:, SUBLANES - tail:])
        ss_p.append(s_p)

        xp, xs = _ffn_call(xp2, xs2, vecs, ROW_FFN2, w_ffn2_gu, w_ffn2_down, l, final=last, tm_a=tm_p,
                           tm_b=tm_s, tf=256, name=f"ffn2_{l}")

    s_stack = ss_s[-1] if depth > 1 else ss_s[0][None]
    return (xp.reshape(bp, lp, d), xs[:ts].reshape(bs, ls, d),
            jnp.stack(hs_p), jnp.stack(convs_p), jnp.stack(ss_p),
            jnp.stack(hs_s), jnp.stack(convs_s), s_stack)
```
